```python
import jax, jax.numpy as jnp
from jax import lax
import numpy as np

D_MODEL = 2048
BATCH = 4
SEQ = 4096
DEPTH = 1

NORM_EPS = 1e-6
MIX_WIDTH = D_MODEL
GLA_HEADS = 4
GLA_WIDTH = MIX_WIDTH // 2
GLA_DV = GLA_WIDTH // GLA_HEADS
GLA_DK = GLA_DV // 2
GLA_GATE_RANK = 16
GLA_TAU = 16.0
GLA_CHUNK = 64
DIL_HD = 128
DIL_WIDTH = MIX_WIDTH - GLA_WIDTH
DIL_HEADS = DIL_WIDTH // DIL_HD
DIL_PATTERNS = ((128, 1), (512, 4), (2048, 16))
DIL_BLOCK = 64
ROPE_THETA = 10000.0
NEG_INF = -1e30
PEER_HEADS = 8
PEER_NKEYS = 128
PEER_NEXPERTS = PEER_NKEYS * PEER_NKEYS
PEER_QDIM = 256
PEER_TOPK = 16
PEER_TOKEN_BLOCK = 128
IN_SIZES = (GLA_HEADS * GLA_DK, GLA_HEADS * GLA_DK, GLA_WIDTH, GLA_WIDTH,
            GLA_GATE_RANK, GLA_GATE_RANK, DIL_WIDTH, DIL_WIDTH, DIL_WIDTH)
IN_WIDTH = sum(IN_SIZES)

kernel_name = "hybrid_gla_dilated_peer_encoder_block"


def rms_norm(x, g):
    xf = x.astype(jnp.float32)
    y = xf * lax.rsqrt(jnp.mean(xf * xf, axis=-1, keepdims=True) + NORM_EPS)
    return (y * g.astype(jnp.float32)).astype(x.dtype)


def split_cols(t, sizes):
    outs, start = [], 0
    for s in sizes:
        outs.append(t[..., start:start + s])
        start += s
    return outs


def apply_rope(t, positions):
    hd = t.shape[-1]
    half = hd // 2
    inv_freq = jnp.power(ROPE_THETA, -jnp.arange(half, dtype=jnp.float32) * 2.0 / hd)
    ang = positions.astype(jnp.float32)[..., None] * inv_freq
    cos = jnp.cos(ang)[:, :, None, :]
    sin = jnp.sin(ang)[:, :, None, :]
    tf = t.astype(jnp.float32)
    t1, t2 = tf[..., :half], tf[..., half:]
    return jnp.concatenate([t1 * cos - t2 * sin, t1 * sin + t2 * cos], axis=-1).astype(t.dtype)


def gla_chunked(q, k, v, log_a):
    B, S, H, dk = q.shape
    dv = v.shape[-1]
    C = GLA_CHUNK
    n = S // C

    def to_chunks(t):
        return t.reshape(B, n, C, H, t.shape[-1]).transpose(1, 0, 3, 2, 4)

    qc, kc, vc = to_chunks(q), to_chunks(k), to_chunks(v.astype(jnp.float32))
    b = jnp.cumsum(to_chunks(log_a).astype(jnp.float32), axis=-2)
    b_last = b[..., -1:, :]
    q_dec = qc.astype(jnp.float32) * jnp.exp(b)
    k_inv = kc.astype(jnp.float32) * jnp.exp(-b)
    k_to_end = kc.astype(jnp.float32) * jnp.exp(b_last - b)
    lower = jnp.tril(jnp.ones((C, C), dtype=bool))
    attn = jnp.where(lower, jnp.einsum('nbhid,nbhjd->nbhij', q_dec, k_inv), 0.0)
    o_intra = jnp.einsum('nbhij,nbhjv->nbhiv', attn, vc)

    def step(state, inp):
        kt, vt, decay = inp
        new = state * decay[:, :, 0, :, None] + jnp.einsum('bhcd,bhcv->bhdv', kt, vt)
        return new, state

    state0 = jnp.zeros((B, H, dk, dv), jnp.float32)
    _, states = lax.scan(step, state0, (k_to_end, vc, jnp.exp(b_last)))
    o_inter = jnp.einsum('nbhid,nbhdv->nbhiv', q_dec, states)
    return (o_intra + o_inter).transpose(1, 0, 3, 2, 4).reshape(B, S, H, dv)


def gla_mixer(q, k, v, r, gz_f, gz_b, w_gate_f, b_gate_f, w_gate_b, b_gate_b, g_gla_out):
    B, S, _ = q.shape
    q = q.reshape(B, S, GLA_HEADS, GLA_DK) * (GLA_DK ** -0.5)
    k = k.reshape(B, S, GLA_HEADS, GLA_DK)
    v = v.reshape(B, S, GLA_HEADS, GLA_DV)

    def log_gate(gz, w, bias):
        z = gz.astype(jnp.float32) @ w.astype(jnp.float32) + bias.astype(jnp.float32)
        return (jax.nn.log_sigmoid(z) / GLA_TAU).reshape(B, S, GLA_HEADS, GLA_DK)

    la_f = log_gate(gz_f, w_gate_f, b_gate_f)
    la_b = log_gate(gz_b, w_gate_b, b_gate_b)
    o_f = gla_chunked(q, k, v, la_f)
    flip = lambda t: jnp.flip(t, axis=1)
    o_b = flip(gla_chunked(flip(q), flip(k), flip(v), flip(la_b)))
    o = rms_norm(o_f + o_b, g_gla_out)
    return o.reshape(B, S, GLA_WIDTH).astype(r.dtype) * jax.nn.silu(r)


def dilated_band_attention(q, k, v, dilation, half):
    B, S, H, hd = q.shape
    M = S // dilation
    N = B * dilation

    def split(t):
        return t.reshape(B, M, dilation, H, hd).transpose(0, 2, 1, 3, 4).reshape(N, M, H, hd)

    qs, ks, vs = split(q), split(k), split(v)
    Qb = DIL_BLOCK
    nb = -(-M // Qb)
    Mp = nb * Qb
    L = Qb + 2 * half
    qs = jnp.pad(qs, ((0, 0), (0, Mp - M), (0, 0), (0, 0)))
    pad_kv = ((0, 0), (half, Mp - M + half), (0, 0), (0, 0))
    ks = jnp.pad(ks, pad_kv)
    vs = jnp.pad(vs, pad_kv)
    kidx = jnp.arange(nb)[:, None] * Qb + jnp.arange(L)[None, :]
    kb = ks[:, kidx]
    vb = vs[:, kidx].astype(jnp.float32)
    qb = qs.reshape(N, nb, Qb, H, hd)
    s = jnp.einsum('nbqhd,nbkhd->nbhqk', qb, kb).astype(jnp.float32) * (hd ** -0.5)
    qpos = jnp.arange(nb)[:, None] * Qb + jnp.arange(Qb)[None, :]
    kpos = kidx - half
    rel = kpos[:, None, :] - qpos[:, :, None]
    valid = (jnp.abs(rel) <= half) & (kpos[:, None, :] >= 0) & (kpos[:, None, :] < M)
    s = jnp.where(valid[None, :, None], s, NEG_INF)
    m = jnp.max(s, axis=-1, keepdims=True)
    p = jnp.exp(s - m)
    den = jnp.sum(p, axis=-1, keepdims=True)
    o = jnp.einsum('nbhqk,nbkhd->nbqhd', p / den, vb)
    lse = (m + jnp.log(den))[..., 0].transpose(0, 1, 3, 2)
    o = o.reshape(N, Mp, H, hd)[:, :M]
    lse = lse.reshape(N, Mp, H)[:, :M]

    def merge(t):
        return t.reshape(B, dilation, M, *t.shape[2:]).swapaxes(1, 2).reshape(B, S, *t.shape[2:])

    return merge(o), merge(lse)


def dilated_mixer(q, k, v, positions):
    B, S, _ = q.shape
    q = apply_rope(q.reshape(B, S, DIL_HEADS, DIL_HD), positions)
    k = apply_rope(k.reshape(B, S, DIL_HEADS, DIL_HD), positions)
    v = v.reshape(B, S, DIL_HEADS, DIL_HD)
    outs, lses = [], []
    for window, dilation in DIL_PATTERNS:
        o, l = dilated_band_attention(q, k, v, dilation, window // (2 * dilation))
        outs.append(o)
        lses.append(l)
    w = jax.nn.softmax(jnp.stack(lses, axis=0), axis=0)
    o = jnp.sum(w[..., None] * jnp.stack(outs, axis=0), axis=0)
    return o.reshape(B, S, DIL_WIDTH).astype(q.dtype)


def peer_ffn(h, w_peer_q, peer_sub_keys, peer_u, peer_v):
    B, S, D = h.shape
    T = B * S
    x = h.reshape(T, D)
    q = (x @ w_peer_q).reshape(T, PEER_HEADS, 2, PEER_QDIM // 2)
    scores = jnp.einsum('thsd,hskd->thsk', q, peer_sub_keys).astype(jnp.float32)
    top_s, top_i = lax.top_k(scores, PEER_TOPK)
    cand_s = (top_s[:, :, 0, :, None] + top_s[:, :, 1, None, :]).reshape(T, PEER_HEADS, PEER_TOPK * PEER_TOPK)
    cand_i = (top_i[:, :, 0, :, None] * PEER_NKEYS + top_i[:, :, 1, None, :]).reshape(T, PEER_HEADS, PEER_TOPK * PEER_TOPK)
    best_s, pos = lax.top_k(cand_s, PEER_TOPK)
    idx = jnp.take_along_axis(cand_i, pos, axis=-1)
    gate = jax.nn.softmax(best_s, axis=-1)
    Tb = PEER_TOKEN_BLOCK
    nblk = T // Tb
    HK = PEER_HEADS * PEER_TOPK

    def block(args):
        xb, ib, gb = args
        a = jnp.einsum('tkd,td->tk', peer_u[ib], xb).astype(jnp.float32)
        wgt = (gb * jax.nn.gelu(a, approximate=False)).astype(xb.dtype)
        return jnp.einsum('tk,tkd->td', wgt, peer_v[ib])

    out = lax.map(block, (x.reshape(nblk, Tb, D), idx.reshape(nblk, Tb, HK), gate.reshape(nblk, Tb, HK)))
    return out.reshape(B, S, D)


def setup_inputs(seed: int = 0) -> dict:
    key = jax.random.key(seed)
    ks = jax.random.split(key, 20)
    nrm = lambda k, shape, std: jax.random.normal(k, shape, jnp.float32) * std
    L, D = DEPTH, D_MODEL
    return {
        "x": nrm(ks[0], (BATCH, SEQ, D), 1.0),
        "c": nrm(ks[1], (BATCH, D), 1.0),
        "positions": jnp.broadcast_to(jnp.arange(SEQ, dtype=jnp.int32), (BATCH, SEQ)),
        "w_ada": nrm(ks[2], (L, D, 6 * D), 0.5 * D ** -0.5),
        "b_ada": nrm(ks[3], (L, 6 * D), 0.02),
        "g_norm_mix": 1.0 + nrm(ks[4], (L, D), 0.02),
        "w_in": nrm(ks[5], (L, D, IN_WIDTH), D ** -0.5),
        "w_gate_f": nrm(ks[6], (L, GLA_GATE_RANK, GLA_HEADS * GLA_DK), GLA_GATE_RANK ** -0.5),
        "b_gate_f": nrm(ks[7], (L, GLA_HEADS * GLA_DK), 0.1),
        "w_gate_b": nrm(ks[8], (L, GLA_GATE_RANK, GLA_HEADS * GLA_DK), GLA_GATE_RANK ** -0.5),
        "b_gate_b": nrm(ks[9], (L, GLA_HEADS * GLA_DK), 0.1),
        "g_gla_out": 1.0 + nrm(ks[10], (L, GLA_HEADS, GLA_DV), 0.02),
        "w_out": nrm(ks[11], (L, MIX_WIDTH, D), MIX_WIDTH ** -0.5),
        "g_norm_ffn": 1.0 + nrm(ks[12], (L, D), 0.02),
        "w_peer_q": nrm(ks[13], (L, D, PEER_HEADS * PEER_QDIM), D ** -0.5),
        "peer_sub_keys": nrm(ks[14], (L, PEER_HEADS, 2, PEER_NKEYS, PEER_QDIM // 2), (PEER_QDIM // 2) ** -0.5),
        "peer_u": nrm(ks[15], (L, PEER_NEXPERTS, D), D ** -0.5),
        "peer_v": nrm(ks[16], (L, PEER_NEXPERTS, D), 0.5),
        "g_final": 1.0 + nrm(ks[17], (D,), 0.02),
    }


def reference(x, c, positions, w_ada, b_ada, g_norm_mix, w_in, w_gate_f, b_gate_f, w_gate_b, b_gate_b,
              g_gla_out, w_out, g_norm_ffn, w_peer_q, peer_sub_keys, peer_u, peer_v, g_final):
    for l in range(DEPTH):
        mod = jax.nn.silu(c) @ w_ada[l] + b_ada[l]
        sh_a, sc_a, ga_a, sh_f, sc_f, ga_f = [m[:, None, :] for m in jnp.split(mod, 6, axis=-1)]
        h = rms_norm(x, g_norm_mix[l]) * (1.0 + sc_a) + sh_a
        proj = h @ w_in[l]
        gq, gk, gv, gr, gz_f, gz_b, dq, dk, dv = split_cols(proj, IN_SIZES)
        o_gla = gla_mixer(gq, gk, gv, gr, gz_f, gz_b, w_gate_f[l], b_gate_f[l], w_gate_b[l], b_gate_b[l], g_gla_out[l])
        o_dil = dilated_mixer(dq, dk, dv, positions)
        mixed = jnp.concatenate([o_gla, o_dil], axis=-1) @ w_out[l]
        x = x + ga_a * mixed
        h = rms_norm(x, g_norm_ffn[l]) * (1.0 + sc_f) + sh_f
        x = x + ga_f * peer_ffn(h, w_peer_q[l], peer_sub_keys[l], peer_u[l], peer_v[l])
    return rms_norm(x, g_final)
```

```python
import functools
import math

import numpy as np
import jax
import jax.numpy as jnp
from jax import lax
from jax.experimental import pallas as pl
from jax.experimental.pallas import tpu as pltpu

F32 = jnp.float32
BF16 = jnp.bfloat16
HIGHEST = lax.Precision.HIGHEST

NORM_EPS = 1e-6
GLA_HEADS = 4
GLA_DK = 128
GLA_DV = 256
GLA_GATE_RANK = 16
GLA_TAU = 16.0
GLA_CHUNK = 64
DIL_HD = 128
DIL_HEADS = 8
DIL_PATTERNS = ((128, 1), (512, 4), (2048, 16))
ROPE_THETA = 10000.0
NEG_INF = -1e30
PEER_HEADS = 8
PEER_NKEYS = 128
PEER_TOPK = 16
PEER_HALF = 128

LANE = 128
MIB = 1024 * 1024

_COL_GQ, _COL_GK, _COL_GV, _COL_GR, _COL_DQ, _COL_DK, _COL_DV = 0, 4, 8, 16, 24, 32, 40
_PROJ_W = 48 * LANE

_NT = (((1,), (1,)), ((), ()))
_TN = (((0,), (0,)), ((), ()))


def _params(sem, vmem_mib):
    return pltpu.CompilerParams(dimension_semantics=sem, vmem_limit_bytes=vmem_mib * MIB)


def _rms(x, g):
    return x * lax.rsqrt(jnp.mean(x * x, axis=-1, keepdims=True) + NORM_EPS) * g


def _silu(x):
    return x / (1.0 + jnp.exp(-x))


def _ada_kernel(c_ref, w_ref, b_ref, o_ref):
    s = _silu(c_ref[...]).astype(BF16)
    o_ref[...] = jnp.dot(s, w_ref[...].astype(BF16), preferred_element_type=F32) + b_ref[...]


def _ada(c, w, b):
    B, D = c.shape
    N = w.shape[1]
    tn = 1024
    cp = jnp.zeros((8, D), F32).at[:B].set(c)
    out = pl.pallas_call(
        _ada_kernel,
        grid=(N // tn,),
        in_specs=[pl.BlockSpec((8, D), lambda j: (0, 0)),
                  pl.BlockSpec((D, tn), lambda j: (0, j)),
                  pl.BlockSpec((1, tn), lambda j: (0, j))],
        out_specs=pl.BlockSpec((8, tn), lambda j: (0, j)),
        out_shape=jax.ShapeDtypeStruct((8, N), F32),
        compiler_params=_params(("parallel",), 40),
        name="ada",
    )(cp, w, b.reshape(1, N))
    return out[:B]


def _inproj_kernel(x_ref, g_ref, sc_ref, sh_ref, w_ref, wz_ref, o_ref, z_ref, h_scr):
    @pl.when(pl.program_id(1) == 0)
    def _():
        h = _rms(x_ref[...], g_ref[...]) * (1.0 + sc_ref[0]) + sh_ref[0]
        hb = h.astype(BF16)
        h_scr[...] = hb
        z_ref[...] = jnp.dot(hb, wz_ref[...], preferred_element_type=F32)

    o_ref[...] = jnp.dot(h_scr[...], w_ref[...], preferred_element_type=F32)


def _inproj(x2, g, mod3, w_main, w_z, S):
    T, D = x2.shape
    tm, tn = 1024, 768
    per_b = S // tm
    return pl.pallas_call(
        _inproj_kernel,
        grid=(T // tm, _PROJ_W // tn),
        in_specs=[pl.BlockSpec((tm, D), lambda i, j: (i, 0)),
                  pl.BlockSpec((1, D), lambda i, j: (0, 0)),
                  pl.BlockSpec((1, 1, D), lambda i, j: ((i // per_b) * 6 + 1, 0, 0)),
                  pl.BlockSpec((1, 1, D), lambda i, j: ((i // per_b) * 6 + 0, 0, 0)),
                  pl.BlockSpec((D, tn), lambda i, j: (0, j)),
                  pl.BlockSpec((D, LANE), lambda i, j: (0, 0))],
        out_specs=[pl.BlockSpec((tm, tn), lambda i, j: (i, j)),
                   pl.BlockSpec((tm, LANE), lambda i, j: (i, 0))],
        out_shape=[jax.ShapeDtypeStruct((T, _PROJ_W), F32),
                   jax.ShapeDtypeStruct((T, LANE), F32)],
        scratch_shapes=[pltpu.VMEM((tm, D), BF16)],
        compiler_params=_params(("parallel", "arbitrary"), 48),
        name="inproj",
    )(x2, g, mod3, mod3, w_main, w_z)


def _rope_kernel(pos_ref, f_ref, sg_ref, cs_ref, sn_ref):
    ang = pos_ref[...].astype(F32) * f_ref[...]
    cs_ref[...] = jnp.cos(ang)
    sn_ref[...] = jnp.sin(ang) * sg_ref[...]


def _rope_tables(pos_col):
    T = pos_col.shape[0]
    half = DIL_HD // 2
    inv = jnp.power(ROPE_THETA, -jnp.arange(half, dtype=F32) * 2.0 / DIL_HD)
    freq = jnp.concatenate([inv, inv]).reshape(1, DIL_HD)
    sign = jnp.concatenate([-jnp.ones((half,), F32), jnp.ones((half,), F32)]).reshape(1, DIL_HD)
    tm = 1024
    return pl.pallas_call(
        _rope_kernel,
        grid=(T // tm,),
        in_specs=[pl.BlockSpec((tm, 1), lambda i: (i, 0)),
                  pl.BlockSpec((1, DIL_HD), lambda i: (0, 0)),
                  pl.BlockSpec((1, DIL_HD), lambda i: (0, 0))],
        out_specs=[pl.BlockSpec((tm, DIL_HD), lambda i: (i, 0)),
                   pl.BlockSpec((tm, DIL_HD), lambda i: (i, 0))],
        out_shape=[jax.ShapeDtypeStruct((T, DIL_HD), F32)] * 2,
        compiler_params=_params(("parallel",), 32),
        name="rope",
    )(pos_col, freq, sign)


def _gla_kernel(q_ref, k_ref, v_ref, r_ref, z_ref, wgf_ref, bgf_ref, wgb_ref, bgb_ref, g_ref,
                o_ref, la_scr, o_scr):
    S = q_ref.shape[0]
    C = GLA_CHUNK
    n = S // C
    scale = GLA_DK ** -0.5
    row = lax.broadcasted_iota(jnp.int32, (C, C), 0)
    col = lax.broadcasted_iota(jnp.int32, (C, C), 1)

    def log_gate(w_ref, b_ref):
        zz = jnp.dot(z_ref[...], w_ref[0], precision=HIGHEST, preferred_element_type=F32) + b_ref[0]
        return (jnp.minimum(zz, 0.0) - jnp.log(1.0 + jnp.exp(-jnp.abs(zz)))) * (1.0 / GLA_TAU)

    def run(fwd):
        keep = (col <= row) if fwd else (col >= row)
        tri = keep.astype(F32)

        def body(i, st_t):
            c = i if fwd else n - 1 - i
            sl = pl.ds(pl.multiple_of(c * C, C), C)
            cum = jnp.dot(tri, la_scr[sl, :], precision=HIGHEST, preferred_element_type=F32)
            tot = cum[C - 1:C, :] if fwd else cum[0:1, :]
            kk = k_ref[sl, :]
            qd = (q_ref[sl, :] * scale * jnp.exp(cum)).astype(BF16)
            ki = (kk * jnp.exp(-cum)).astype(BF16)
            kte = (kk * jnp.exp(tot - cum)).astype(BF16)
            vb = v_ref[sl, :].astype(BF16)
            attn = lax.dot_general(qd, ki, _NT, preferred_element_type=F32)
            attn = jnp.where(keep, attn, 0.0).astype(BF16)
            o = jnp.dot(attn, vb, preferred_element_type=F32)
            o = o + lax.dot_general(qd, st_t.astype(BF16), _NT, preferred_element_type=F32)
            if fwd:
                o_scr[sl, :] = o
            else:
                o_scr[sl, :] = o_scr[sl, :] + o
            upd = lax.dot_general(vb, kte, _TN, preferred_element_type=F32)
            return st_t * jnp.exp(tot) + upd

        lax.fori_loop(0, n, body, jnp.zeros((GLA_DV, GLA_DK), F32))

    la_scr[...] = log_gate(wgf_ref, bgf_ref)
    run(True)
    la_scr[...] = log_gate(wgb_ref, bgb_ref)
    run(False)
    y = _rms(o_scr[...], g_ref[0])
    o_ref[...] = y * _silu(r_ref[...])


def _gla(proj, gz, wgf, bgf, wgb, bgb, g_out, B, S):
    T = proj.shape[0]
    H = GLA_HEADS
    return pl.pallas_call(
        _gla_kernel,
        grid=(B, H),
        in_specs=[pl.BlockSpec((S, GLA_DK), lambda b, h: (b, _COL_GQ + h)),
                  pl.BlockSpec((S, GLA_DK), lambda b, h: (b, _COL_GK + h)),
                  pl.BlockSpec((S, GLA_DV), lambda b, h: (b, _COL_GV // 2 + h)),
                  pl.BlockSpec((S, GLA_DV), lambda b, h: (b, _COL_GR // 2 + h)),
                  pl.BlockSpec((S, LANE), lambda b, h: (b, 0)),
                  pl.BlockSpec((1, LANE, GLA_DK), lambda b, h: (h, 0, 0)),
                  pl.BlockSpec((1, 1, GLA_DK), lambda b, h: (h, 0, 0)),
                  pl.BlockSpec((1, LANE, GLA_DK), lambda b, h: (h, 0, 0)),
                  pl.BlockSpec((1, 1, GLA_DK), lambda b, h: (h, 0, 0)),
                  pl.BlockSpec((1, 1, GLA_DV), lambda b, h: (h, 0, 0))],
        out_specs=pl.BlockSpec((S, GLA_DV), lambda b, h: (b, h)),
        out_shape=jax.ShapeDtypeStruct((T, GLA_HEADS * GLA_DV), F32),
        scratch_shapes=[pltpu.VMEM((S, GLA_DK), F32), pltpu.VMEM((S, GLA_DV), F32)],
        compiler_params=_params(("parallel", "parallel"), 56),
        name="gla",
    )(proj, proj, proj, proj, gz, wgf, bgf, wgb, bgb, g_out)


_DIL_QB = 256
_DIL_REACH = max(w // 2 for w, _ in DIL_PATTERNS) // _DIL_QB


def _dil_bias():
    qb = _DIL_QB
    d = np.arange(-_DIL_REACH, _DIL_REACH + 1)[:, None, None] * qb
    delta = d + np.arange(qb)[None, None, :] - np.arange(qb)[None, :, None]
    mult = np.zeros(delta.shape, np.float64)
    for window, dilation in DIL_PATTERNS:
        half = window // (2 * dilation)
        mult += ((delta % dilation) == 0) & (np.abs(delta) <= half * dilation)
    with np.errstate(divide="ignore"):
        bias = np.where(mult > 0, np.log(np.maximum(mult, 1.0)), NEG_INF)
    return jnp.asarray(bias, F32)


def _dil_kernel(q_ref, k_ref, v_ref, cs_ref, sn_ref, bias_ref, o_ref, kr_scr, vb_scr):
    qi = pl.program_id(2)
    nq = pl.num_programs(2)
    QB = _DIL_QB
    half = DIL_HD // 2

    @pl.when(qi == 0)
    def _():
        k = k_ref[...]
        kr_scr[...] = (k * cs_ref[...] + pltpu.roll(k, half, 1) * sn_ref[...]).astype(BF16)
        vb_scr[...] = v_ref[...].astype(BF16)

    r0 = pl.multiple_of(qi * QB, QB)
    q = q_ref[...]
    qr = q * cs_ref[pl.ds(r0, QB), :] + pltpu.roll(q, half, 1) * sn_ref[pl.ds(r0, QB), :]
    qb = (qr * (DIL_HD ** -0.5)).astype(BF16)

    m = jnp.full((QB, 1), NEG_INF, F32)
    l = jnp.zeros((QB, 1), F32)
    acc = jnp.zeros((QB, DIL_HD), F32)
    order = [0] + [s * d for d in range(1, _DIL_REACH + 1) for s in (-1, 1)]
    for d in order:
        kb = qi + d
        valid = jnp.logical_and(kb >= 0, kb < nq)
        k0 = pl.multiple_of(jnp.clip(kb, 0, nq - 1) * QB, QB)
        s = lax.dot_general(qb, kr_scr[pl.ds(k0, QB), :], _NT, preferred_element_type=F32)
        s = jnp.where(valid, s + bias_ref[d + _DIL_REACH], NEG_INF)
        m_new = jnp.maximum(m, jnp.max(s, axis=-1, keepdims=True))
        alpha = jnp.exp(m - m_new)
        p = jnp.exp(s - m_new)
        l = alpha * l + jnp.sum(p, axis=-1, keepdims=True)
        acc = alpha * acc + jnp.dot(p.astype(BF16), vb_scr[pl.ds(k0, QB), :],
                                    preferred_element_type=F32)
        m = m_new
    o_ref[...] = acc / l


def _dil(proj, cs, sn, B, S):
    T = proj.shape[0]
    QB = _DIL_QB
    nq = S // QB
    nb = 2 * _DIL_REACH + 1
    return pl.pallas_call(
        _dil_kernel,
        grid=(B, DIL_HEADS, nq),
        in_specs=[pl.BlockSpec((QB, DIL_HD), lambda b, h, i: (b * nq + i, _COL_DQ + h)),
                  pl.BlockSpec((S, DIL_HD), lambda b, h, i: (b, _COL_DK + h)),
                  pl.BlockSpec((S, DIL_HD), lambda b, h, i: (b, _COL_DV + h)),
                  pl.BlockSpec((S, DIL_HD), lambda b, h, i: (b, 0)),
                  pl.BlockSpec((S, DIL_HD), lambda b, h, i: (b, 0)),
                  pl.BlockSpec((nb, QB, QB), lambda b, h, i: (0, 0, 0))],
        out_specs=pl.BlockSpec((QB, DIL_HD), lambda b, h, i: (b * nq + i, h)),
        out_shape=jax.ShapeDtypeStruct((T, DIL_HEADS * DIL_HD), F32),
        scratch_shapes=[pltpu.VMEM((S, DIL_HD), BF16), pltpu.VMEM((S, DIL_HD), BF16)],
        compiler_params=_params(("parallel", "parallel", "arbitrary"), 48),
        name="dil",
    )(proj, proj, proj, cs, sn, _dil_bias())


def _outproj_kernel(og_ref, od_ref, w_ref, x_ref, ga_ref, o_ref):
    kg = og_ref.shape[1]
    mixed = jnp.dot(og_ref[...].astype(BF16), w_ref[:kg, :], preferred_element_type=F32)
    mixed = mixed + jnp.dot(od_ref[...].astype(BF16), w_ref[kg:, :], preferred_element_type=F32)
    o_ref[...] = x_ref[...] + ga_ref[0] * mixed


def _outproj(o_gla, o_dil, w_out, x2, mod3, S):
    T, D = x2.shape
    tm = 512
    per_b = S // tm
    kg, kd = o_gla.shape[1], o_dil.shape[1]
    return pl.pallas_call(
        _outproj_kernel,
        grid=(T // tm,),
        in_specs=[pl.BlockSpec((tm, kg), lambda i: (i, 0)),
                  pl.BlockSpec((tm, kd), lambda i: (i, 0)),
                  pl.BlockSpec((kg + kd, D), lambda i: (0, 0)),
                  pl.BlockSpec((tm, D), lambda i: (i, 0)),
                  pl.BlockSpec((1, 1, D), lambda i: ((i // per_b) * 6 + 2, 0, 0))],
        out_specs=pl.BlockSpec((tm, D), lambda i: (i, 0)),
        out_shape=jax.ShapeDtypeStruct((T, D), F32),
        compiler_params=_params(("parallel",), 48),
        name="outproj",
    )(o_gla, o_dil, w_out, x2, mod3)


def _pq_kernel(x_ref, g_ref, sc_ref, sh_ref, w_ref, o_ref):
    h = _rms(x_ref[...], g_ref[...]) * (1.0 + sc_ref[0]) + sh_ref[0]
    o_ref[...] = jnp.dot(h.astype(BF16), w_ref[...], preferred_element_type=F32)


def _pq(x1, g, mod3, wq, S):
    T, D = x1.shape
    N = wq.shape[1]
    tm = 512
    per_b = S // tm
    return pl.pallas_call(
        _pq_kernel,
        grid=(T // tm,),
        in_specs=[pl.BlockSpec((tm, D), lambda i: (i, 0)),
                  pl.BlockSpec((1, D), lambda i: (0, 0)),
                  pl.BlockSpec((1, 1, D), lambda i: ((i // per_b) * 6 + 4, 0, 0)),
                  pl.BlockSpec((1, 1, D), lambda i: ((i // per_b) * 6 + 3, 0, 0)),
                  pl.BlockSpec((D, N), lambda i: (0, 0))],
        out_specs=pl.BlockSpec((tm, N), lambda i: (i, 0)),
        out_shape=jax.ShapeDtypeStruct((T, N), F32),
        compiler_params=_params(("parallel",), 48),
        name="pq",
    )(x1, g, mod3, mod3, wq)


def _top_rows(s, k, payload=None):
    n_rows = s.shape[0]
    rid = lax.broadcasted_iota(jnp.int32, s.shape, 0)
    vals, picks = [], []
    for _ in range(k):
        m = jnp.max(s, axis=0, keepdims=True)
        pos = jnp.min(jnp.where(s == m, rid, n_rows), axis=0, keepdims=True)
        hit = rid == pos
        vals.append(m)
        if payload is None:
            picks.append(pos)
        else:
            picks.append(jnp.sum(jnp.where(hit, payload, 0), axis=0, keepdims=True))
        s = jnp.where(hit, -jnp.inf, s)
    return jnp.concatenate(vals, axis=0), jnp.concatenate(picks, axis=0)


def _topk_kernel(q_ref, keys_ref, idx_ref, gate_ref):
    K = PEER_TOPK
    for h in range(PEER_HEADS):
        tops = []
        for half in range(2):
            c0 = (h * 2 + half) * PEER_HALF
            qh = q_ref[:, c0:c0 + PEER_HALF].astype(BF16)
            sc = lax.dot_general(keys_ref[h, half], qh, _NT, preferred_element_type=F32)
            tops.append(_top_rows(sc, K))
        (s0, i0), (s1, i1) = tops
        cand_s = jnp.concatenate([s0[i:i + 1] + s1 for i in range(K)], axis=0)
        cand_i = jnp.concatenate([i0[i:i + 1] * PEER_NKEYS + i1 for i in range(K)], axis=0)
        best, idx = _top_rows(cand_s, K, payload=cand_i)
        e = jnp.exp(best - best[0:1])
        gate = e / jnp.sum(e, axis=0, keepdims=True)
        idx_ref[h * K:(h + 1) * K, :] = idx
        gate_ref[h * K:(h + 1) * K, :] = gate


def _topk(qp, keys_bf):
    T, N = qp.shape
    tt = 256
    HK = PEER_HEADS * PEER_TOPK
    return pl.pallas_call(
        _topk_kernel,
        grid=(T // tt,),
        in_specs=[pl.BlockSpec((tt, N), lambda i: (i, 0)),
                  pl.BlockSpec(keys_bf.shape, lambda i: (0, 0, 0, 0))],
        out_specs=[pl.BlockSpec((HK, tt), lambda i: (0, i)),
                   pl.BlockSpec((HK, tt), lambda i: (0, i))],
        out_shape=[jax.ShapeDtypeStruct((HK, T), jnp.int32),
                   jax.ShapeDtypeStruct((HK, T), F32)],
        compiler_params=_params(("parallel",), 32),
        name="topk",
    )(qp, keys_bf)


_PEER_TB = 128
_PEER_SUB = 8


def _peer_kernel(idx_hbm, gate_ref, x1_ref, gn_ref, sc_ref, sh_ref, ga_ref, gf_ref, u_hbm, v_hbm,
                 o_ref, idx_smem, ubuf, vbuf, h_scr, acc_scr, stage_scr, sem_i, sem_u, sem_v):
    HK = PEER_HEADS * PEER_TOPK
    TB, SUB = _PEER_TB, _PEER_SUB
    R = SUB * HK
    nsub = TB // SUB
    D = x1_ref.shape[1]
    nchunk = D // LANE
    i = pl.program_id(0)

    idx_copy = pltpu.make_async_copy(idx_hbm.at[pl.ds(i * (TB * HK), TB * HK)], idx_smem, sem_i)
    idx_copy.start()
    x1 = x1_ref[...]
    h_scr[...] = _rms(x1, gn_ref[...]) * (1.0 + sc_ref[0]) + sh_ref[0]
    idx_copy.wait()

    def issue(j, slot):
        def one(r, carry):
            e = idx_smem[j * R + r]
            pltpu.make_async_copy(u_hbm.at[pl.ds(e, 1), :], ubuf.at[slot, pl.ds(r, 1), :],
                                  sem_u.at[slot]).start()
            pltpu.make_async_copy(v_hbm.at[pl.ds(e, 1), :], vbuf.at[slot, pl.ds(r, 1), :],
                                  sem_v.at[slot]).start()
            return carry
        lax.fori_loop(0, R, one, 0, unroll=8)

    def wait(slot):
        pltpu.make_async_copy(u_hbm.at[pl.ds(0, R), :], ubuf.at[slot], sem_u.at[slot]).wait()
        pltpu.make_async_copy(v_hbm.at[pl.ds(0, R), :], vbuf.at[slot], sem_v.at[slot]).wait()

    lane = lax.broadcasted_iota(jnp.int32, (HK, TB), 1)
    issue(0, 0)

    def sub(j, carry):
        slot = lax.rem(j, 2)

        @pl.when(j + 1 < nsub)
        def _():
            issue(j + 1, 1 - slot)

        wait(slot)
        for t in range(SUB):
            tok = j * SUB + t
            xt = h_scr[pl.ds(tok, 1), :]
            rows = pl.ds(t * HK, HK)
            part = ubuf[slot, rows, 0:LANE] * xt[:, 0:LANE]
            for c in range(1, nchunk):
                part = part + ubuf[slot, rows, c * LANE:(c + 1) * LANE] * xt[:, c * LANE:(c + 1) * LANE]
            a = jnp.sum(part, axis=1, keepdims=True)
            g = jnp.sum(jnp.where(lane == tok, gate_ref[...], 0.0), axis=1, keepdims=True)
            wgt = g * (0.5 * a * (1.0 + lax.erf(a * (2.0 ** -0.5))))
            for c in range(nchunk):
                cs = slice(c * LANE, (c + 1) * LANE)
                stage_scr[t:t + 1, cs] = jnp.sum(wgt * vbuf[slot, rows, cs], axis=0, keepdims=True)
        acc_scr[pl.ds(pl.multiple_of(j * SUB, SUB), SUB), :] = stage_scr[...]
        return carry

    lax.fori_loop(0, nsub, sub, 0)
    y = x1 + ga_ref[0] * acc_scr[...]
    o_ref[...] = _rms(y, gf_ref[...])


def _peer(idx_flat, gate_t, x1, g_norm, mod3, g_final, peer_u, peer_v, S):
    T, D = x1.shape
    HK = PEER_HEADS * PEER_TOPK
    TB, SUB = _PEER_TB, _PEER_SUB
    per_b = S // TB
    row = lambda i: (0, 0)
    modrow = lambda k: (lambda i: ((i // per_b) * 6 + k, 0, 0))
    return pl.pallas_call(
        _peer_kernel,
        grid=(T // TB,),
        in_specs=[pl.BlockSpec(memory_space=pl.ANY),
                  pl.BlockSpec((HK, TB), lambda i: (0, i)),
                  pl.BlockSpec((TB, D), lambda i: (i, 0)),
                  pl.BlockSpec((1, D), row),
                  pl.BlockSpec((1, 1, D), modrow(4)),
                  pl.BlockSpec((1, 1, D), modrow(3)),
                  pl.BlockSpec((1, 1, D), modrow(5)),
                  pl.BlockSpec((1, D), row),
                  pl.BlockSpec(memory_space=pl.ANY),
                  pl.BlockSpec(memory_space=pl.ANY)],
        out_specs=pl.BlockSpec((TB, D), lambda i: (i, 0)),
        out_shape=jax.ShapeDtypeStruct((T, D), F32),
        scratch_shapes=[pltpu.SMEM((TB * HK,), jnp.int32),
                        pltpu.VMEM((2, SUB * HK, D), F32),
                        pltpu.VMEM((2, SUB * HK, D), F32),
                        pltpu.VMEM((TB, D), F32),
                        pltpu.VMEM((TB, D), F32),
                        pltpu.VMEM((SUB, D), F32),
                        pltpu.SemaphoreType.DMA,
                        pltpu.SemaphoreType.DMA((2,)),
                        pltpu.SemaphoreType.DMA((2,))],
        compiler_params=_params(("arbitrary",), 56),
        name="peer",
    )(idx_flat, gate_t, x1, g_norm, mod3, mod3, mod3, g_final, peer_u, peer_v)


def _pad_gate(w, lo):
    rank = w.shape[0]
    wh = w.reshape(rank, GLA_HEADS, GLA_DK).transpose(1, 0, 2)
    return jnp.zeros((GLA_HEADS, LANE, GLA_DK), F32).at[:, lo:lo + rank, :].set(wh)


def kernel(x, c, positions, w_ada, b_ada, g_norm_mix, w_in, w_gate_f, b_gate_f, w_gate_b, b_gate_b,
           g_gla_out, w_out, g_norm_ffn, w_peer_q, peer_sub_keys, peer_u, peer_v, g_final):
    B, S, D = x.shape
    T = B * S
    depth = w_ada.shape[0]
    assert depth == 1, "the final norm is fused into the last PEER call; one layer only"
    xt = x.reshape(T, D)
    cs, sn = _rope_tables(positions.reshape(T, 1))
    gz0 = 2 * GLA_HEADS * GLA_DK + 2 * GLA_HEADS * GLA_DV
    gz1 = gz0 + 2 * GLA_GATE_RANK
    for l in range(depth):
        mod3 = _ada(c, w_ada[l], b_ada[l]).reshape(B * 6, 1, D)
        w_main = jnp.concatenate([w_in[l][:, :gz0], w_in[l][:, gz1:]], axis=1).astype(BF16)
        w_z = jnp.pad(w_in[l][:, gz0:gz1], ((0, 0), (0, LANE - (gz1 - gz0)))).astype(BF16)
        proj, gz = _inproj(xt, g_norm_mix[l].reshape(1, D), mod3, w_main, w_z, S)
        o_gla = _gla(proj, gz,
                     _pad_gate(w_gate_f[l], 0), b_gate_f[l].reshape(GLA_HEADS, 1, GLA_DK),
                     _pad_gate(w_gate_b[l], GLA_GATE_RANK), b_gate_b[l].reshape(GLA_HEADS, 1, GLA_DK),
                     g_gla_out[l].reshape(GLA_HEADS, 1, GLA_DV), B, S)
        o_dil = _dil(proj, cs, sn, B, S)
        x1 = _outproj(o_gla, o_dil, w_out[l].astype(BF16), xt, mod3, S)
        qp = _pq(x1, g_norm_ffn[l].reshape(1, D), mod3, w_peer_q[l].astype(BF16), S)
        idx_t, gate_t = _topk(qp, peer_sub_keys[l].astype(BF16))
        idx_flat = idx_t.T.reshape(-1)
        xt = _peer(idx_flat, gate_t, x1, g_norm_ffn[l].reshape(1, D), mod3,
                   g_final.reshape(1, D), peer_u[l], peer_v[l], S)
    return xt.reshape(B, S, D)
```

```python
import functools
import math

import numpy as np
import jax
import jax.numpy as jnp
from jax import lax
from jax.experimental import pallas as pl
from jax.experimental.pallas import tpu as pltpu

F32 = jnp.float32
BF16 = jnp.bfloat16
HIGHEST = lax.Precision.HIGHEST

NORM_EPS = 1e-6
GLA_HEADS = 4
GLA_DK = 128
GLA_DV = 256
GLA_GATE_RANK = 16
GLA_TAU = 16.0
GLA_CHUNK = 64
DIL_HD = 128
DIL_HEADS = 8
DIL_PATTERNS = ((128, 1), (512, 4), (2048, 16))
ROPE_THETA = 10000.0
NEG_INF = -1e30
PEER_HEADS = 8
PEER_NKEYS = 128
PEER_TOPK = 16
PEER_HALF = 128

LANE = 128
MIB = 1024 * 1024

_COL_GQ, _COL_GK, _COL_GV, _COL_GR, _COL_DQ, _COL_DK, _COL_DV = 0, 4, 8, 16, 24, 32, 40
_PROJ_W = 48 * LANE

_NT = (((1,), (1,)), ((), ()))
_TN = (((0,), (0,)), ((), ()))


def _params(sem, vmem_mib):
    return pltpu.CompilerParams(dimension_semantics=sem, vmem_limit_bytes=vmem_mib * MIB)


def _rms(x, g):
    return x * lax.rsqrt(jnp.mean(x * x, axis=-1, keepdims=True) + NORM_EPS) * g


def _silu(x):
    return x / (1.0 + jnp.exp(-x))


def _ada_kernel(c_ref, w_ref, b_ref, o_ref):
    s = _silu(c_ref[...]).astype(BF16)
    o_ref[...] = jnp.dot(s, w_ref[...].astype(BF16), preferred_element_type=F32) + b_ref[...]


def _ada(c, w, b):
    B, D = c.shape
    N = w.shape[1]
    tn = 1024
    cp = jnp.zeros((8, D), F32).at[:B].set(c)
    out = pl.pallas_call(
        _ada_kernel,
        grid=(N // tn,),
        in_specs=[pl.BlockSpec((8, D), lambda j: (0, 0)),
                  pl.BlockSpec((D, tn), lambda j: (0, j)),
                  pl.BlockSpec((1, tn), lambda j: (0, j))],
        out_specs=pl.BlockSpec((8, tn), lambda j: (0, j)),
        out_shape=jax.ShapeDtypeStruct((8, N), F32),
        compiler_params=_params(("parallel",), 40),
        name="ada",
    )(cp, w, b.reshape(1, N))
    return out[:B]


def _inproj_kernel(x_ref, g_ref, sc_ref, sh_ref, w_ref, wz_ref, o_ref, z_ref, h_scr):
    @pl.when(pl.program_id(1) == 0)
    def _():
        h = _rms(x_ref[...], g_ref[...]) * (1.0 + sc_ref[0]) + sh_ref[0]
        hb = h.astype(BF16)
        h_scr[...] = hb
        z_ref[...] = jnp.dot(hb, wz_ref[...], preferred_element_type=F32)

    o_ref[...] = jnp.dot(h_scr[...], w_ref[...], preferred_element_type=F32)


def _inproj(x2, g, mod3, w_main, w_z, S):
    T, D = x2.shape
    tm, tn = 1024, 768
    per_b = S // tm
    return pl.pallas_call(
        _inproj_kernel,
        grid=(T // tm, _PROJ_W // tn),
        in_specs=[pl.BlockSpec((tm, D), lambda i, j: (i, 0)),
                  pl.BlockSpec((1, D), lambda i, j: (0, 0)),
                  pl.BlockSpec((1, 1, D), lambda i, j: ((i // per_b) * 6 + 1, 0, 0)),
                  pl.BlockSpec((1, 1, D), lambda i, j: ((i // per_b) * 6 + 0, 0, 0)),
                  pl.BlockSpec((D, tn), lambda i, j: (0, j)),
                  pl.BlockSpec((D, LANE), lambda i, j: (0, 0))],
        out_specs=[pl.BlockSpec((tm, tn), lambda i, j: (i, j)),
                   pl.BlockSpec((tm, LANE), lambda i, j: (i, 0))],
        out_shape=[jax.ShapeDtypeStruct((T, _PROJ_W), F32),
                   jax.ShapeDtypeStruct((T, LANE), F32)],
        scratch_shapes=[pltpu.VMEM((tm, D), BF16)],
        compiler_params=_params(("parallel", "arbitrary"), 48),
        name="inproj",
    )(x2, g, mod3, mod3, w_main, w_z)


def _rope_kernel(pos_ref, f_ref, sg_ref, cs_ref, sn_ref):
    ang = pos_ref[...].astype(F32) * f_ref[...]
    cs_ref[...] = jnp.cos(ang)
    sn_ref[...] = jnp.sin(ang) * sg_ref[...]


def _rope_tables(pos_col):
    T = pos_col.shape[0]
    half = DIL_HD // 2
    inv = jnp.power(ROPE_THETA, -jnp.arange(half, dtype=F32) * 2.0 / DIL_HD)
    freq = jnp.concatenate([inv, inv]).reshape(1, DIL_HD)
    sign = jnp.concatenate([-jnp.ones((half,), F32), jnp.ones((half,), F32)]).reshape(1, DIL_HD)
    tm = 1024
    return pl.pallas_call(
        _rope_kernel,
        grid=(T // tm,),
        in_specs=[pl.BlockSpec((tm, 1), lambda i: (i, 0)),
                  pl.BlockSpec((1, DIL_HD), lambda i: (0, 0)),
                  pl.BlockSpec((1, DIL_HD), lambda i: (0, 0))],
        out_specs=[pl.BlockSpec((tm, DIL_HD), lambda i: (i, 0)),
                   pl.BlockSpec((tm, DIL_HD), lambda i: (i, 0))],
        out_shape=[jax.ShapeDtypeStruct((T, DIL_HD), F32)] * 2,
        compiler_params=_params(("parallel",), 32),
        name="rope",
    )(pos_col, freq, sign)


def _gla_kernel(q_ref, k_ref, v_ref, r_ref, z_ref, wgf_ref, bgf_ref, wgb_ref, bgb_ref, g_ref,
                o_ref, la_scr, o_scr):
    S = q_ref.shape[0]
    C = GLA_CHUNK
    n = S // C
    scale = GLA_DK ** -0.5
    row = lax.broadcasted_iota(jnp.int32, (C, C), 0)
    col = lax.broadcasted_iota(jnp.int32, (C, C), 1)

    def log_gate(w_ref, b_ref):
        zz = jnp.dot(z_ref[...], w_ref[0], precision=HIGHEST, preferred_element_type=F32) + b_ref[0]
        return (jnp.minimum(zz, 0.0) - jnp.log(1.0 + jnp.exp(-jnp.abs(zz)))) * (1.0 / GLA_TAU)

    def run(fwd):
        keep = (col <= row) if fwd else (col >= row)
        tri = keep.astype(F32)

        def body(i, st_t):
            c = i if fwd else n - 1 - i
            sl = pl.ds(pl.multiple_of(c * C, C), C)
            cum = jnp.dot(tri, la_scr[sl, :], precision=HIGHEST, preferred_element_type=F32)
            tot = cum[C - 1:C, :] if fwd else cum[0:1, :]
            kk = k_ref[sl, :]
            qd = (q_ref[sl, :] * scale * jnp.exp(cum)).astype(BF16)
            ki = (kk * jnp.exp(-cum)).astype(BF16)
            kte = (kk * jnp.exp(tot - cum)).astype(BF16)
            vb = v_ref[sl, :].astype(BF16)
            attn = lax.dot_general(qd, ki, _NT, preferred_element_type=F32)
            attn = jnp.where(keep, attn, 0.0).astype(BF16)
            o = jnp.dot(attn, vb, preferred_element_type=F32)
            o = o + lax.dot_general(qd, st_t.astype(BF16), _NT, preferred_element_type=F32)
            if fwd:
                o_scr[sl, :] = o
            else:
                o_scr[sl, :] = o_scr[sl, :] + o
            upd = lax.dot_general(vb, kte, _TN, preferred_element_type=F32)
            return st_t * jnp.exp(tot) + upd

        lax.fori_loop(0, n, body, jnp.zeros((GLA_DV, GLA_DK), F32))

    la_scr[...] = log_gate(wgf_ref, bgf_ref)
    run(True)
    la_scr[...] = log_gate(wgb_ref, bgb_ref)
    run(False)
    y = _rms(o_scr[...], g_ref[0])
    o_ref[...] = y * _silu(r_ref[...])


def _gla(proj, gz, wgf, bgf, wgb, bgb, g_out, B, S):
    T = proj.shape[0]
    H = GLA_HEADS
    return pl.pallas_call(
        _gla_kernel,
        grid=(B, H),
        in_specs=[pl.BlockSpec((S, GLA_DK), lambda b, h: (b, _COL_GQ + h)),
                  pl.BlockSpec((S, GLA_DK), lambda b, h: (b, _COL_GK + h)),
                  pl.BlockSpec((S, GLA_DV), lambda b, h: (b, _COL_GV // 2 + h)),
                  pl.BlockSpec((S, GLA_DV), lambda b, h: (b, _COL_GR // 2 + h)),
                  pl.BlockSpec((S, LANE), lambda b, h: (b, 0)),
                  pl.BlockSpec((1, LANE, GLA_DK), lambda b, h: (h, 0, 0)),
                  pl.BlockSpec((1, 1, GLA_DK), lambda b, h: (h, 0, 0)),
                  pl.BlockSpec((1, LANE, GLA_DK), lambda b, h: (h, 0, 0)),
                  pl.BlockSpec((1, 1, GLA_DK), lambda b, h: (h, 0, 0)),
                  pl.BlockSpec((1, 1, GLA_DV), lambda b, h: (h, 0, 0))],
        out_specs=pl.BlockSpec((S, GLA_DV), lambda b, h: (b, h)),
        out_shape=jax.ShapeDtypeStruct((T, GLA_HEADS * GLA_DV), F32),
        scratch_shapes=[pltpu.VMEM((S, GLA_DK), F32), pltpu.VMEM((S, GLA_DV), F32)],
        compiler_params=_params(("parallel", "parallel"), 56),
        name="gla",
    )(proj, proj, proj, proj, gz, wgf, bgf, wgb, bgb, g_out)


_DIL_QB = 256
_DIL_REACH = max(w // 2 for w, _ in DIL_PATTERNS) // _DIL_QB


def _dil_bias():
    qb = _DIL_QB
    d = np.arange(-_DIL_REACH, _DIL_REACH + 1)[:, None, None] * qb
    delta = d + np.arange(qb)[None, None, :] - np.arange(qb)[None, :, None]
    mult = np.zeros(delta.shape, np.float64)
    for window, dilation in DIL_PATTERNS:
        half = window // (2 * dilation)
        mult += ((delta % dilation) == 0) & (np.abs(delta) <= half * dilation)
    with np.errstate(divide="ignore"):
        bias = np.where(mult > 0, np.log(np.maximum(mult, 1.0)), NEG_INF)
    return jnp.asarray(bias, F32)


def _dil_kernel(q_ref, k_ref, v_ref, cs_ref, sn_ref, bias_ref, o_ref, kr_scr, vb_scr):
    qi = pl.program_id(2)
    nq = pl.num_programs(2)
    QB = _DIL_QB
    half = DIL_HD // 2

    @pl.when(qi == 0)
    def _():
        k = k_ref[...]
        kr_scr[...] = (k * cs_ref[...] + pltpu.roll(k, half, 1) * sn_ref[...]).astype(BF16)
        vb_scr[...] = v_ref[...].astype(BF16)

    r0 = pl.multiple_of(qi * QB, QB)
    q = q_ref[...]
    qr = q * cs_ref[pl.ds(r0, QB), :] + pltpu.roll(q, half, 1) * sn_ref[pl.ds(r0, QB), :]
    qb = (qr * (DIL_HD ** -0.5)).astype(BF16)

    m = jnp.full((QB, 1), NEG_INF, F32)
    l = jnp.zeros((QB, 1), F32)
    acc = jnp.zeros((QB, DIL_HD), F32)
    order = [0] + [s * d for d in range(1, _DIL_REACH + 1) for s in (-1, 1)]
    for d in order:
        kb = qi + d
        valid = jnp.logical_and(kb >= 0, kb < nq)
        k0 = pl.multiple_of(jnp.clip(kb, 0, nq - 1) * QB, QB)
        s = lax.dot_general(qb, kr_scr[pl.ds(k0, QB), :], _NT, preferred_element_type=F32)
        s = jnp.where(valid, s + bias_ref[d + _DIL_REACH], NEG_INF)
        m_new = jnp.maximum(m, jnp.max(s, axis=-1, keepdims=True))
        alpha = jnp.exp(m - m_new)
        p = jnp.exp(s - m_new)
        l = alpha * l + jnp.sum(p, axis=-1, keepdims=True)
        acc = alpha * acc + jnp.dot(p.astype(BF16), vb_scr[pl.ds(k0, QB), :],
                                    preferred_element_type=F32)
        m = m_new
    o_ref[...] = acc / l


def _dil(proj, cs, sn, B, S):
    T = proj.shape[0]
    QB = _DIL_QB
    nq = S // QB
    nb = 2 * _DIL_REACH + 1
    return pl.pallas_call(
        _dil_kernel,
        grid=(B, DIL_HEADS, nq),
        in_specs=[pl.BlockSpec((QB, DIL_HD), lambda b, h, i: (b * nq + i, _COL_DQ + h)),
                  pl.BlockSpec((S, DIL_HD), lambda b, h, i: (b, _COL_DK + h)),
                  pl.BlockSpec((S, DIL_HD), lambda b, h, i: (b, _COL_DV + h)),
                  pl.BlockSpec((S, DIL_HD), lambda b, h, i: (b, 0)),
                  pl.BlockSpec((S, DIL_HD), lambda b, h, i: (b, 0)),
                  pl.BlockSpec((nb, QB, QB), lambda b, h, i: (0, 0, 0))],
        out_specs=pl.BlockSpec((QB, DIL_HD), lambda b, h, i: (b * nq + i, h)),
        out_shape=jax.ShapeDtypeStruct((T, DIL_HEADS * DIL_HD), F32),
        scratch_shapes=[pltpu.VMEM((S, DIL_HD), BF16), pltpu.VMEM((S, DIL_HD), BF16)],
        compiler_params=_params(("parallel", "parallel", "arbitrary"), 48),
        name="dil",
    )(proj, proj, proj, cs, sn, _dil_bias())


def _outproj_kernel(og_ref, od_ref, w_ref, x_ref, ga_ref, o_ref):
    kg = og_ref.shape[1]
    mixed = jnp.dot(og_ref[...].astype(BF16), w_ref[:kg, :], preferred_element_type=F32)
    mixed = mixed + jnp.dot(od_ref[...].astype(BF16), w_ref[kg:, :], preferred_element_type=F32)
    o_ref[...] = x_ref[...] + ga_ref[0] * mixed


def _outproj(o_gla, o_dil, w_out, x2, mod3, S):
    T, D = x2.shape
    tm = 512
    per_b = S // tm
    kg, kd = o_gla.shape[1], o_dil.shape[1]
    return pl.pallas_call(
        _outproj_kernel,
        grid=(T // tm,),
        in_specs=[pl.BlockSpec((tm, kg), lambda i: (i, 0)),
                  pl.BlockSpec((tm, kd), lambda i: (i, 0)),
                  pl.BlockSpec((kg + kd, D), lambda i: (0, 0)),
                  pl.BlockSpec((tm, D), lambda i: (i, 0)),
                  pl.BlockSpec((1, 1, D), lambda i: ((i // per_b) * 6 + 2, 0, 0))],
        out_specs=pl.BlockSpec((tm, D), lambda i: (i, 0)),
        out_shape=jax.ShapeDtypeStruct((T, D), F32),
        compiler_params=_params(("parallel",), 48),
        name="outproj",
    )(o_gla, o_dil, w_out, x2, mod3)


def _pq_kernel(x_ref, g_ref, sc_ref, sh_ref, w_ref, o_ref):
    h = _rms(x_ref[...], g_ref[...]) * (1.0 + sc_ref[0]) + sh_ref[0]
    o_ref[...] = jnp.dot(h.astype(BF16), w_ref[...], preferred_element_type=F32)


def _pq(x1, g, mod3, wq, S):
    T, D = x1.shape
    N = wq.shape[1]
    tm = 512
    per_b = S // tm
    return pl.pallas_call(
        _pq_kernel,
        grid=(T // tm,),
        in_specs=[pl.BlockSpec((tm, D), lambda i: (i, 0)),
                  pl.BlockSpec((1, D), lambda i: (0, 0)),
                  pl.BlockSpec((1, 1, D), lambda i: ((i // per_b) * 6 + 4, 0, 0)),
                  pl.BlockSpec((1, 1, D), lambda i: ((i // per_b) * 6 + 3, 0, 0)),
                  pl.BlockSpec((D, N), lambda i: (0, 0))],
        out_specs=pl.BlockSpec((tm, N), lambda i: (i, 0)),
        out_shape=jax.ShapeDtypeStruct((T, N), F32),
        compiler_params=_params(("parallel",), 48),
        name="pq",
    )(x1, g, mod3, mod3, wq)


def _top_rows(s, k, payload=None):
    n_rows = s.shape[0]
    rid = lax.broadcasted_iota(jnp.int32, s.shape, 0)
    vals, picks = [], []
    for _ in range(k):
        m = jnp.max(s, axis=0, keepdims=True)
        pos = jnp.min(jnp.where(s == m, rid, n_rows), axis=0, keepdims=True)
        hit = rid == pos
        vals.append(m)
        if payload is None:
            picks.append(pos)
        else:
            picks.append(jnp.sum(jnp.where(hit, payload, 0), axis=0, keepdims=True))
        s = jnp.where(hit, -jnp.inf, s)
    return jnp.concatenate(vals, axis=0), jnp.concatenate(picks, axis=0)


def _topk_kernel(q_ref, keys_ref, idx_ref, gate_ref):
    K = PEER_TOPK
    for h in range(PEER_HEADS):
        tops = []
        for half in range(2):
            c0 = (h * 2 + half) * PEER_HALF
            qh = q_ref[:, c0:c0 + PEER_HALF].astype(BF16)
            sc = lax.dot_general(keys_ref[h, half], qh, _NT, preferred_element_type=F32)
            tops.append(_top_rows(sc, K))
        (s0, i0), (s1, i1) = tops
        cand_s = jnp.concatenate([s0[i:i + 1] + s1 for i in range(K)], axis=0)
        cand_i = jnp.concatenate([i0[i:i + 1] * PEER_NKEYS + i1 for i in range(K)], axis=0)
        best, idx = _top_rows(cand_s, K, payload=cand_i)
        e = jnp.exp(best - best[0:1])
        gate = e / jnp.sum(e, axis=0, keepdims=True)
        idx_ref[h * K:(h + 1) * K, :] = idx
        gate_ref[h * K:(h + 1) * K, :] = gate


def _topk(qp, keys_bf):
    T, N = qp.shape
    tt = 256
    HK = PEER_HEADS * PEER_TOPK
    return pl.pallas_call(
        _topk_kernel,
        grid=(T // tt,),
        in_specs=[pl.BlockSpec((tt, N), lambda i: (i, 0)),
                  pl.BlockSpec(keys_bf.shape, lambda i: (0, 0, 0, 0))],
        out_specs=[pl.BlockSpec((HK, tt), lambda i: (0, i)),
                   pl.BlockSpec((HK, tt), lambda i: (0, i))],
        out_shape=[jax.ShapeDtypeStruct((HK, T), jnp.int32),
                   jax.ShapeDtypeStruct((HK, T), F32)],
        compiler_params=_params(("parallel",), 32),
        name="topk",
    )(qp, keys_bf)


_PEER_TB = 128
_PEER_SUB = 8


_SUBLANES = 8


def _peer_kernel(idx_hbm, gate_ref, x1_ref, gn_ref, sc_ref, sh_ref, ga_ref, gf_ref, u_hbm, v_hbm,
                 o_ref, idx_smem, ub0, ub1, vb0, vb1, h_scr, acc_scr, stage_scr, sem_i, sem_u, sem_v):
    HK = PEER_HEADS * PEER_TOPK
    TB, SUB = _PEER_TB, _PEER_SUB
    R = SUB * HK
    N = TB * HK
    nsub = TB // SUB
    D = x1_ref.shape[1]
    nchunk = D // LANE
    tiles = HK // _SUBLANES
    i = pl.program_id(0)
    n = pl.num_programs(0)
    cur = lax.rem(i, 2) * N
    nxt = N - cur
    more = i + 1 < n
    ubufs, vbufs = (ub0, ub1), (vb0, vb1)

    def idx_copy(step, base):
        return pltpu.make_async_copy(idx_hbm.at[pl.ds(step * N, N)], idx_smem.at[pl.ds(base, N)], sem_i)

    def issue_token(base, t, slot):
        for k in range(HK):
            e = idx_smem[base + t * HK + k]
            rt, s = t * tiles + k // _SUBLANES, k % _SUBLANES
            pltpu.make_async_copy(u_hbm.at[e], ubufs[slot].at[rt, :, s, :], sem_u.at[slot]).start()
            pltpu.make_async_copy(v_hbm.at[e], vbufs[slot].at[rt, :, s, :], sem_v.at[slot]).start()

    def wait(slot):
        pltpu.make_async_copy(ubufs[slot], ubufs[slot], sem_u.at[slot]).wait()
        pltpu.make_async_copy(vbufs[slot], vbufs[slot], sem_v.at[slot]).wait()

    @pl.when(i == 0)
    def _():
        first = idx_copy(0, 0)
        first.start()
        first.wait()
        for t in range(SUB):
            issue_token(0, t, 0)

    @pl.when(more)
    def _():
        idx_copy(i + 1, nxt).start()

    x1 = x1_ref[...]
    h_scr[...] = _rms(x1, gn_ref[...]) * (1.0 + sc_ref[0]) + sh_ref[0]
    lane = lax.broadcasted_iota(jnp.int32, (HK, TB), 1)

    def compute_token(j, t, slot):
        ub, vb = ubufs[slot], vbufs[slot]
        tok = j * SUB + t
        xt = h_scr[pl.ds(tok, 1), :]
        rows = slice(t * tiles, (t + 1) * tiles)
        part = ub[rows, 0].reshape(HK, LANE) * xt[:, 0:LANE]
        for c in range(1, nchunk):
            part = part + ub[rows, c].reshape(HK, LANE) * xt[:, c * LANE:(c + 1) * LANE]
        a = jnp.sum(part, axis=1, keepdims=True)
        g = jnp.sum(jnp.where(lane == tok, gate_ref[...], 0.0), axis=1, keepdims=True)
        wgt = g * (0.5 * a * (1.0 + lax.erf(a * (2.0 ** -0.5))))
        for c in range(nchunk):
            stage_scr[t:t + 1, c * LANE:(c + 1) * LANE] = jnp.sum(
                wgt * vb[rows, c].reshape(HK, LANE), axis=0, keepdims=True)

    def half(j, slot, next_base):
        wait(slot)
        for t in range(SUB):
            issue_token(next_base, t, 1 - slot)
            compute_token(j, t, slot)
        acc_scr[pl.ds(pl.multiple_of(j * SUB, SUB), SUB), :] = stage_scr[...]

    def pair(jj, carry):
        j0 = 2 * jj
        half(j0, 0, cur + (j0 + 1) * R)
        last = jj == nsub // 2 - 1

        @pl.when(jnp.logical_and(last, more))
        def _():
            idx_copy(i + 1, nxt).wait()

        after = jnp.where(more, nxt, cur)
        half(j0 + 1, 1, jnp.where(last, after, cur + (j0 + 2) * R))
        return carry

    lax.fori_loop(0, nsub // 2, pair, 0)

    @pl.when(jnp.logical_not(more))
    def _():
        wait(0)

    y = x1 + ga_ref[0] * acc_scr[...]
    o_ref[...] = _rms(y, gf_ref[...])


def _peer(idx_flat, gate_t, x1, g_norm, mod3, g_final, peer_u, peer_v, S):
    T, D = x1.shape
    E = peer_u.shape[0]
    HK = PEER_HEADS * PEER_TOPK
    TB, SUB = _PEER_TB, _PEER_SUB
    per_b = S // TB
    row = lambda i: (0, 0)
    modrow = lambda k: (lambda i: ((i // per_b) * 6 + k, 0, 0))
    gbuf = pltpu.VMEM((SUB * HK // _SUBLANES, D // LANE, _SUBLANES, LANE), F32)
    return pl.pallas_call(
        _peer_kernel,
        grid=(T // TB,),
        in_specs=[pl.BlockSpec(memory_space=pl.ANY),
                  pl.BlockSpec((HK, TB), lambda i: (0, i)),
                  pl.BlockSpec((TB, D), lambda i: (i, 0)),
                  pl.BlockSpec((1, D), row),
                  pl.BlockSpec((1, 1, D), modrow(4)),
                  pl.BlockSpec((1, 1, D), modrow(3)),
                  pl.BlockSpec((1, 1, D), modrow(5)),
                  pl.BlockSpec((1, D), row),
                  pl.BlockSpec(memory_space=pl.ANY),
                  pl.BlockSpec(memory_space=pl.ANY)],
        out_specs=pl.BlockSpec((TB, D), lambda i: (i, 0)),
        out_shape=jax.ShapeDtypeStruct((T, D), F32),
        scratch_shapes=[pltpu.SMEM((2 * TB * HK,), jnp.int32),
                        gbuf, gbuf, gbuf, gbuf,
                        pltpu.VMEM((TB, D), F32),
                        pltpu.VMEM((TB, D), F32),
                        pltpu.VMEM((SUB, D), F32),
                        pltpu.SemaphoreType.DMA,
                        pltpu.SemaphoreType.DMA((2,)),
                        pltpu.SemaphoreType.DMA((2,))],
        compiler_params=_params(("arbitrary",), 56),
        name="peer",
    )(idx_flat, gate_t, x1, g_norm, mod3, mod3, mod3, g_final,
      peer_u.reshape(E, D // LANE, LANE), peer_v.reshape(E, D // LANE, LANE))


def _pad_gate(w, lo):
    rank = w.shape[0]
    wh = w.reshape(rank, GLA_HEADS, GLA_DK).transpose(1, 0, 2)
    return jnp.zeros((GLA_HEADS, LANE, GLA_DK), F32).at[:, lo:lo + rank, :].set(wh)


def kernel(x, c, positions, w_ada, b_ada, g_norm_mix, w_in, w_gate_f, b_gate_f, w_gate_b, b_gate_b,
           g_gla_out, w_out, g_norm_ffn, w_peer_q, peer_sub_keys, peer_u, peer_v, g_final):
    B, S, D = x.shape
    T = B * S
    depth = w_ada.shape[0]
    assert depth == 1, "the final norm is fused into the last PEER call; one layer only"
    xt = x.reshape(T, D)
    cs, sn = _rope_tables(positions.reshape(T, 1))
    gz0 = 2 * GLA_HEADS * GLA_DK + 2 * GLA_HEADS * GLA_DV
    gz1 = gz0 + 2 * GLA_GATE_RANK
    for l in range(depth):
        mod3 = _ada(c, w_ada[l], b_ada[l]).reshape(B * 6, 1, D)
        w_main = jnp.concatenate([w_in[l][:, :gz0], w_in[l][:, gz1:]], axis=1).astype(BF16)
        w_z = jnp.pad(w_in[l][:, gz0:gz1], ((0, 0), (0, LANE - (gz1 - gz0)))).astype(BF16)
        proj, gz = _inproj(xt, g_norm_mix[l].reshape(1, D), mod3, w_main, w_z, S)
        o_gla = _gla(proj, gz,
                     _pad_gate(w_gate_f[l], 0), b_gate_f[l].reshape(GLA_HEADS, 1, GLA_DK),
                     _pad_gate(w_gate_b[l], GLA_GATE_RANK), b_gate_b[l].reshape(GLA_HEADS, 1, GLA_DK),
                     g_gla_out[l].reshape(GLA_HEADS, 1, GLA_DV), B, S)
        o_dil = _dil(proj, cs, sn, B, S)
        x1 = _outproj(o_gla, o_dil, w_out[l].astype(BF16), xt, mod3, S)
        qp = _pq(x1, g_norm_ffn[l].reshape(1, D), mod3, w_peer_q[l].astype(BF16), S)
        idx_t, gate_t = _topk(qp, peer_sub_keys[l].astype(BF16))
        idx_flat = idx_t.T.reshape(-1)
        xt = _peer(idx_flat, gate_t, x1, g_norm_ffn[l].reshape(1, D), mod3,
                   g_final.reshape(1, D), peer_u[l], peer_v[l], S)
    return xt.reshape(B, S, D)
```

```python
import functools
import math

import numpy as np
import jax
import jax.numpy as jnp
from jax import lax
from jax.experimental import pallas as pl
from jax.experimental.pallas import tpu as pltpu
from jax.experimental.pallas import tpu_sc as plsc

F32 = jnp.float32
BF16 = jnp.bfloat16
HIGHEST = lax.Precision.HIGHEST

NORM_EPS = 1e-6
GLA_HEADS = 4
GLA_DK = 128
GLA_DV = 256
GLA_GATE_RANK = 16
GLA_TAU = 16.0
GLA_CHUNK = 64
DIL_HD = 128
DIL_HEADS = 8
DIL_PATTERNS = ((128, 1), (512, 4), (2048, 16))
ROPE_THETA = 10000.0
NEG_INF = -1e30
PEER_HEADS = 8
PEER_NKEYS = 128
PEER_TOPK = 16
PEER_HALF = 128

LANE = 128
MIB = 1024 * 1024

_COL_GQ, _COL_GK, _COL_GV, _COL_GR, _COL_DQ, _COL_DK, _COL_DV = 0, 4, 8, 16, 24, 32, 40
_PROJ_W = 48 * LANE

_NT = (((1,), (1,)), ((), ()))
_TN = (((0,), (0,)), ((), ()))


def _params(sem, vmem_mib):
    return pltpu.CompilerParams(dimension_semantics=sem, vmem_limit_bytes=vmem_mib * MIB)


def _rms(x, g):
    return x * lax.rsqrt(jnp.mean(x * x, axis=-1, keepdims=True) + NORM_EPS) * g


def _silu(x):
    return x / (1.0 + jnp.exp(-x))


def _ada_kernel(c_ref, w_ref, b_ref, o_ref):
    s = _silu(c_ref[...]).astype(BF16)
    o_ref[...] = jnp.dot(s, w_ref[...].astype(BF16), preferred_element_type=F32) + b_ref[...]


def _ada(c, w, b):
    B, D = c.shape
    N = w.shape[1]
    tn = 1024
    cp = jnp.zeros((8, D), F32).at[:B].set(c)
    out = pl.pallas_call(
        _ada_kernel,
        grid=(N // tn,),
        in_specs=[pl.BlockSpec((8, D), lambda j: (0, 0)),
                  pl.BlockSpec((D, tn), lambda j: (0, j)),
                  pl.BlockSpec((1, tn), lambda j: (0, j))],
        out_specs=pl.BlockSpec((8, tn), lambda j: (0, j)),
        out_shape=jax.ShapeDtypeStruct((8, N), F32),
        compiler_params=_params(("parallel",), 40),
        name="ada",
    )(cp, w, b.reshape(1, N))
    return out[:B]


def _inproj_kernel(x_ref, g_ref, sc_ref, sh_ref, w_ref, wz_ref, o_ref, z_ref, h_scr):
    @pl.when(pl.program_id(1) == 0)
    def _():
        h = _rms(x_ref[...], g_ref[...]) * (1.0 + sc_ref[0]) + sh_ref[0]
        hb = h.astype(BF16)
        h_scr[...] = hb
        z_ref[...] = jnp.dot(hb, wz_ref[...], preferred_element_type=F32)

    o_ref[...] = jnp.dot(h_scr[...], w_ref[...], preferred_element_type=F32)


def _inproj(x2, g, mod3, w_main, w_z, S):
    T, D = x2.shape
    tm, tn = 1024, 768
    per_b = S // tm
    return pl.pallas_call(
        _inproj_kernel,
        grid=(T // tm, _PROJ_W // tn),
        in_specs=[pl.BlockSpec((tm, D), lambda i, j: (i, 0)),
                  pl.BlockSpec((1, D), lambda i, j: (0, 0)),
                  pl.BlockSpec((1, 1, D), lambda i, j: ((i // per_b) * 6 + 1, 0, 0)),
                  pl.BlockSpec((1, 1, D), lambda i, j: ((i // per_b) * 6 + 0, 0, 0)),
                  pl.BlockSpec((D, tn), lambda i, j: (0, j)),
                  pl.BlockSpec((D, LANE), lambda i, j: (0, 0))],
        out_specs=[pl.BlockSpec((tm, tn), lambda i, j: (i, j)),
                   pl.BlockSpec((tm, LANE), lambda i, j: (i, 0))],
        out_shape=[jax.ShapeDtypeStruct((T, _PROJ_W), F32),
                   jax.ShapeDtypeStruct((T, LANE), F32)],
        scratch_shapes=[pltpu.VMEM((tm, D), BF16)],
        compiler_params=_params(("parallel", "arbitrary"), 48),
        name="inproj",
    )(x2, g, mod3, mod3, w_main, w_z)


def _rope_kernel(pos_ref, f_ref, sg_ref, cs_ref, sn_ref):
    ang = pos_ref[...].astype(F32) * f_ref[...]
    cs_ref[...] = jnp.cos(ang)
    sn_ref[...] = jnp.sin(ang) * sg_ref[...]


def _rope_tables(pos_col):
    T = pos_col.shape[0]
    half = DIL_HD // 2
    inv = jnp.power(ROPE_THETA, -jnp.arange(half, dtype=F32) * 2.0 / DIL_HD)
    freq = jnp.concatenate([inv, inv]).reshape(1, DIL_HD)
    sign = jnp.concatenate([-jnp.ones((half,), F32), jnp.ones((half,), F32)]).reshape(1, DIL_HD)
    tm = 1024
    return pl.pallas_call(
        _rope_kernel,
        grid=(T // tm,),
        in_specs=[pl.BlockSpec((tm, 1), lambda i: (i, 0)),
                  pl.BlockSpec((1, DIL_HD), lambda i: (0, 0)),
                  pl.BlockSpec((1, DIL_HD), lambda i: (0, 0))],
        out_specs=[pl.BlockSpec((tm, DIL_HD), lambda i: (i, 0)),
                   pl.BlockSpec((tm, DIL_HD), lambda i: (i, 0))],
        out_shape=[jax.ShapeDtypeStruct((T, DIL_HD), F32)] * 2,
        compiler_params=_params(("parallel",), 32),
        name="rope",
    )(pos_col, freq, sign)


def _gla_kernel(q_ref, k_ref, v_ref, r_ref, z_ref, wgf_ref, bgf_ref, wgb_ref, bgb_ref, g_ref,
                o_ref, la_scr, o_scr):
    S = q_ref.shape[0]
    C = GLA_CHUNK
    n = S // C
    scale = GLA_DK ** -0.5
    row = lax.broadcasted_iota(jnp.int32, (C, C), 0)
    col = lax.broadcasted_iota(jnp.int32, (C, C), 1)

    def log_gate(w_ref, b_ref):
        zz = jnp.dot(z_ref[...], w_ref[0], precision=HIGHEST, preferred_element_type=F32) + b_ref[0]
        return (jnp.minimum(zz, 0.0) - jnp.log(1.0 + jnp.exp(-jnp.abs(zz)))) * (1.0 / GLA_TAU)

    def run(fwd):
        keep = (col <= row) if fwd else (col >= row)
        tri = keep.astype(F32)

        def body(i, st_t):
            c = i if fwd else n - 1 - i
            sl = pl.ds(pl.multiple_of(c * C, C), C)
            cum = jnp.dot(tri, la_scr[sl, :], precision=HIGHEST, preferred_element_type=F32)
            tot = cum[C - 1:C, :] if fwd else cum[0:1, :]
            kk = k_ref[sl, :]
            qd = (q_ref[sl, :] * scale * jnp.exp(cum)).astype(BF16)
            ki = (kk * jnp.exp(-cum)).astype(BF16)
            kte = (kk * jnp.exp(tot - cum)).astype(BF16)
            vb = v_ref[sl, :].astype(BF16)
            attn = lax.dot_general(qd, ki, _NT, preferred_element_type=F32)
            attn = jnp.where(keep, attn, 0.0).astype(BF16)
            o = jnp.dot(attn, vb, preferred_element_type=F32)
            o = o + lax.dot_general(qd, st_t.astype(BF16), _NT, preferred_element_type=F32)
            if fwd:
                o_scr[sl, :] = o
            else:
                o_scr[sl, :] = o_scr[sl, :] + o
            upd = lax.dot_general(vb, kte, _TN, preferred_element_type=F32)
            return st_t * jnp.exp(tot) + upd

        lax.fori_loop(0, n, body, jnp.zeros((GLA_DV, GLA_DK), F32))

    la_scr[...] = log_gate(wgf_ref, bgf_ref)
    run(True)
    la_scr[...] = log_gate(wgb_ref, bgb_ref)
    run(False)
    y = _rms(o_scr[...], g_ref[0])
    o_ref[...] = y * _silu(r_ref[...])


def _gla(proj, gz, wgf, bgf, wgb, bgb, g_out, B, S):
    T = proj.shape[0]
    H = GLA_HEADS
    return pl.pallas_call(
        _gla_kernel,
        grid=(B, H),
        in_specs=[pl.BlockSpec((S, GLA_DK), lambda b, h: (b, _COL_GQ + h)),
                  pl.BlockSpec((S, GLA_DK), lambda b, h: (b, _COL_GK + h)),
                  pl.BlockSpec((S, GLA_DV), lambda b, h: (b, _COL_GV // 2 + h)),
                  pl.BlockSpec((S, GLA_DV), lambda b, h: (b, _COL_GR // 2 + h)),
                  pl.BlockSpec((S, LANE), lambda b, h: (b, 0)),
                  pl.BlockSpec((1, LANE, GLA_DK), lambda b, h: (h, 0, 0)),
                  pl.BlockSpec((1, 1, GLA_DK), lambda b, h: (h, 0, 0)),
                  pl.BlockSpec((1, LANE, GLA_DK), lambda b, h: (h, 0, 0)),
                  pl.BlockSpec((1, 1, GLA_DK), lambda b, h: (h, 0, 0)),
                  pl.BlockSpec((1, 1, GLA_DV), lambda b, h: (h, 0, 0))],
        out_specs=pl.BlockSpec((S, GLA_DV), lambda b, h: (b, h)),
        out_shape=jax.ShapeDtypeStruct((T, GLA_HEADS * GLA_DV), F32),
        scratch_shapes=[pltpu.VMEM((S, GLA_DK), F32), pltpu.VMEM((S, GLA_DV), F32)],
        compiler_params=_params(("parallel", "parallel"), 56),
        name="gla",
    )(proj, proj, proj, proj, gz, wgf, bgf, wgb, bgb, g_out)


_DIL_QB = 256
_DIL_REACH = max(w // 2 for w, _ in DIL_PATTERNS) // _DIL_QB


def _dil_bias():
    qb = _DIL_QB
    d = np.arange(-_DIL_REACH, _DIL_REACH + 1)[:, None, None] * qb
    delta = d + np.arange(qb)[None, None, :] - np.arange(qb)[None, :, None]
    mult = np.zeros(delta.shape, np.float64)
    for window, dilation in DIL_PATTERNS:
        half = window // (2 * dilation)
        mult += ((delta % dilation) == 0) & (np.abs(delta) <= half * dilation)
    with np.errstate(divide="ignore"):
        bias = np.where(mult > 0, np.log(np.maximum(mult, 1.0)), NEG_INF)
    return jnp.asarray(bias, F32)


def _dil_kernel(q_ref, k_ref, v_ref, cs_ref, sn_ref, bias_ref, o_ref, kr_scr, vb_scr):
    qi = pl.program_id(2)
    nq = pl.num_programs(2)
    QB = _DIL_QB
    half = DIL_HD // 2

    @pl.when(qi == 0)
    def _():
        k = k_ref[...]
        kr_scr[...] = (k * cs_ref[...] + pltpu.roll(k, half, 1) * sn_ref[...]).astype(BF16)
        vb_scr[...] = v_ref[...].astype(BF16)

    r0 = pl.multiple_of(qi * QB, QB)
    q = q_ref[...]
    qr = q * cs_ref[pl.ds(r0, QB), :] + pltpu.roll(q, half, 1) * sn_ref[pl.ds(r0, QB), :]
    qb = (qr * (DIL_HD ** -0.5)).astype(BF16)

    m = jnp.full((QB, 1), NEG_INF, F32)
    l = jnp.zeros((QB, 1), F32)
    acc = jnp.zeros((QB, DIL_HD), F32)
    order = [0] + [s * d for d in range(1, _DIL_REACH + 1) for s in (-1, 1)]
    for d in order:
        kb = qi + d
        valid = jnp.logical_and(kb >= 0, kb < nq)
        k0 = pl.multiple_of(jnp.clip(kb, 0, nq - 1) * QB, QB)
        s = lax.dot_general(qb, kr_scr[pl.ds(k0, QB), :], _NT, preferred_element_type=F32)
        s = jnp.where(valid, s + bias_ref[d + _DIL_REACH], NEG_INF)
        m_new = jnp.maximum(m, jnp.max(s, axis=-1, keepdims=True))
        alpha = jnp.exp(m - m_new)
        p = jnp.exp(s - m_new)
        l = alpha * l + jnp.sum(p, axis=-1, keepdims=True)
        acc = alpha * acc + jnp.dot(p.astype(BF16), vb_scr[pl.ds(k0, QB), :],
                                    preferred_element_type=F32)
        m = m_new
    o_ref[...] = acc / l


def _dil(proj, cs, sn, B, S):
    T = proj.shape[0]
    QB = _DIL_QB
    nq = S // QB
    nb = 2 * _DIL_REACH + 1
    return pl.pallas_call(
        _dil_kernel,
        grid=(B, DIL_HEADS, nq),
        in_specs=[pl.BlockSpec((QB, DIL_HD), lambda b, h, i: (b * nq + i, _COL_DQ + h)),
                  pl.BlockSpec((S, DIL_HD), lambda b, h, i: (b, _COL_DK + h)),
                  pl.BlockSpec((S, DIL_HD), lambda b, h, i: (b, _COL_DV + h)),
                  pl.BlockSpec((S, DIL_HD), lambda b, h, i: (b, 0)),
                  pl.BlockSpec((S, DIL_HD), lambda b, h, i: (b, 0)),
                  pl.BlockSpec((nb, QB, QB), lambda b, h, i: (0, 0, 0))],
        out_specs=pl.BlockSpec((QB, DIL_HD), lambda b, h, i: (b * nq + i, h)),
        out_shape=jax.ShapeDtypeStruct((T, DIL_HEADS * DIL_HD), F32),
        scratch_shapes=[pltpu.VMEM((S, DIL_HD), BF16), pltpu.VMEM((S, DIL_HD), BF16)],
        compiler_params=_params(("parallel", "parallel", "arbitrary"), 48),
        name="dil",
    )(proj, proj, proj, cs, sn, _dil_bias())


def _outproj_kernel(og_ref, od_ref, w_ref, x_ref, ga_ref, o_ref):
    kg = og_ref.shape[1]
    mixed = jnp.dot(og_ref[...].astype(BF16), w_ref[:kg, :], preferred_element_type=F32)
    mixed = mixed + jnp.dot(od_ref[...].astype(BF16), w_ref[kg:, :], preferred_element_type=F32)
    o_ref[...] = x_ref[...] + ga_ref[0] * mixed


def _outproj(o_gla, o_dil, w_out, x2, mod3, S):
    T, D = x2.shape
    tm = 512
    per_b = S // tm
    kg, kd = o_gla.shape[1], o_dil.shape[1]
    return pl.pallas_call(
        _outproj_kernel,
        grid=(T // tm,),
        in_specs=[pl.BlockSpec((tm, kg), lambda i: (i, 0)),
                  pl.BlockSpec((tm, kd), lambda i: (i, 0)),
                  pl.BlockSpec((kg + kd, D), lambda i: (0, 0)),
                  pl.BlockSpec((tm, D), lambda i: (i, 0)),
                  pl.BlockSpec((1, 1, D), lambda i: ((i // per_b) * 6 + 2, 0, 0))],
        out_specs=pl.BlockSpec((tm, D), lambda i: (i, 0)),
        out_shape=jax.ShapeDtypeStruct((T, D), F32),
        compiler_params=_params(("parallel",), 48),
        name="outproj",
    )(o_gla, o_dil, w_out, x2, mod3)


def _pq_kernel(x_ref, g_ref, sc_ref, sh_ref, w_ref, o_ref):
    h = _rms(x_ref[...], g_ref[...]) * (1.0 + sc_ref[0]) + sh_ref[0]
    o_ref[...] = jnp.dot(h.astype(BF16), w_ref[...], preferred_element_type=F32)


def _pq(x1, g, mod3, wq, S):
    T, D = x1.shape
    N = wq.shape[1]
    tm = 512
    per_b = S // tm
    return pl.pallas_call(
        _pq_kernel,
        grid=(T // tm,),
        in_specs=[pl.BlockSpec((tm, D), lambda i: (i, 0)),
                  pl.BlockSpec((1, D), lambda i: (0, 0)),
                  pl.BlockSpec((1, 1, D), lambda i: ((i // per_b) * 6 + 4, 0, 0)),
                  pl.BlockSpec((1, 1, D), lambda i: ((i // per_b) * 6 + 3, 0, 0)),
                  pl.BlockSpec((D, N), lambda i: (0, 0))],
        out_specs=pl.BlockSpec((tm, N), lambda i: (i, 0)),
        out_shape=jax.ShapeDtypeStruct((T, N), F32),
        compiler_params=_params(("parallel",), 48),
        name="pq",
    )(x1, g, mod3, mod3, wq)


def _top_rows(s, k, payload=None):
    n_rows = s.shape[0]
    rid = lax.broadcasted_iota(jnp.int32, s.shape, 0)
    vals, picks = [], []
    for _ in range(k):
        m = jnp.max(s, axis=0, keepdims=True)
        pos = jnp.min(jnp.where(s == m, rid, n_rows), axis=0, keepdims=True)
        hit = rid == pos
        vals.append(m)
        if payload is None:
            picks.append(pos)
        else:
            picks.append(jnp.sum(jnp.where(hit, payload, 0), axis=0, keepdims=True))
        s = jnp.where(hit, -jnp.inf, s)
    return jnp.concatenate(vals, axis=0), jnp.concatenate(picks, axis=0)


def _topk_kernel(q_ref, keys_ref, idx_ref, gate_ref):
    K = PEER_TOPK
    for h in range(PEER_HEADS):
        tops = []
        for half in range(2):
            c0 = (h * 2 + half) * PEER_HALF
            qh = q_ref[:, c0:c0 + PEER_HALF].astype(BF16)
            sc = lax.dot_general(keys_ref[h, half], qh, _NT, preferred_element_type=F32)
            tops.append(_top_rows(sc, K))
        (s0, i0), (s1, i1) = tops
        cand_s = jnp.concatenate([s0[i:i + 1] + s1 for i in range(K)], axis=0)
        cand_i = jnp.concatenate([i0[i:i + 1] * PEER_NKEYS + i1 for i in range(K)], axis=0)
        best, idx = _top_rows(cand_s, K, payload=cand_i)
        e = jnp.exp(best - best[0:1])
        gate = e / jnp.sum(e, axis=0, keepdims=True)
        idx_ref[h * K:(h + 1) * K, :] = idx
        gate_ref[h * K:(h + 1) * K, :] = gate


def _topk(qp, keys_bf):
    T, N = qp.shape
    tt = 256
    HK = PEER_HEADS * PEER_TOPK
    return pl.pallas_call(
        _topk_kernel,
        grid=(T // tt,),
        in_specs=[pl.BlockSpec((tt, N), lambda i: (i, 0)),
                  pl.BlockSpec(keys_bf.shape, lambda i: (0, 0, 0, 0))],
        out_specs=[pl.BlockSpec((HK, tt), lambda i: (0, i)),
                   pl.BlockSpec((HK, tt), lambda i: (0, i))],
        out_shape=[jax.ShapeDtypeStruct((HK, T), jnp.int32),
                   jax.ShapeDtypeStruct((HK, T), F32)],
        compiler_params=_params(("parallel",), 32),
        name="topk",
    )(qp, keys_bf)


_PEER_TB = 128
_PEER_SUB = 8
_SUBLANES = 8
_PEER_CHUNKS = 4
_SC_LANES = 16
_SC_ROWS = 16


def _peer_u_kernel(idx_hbm, gate_ref, x1_ref, gn_ref, sc_ref, sh_ref, u_hbm, w_ref,
                   idx_smem, ub0, ub1, h_scr, sem_i, sem_u, *, step0):
    HK = PEER_HEADS * PEER_TOPK
    TB, SUB = _PEER_TB, _PEER_SUB
    R = SUB * HK
    N = TB * HK
    nsub = TB // SUB
    D = x1_ref.shape[1]
    nchunk = D // LANE
    tiles = HK // _SUBLANES
    i = pl.program_id(0)
    n = pl.num_programs(0)
    cur = lax.rem(i, 2) * N
    nxt = N - cur
    more = i + 1 < n
    ubufs = (ub0, ub1)

    def idx_copy(step, base):
        return pltpu.make_async_copy(idx_hbm.at[pl.ds((step0 + step) * N, N)],
                                     idx_smem.at[pl.ds(base, N)], sem_i)

    def issue_token(base, t, slot):
        for k in range(HK):
            e = idx_smem[base + t * HK + k]
            rt, s = t * tiles + k // _SUBLANES, k % _SUBLANES
            pltpu.make_async_copy(u_hbm.at[e], ubufs[slot].at[rt, :, s, :], sem_u.at[slot]).start()

    def wait(slot):
        pltpu.make_async_copy(ubufs[slot], ubufs[slot], sem_u.at[slot]).wait()

    @pl.when(i == 0)
    def _():
        first = idx_copy(0, 0)
        first.start()
        first.wait()
        for t in range(SUB):
            issue_token(0, t, 0)

    @pl.when(more)
    def _():
        idx_copy(i + 1, nxt).start()

    h_scr[...] = _rms(x1_ref[...], gn_ref[...]) * (1.0 + sc_ref[0]) + sh_ref[0]
    lane = lax.broadcasted_iota(jnp.int32, (HK, TB), 1)

    def compute_token(j, t, slot, wacc):
        ub = ubufs[slot]
        tok = j * SUB + t
        xt = h_scr[pl.ds(tok, 1), :]
        rows = slice(t * tiles, (t + 1) * tiles)
        part = ub[rows, 0].reshape(HK, LANE) * xt[:, 0:LANE]
        for c in range(1, nchunk):
            part = part + ub[rows, c].reshape(HK, LANE) * xt[:, c * LANE:(c + 1) * LANE]
        a = jnp.sum(part, axis=1, keepdims=True)
        hit = lane == tok
        g = jnp.sum(jnp.where(hit, gate_ref[...], 0.0), axis=1, keepdims=True)
        wgt = g * (0.5 * a * (1.0 + lax.erf(a * (2.0 ** -0.5))))
        return jnp.where(hit, wgt, wacc)

    def half(j, slot, next_base, wacc):
        wait(slot)
        for t in range(SUB):
            issue_token(next_base, t, 1 - slot)
            wacc = compute_token(j, t, slot, wacc)
        return wacc

    def pair(jj, wacc):
        j0 = 2 * jj
        wacc = half(j0, 0, cur + (j0 + 1) * R, wacc)
        last = jj == nsub // 2 - 1

        @pl.when(jnp.logical_and(last, more))
        def _():
            idx_copy(i + 1, nxt).wait()

        after = jnp.where(more, nxt, cur)
        return half(j0 + 1, 1, jnp.where(last, after, cur + (j0 + 2) * R), wacc)

    wacc = lax.fori_loop(0, nsub // 2, pair, jnp.zeros((HK, TB), F32))

    @pl.when(jnp.logical_not(more))
    def _():
        wait(0)

    w_ref[:, :HK] = jnp.zeros((TB, HK), F32)
    w_ref[:, HK:] = wacc.T


def _peer_u(idx_flat, gate_t, x1, g_norm, mod3, u3, S, step0, nsteps):
    T, D = x1.shape
    HK = PEER_HEADS * PEER_TOPK
    TB, SUB = _PEER_TB, _PEER_SUB
    per_b = S // TB
    modrow = lambda k: (lambda i: (((step0 + i) // per_b) * 6 + k, 0, 0))
    gbuf = pltpu.VMEM((SUB * HK // _SUBLANES, D // LANE, _SUBLANES, LANE), F32)
    return pl.pallas_call(
        functools.partial(_peer_u_kernel, step0=step0),
        grid=(nsteps,),
        in_specs=[pl.BlockSpec(memory_space=pl.ANY),
                  pl.BlockSpec((HK, TB), lambda i: (0, step0 + i)),
                  pl.BlockSpec((TB, D), lambda i: (step0 + i, 0)),
                  pl.BlockSpec((1, D), lambda i: (0, 0)),
                  pl.BlockSpec((1, 1, D), modrow(4)),
                  pl.BlockSpec((1, 1, D), modrow(3)),
                  pl.BlockSpec(memory_space=pl.ANY)],
        out_specs=pl.BlockSpec((TB, 2 * HK), lambda i: (i, 0)),
        out_shape=jax.ShapeDtypeStruct((nsteps * TB, 2 * HK), F32),
        scratch_shapes=[pltpu.SMEM((2 * TB * HK,), jnp.int32),
                        gbuf, gbuf,
                        pltpu.VMEM((TB, D), F32),
                        pltpu.SemaphoreType.DMA,
                        pltpu.SemaphoreType.DMA((2,))],
        compiler_params=_params(("arbitrary",), 40),
        name="peer_u",
    )(idx_flat, gate_t, x1, g_norm, mod3, mod3, u3)


def _sc_peer_v(v_tab, idx_flat, wgt, tok_base):
    E, D = v_tab.shape
    Tc, HK = wgt.shape[0], wgt.shape[1] // 2
    info = plsc.get_sparse_core_info()
    nw = info.num_cores * info.num_subcores
    tpw = Tc // nw
    CH = _SC_ROWS
    nch = HK // CH
    nsl = D // _SC_LANES
    mesh = plsc.VectorSubcoreMesh(core_axis_name="c", subcore_axis_name="s")

    @functools.partial(
        pl.kernel, mesh=mesh, out_type=jax.ShapeDtypeStruct((Tc, D), F32),
        scratch_types=[pltpu.VMEM((tpw * HK,), jnp.int32), pltpu.VMEM((2 * HK,), F32),
                       pltpu.VMEM((D,), F32),
                       pltpu.VMEM((CH, D), F32), pltpu.VMEM((CH, D), F32),
                       pltpu.SemaphoreType.DMA, pltpu.SemaphoreType.DMA],
        compiler_params=pltpu.CompilerParams(needs_layout_passes=False),
        name="sc_peer_v",
    )
    def k(tab_hbm, idx_hbm, w_hbm, o_hbm, idx_v, w_v, o_v, buf0, buf1, g0, g1):
        wid = lax.axis_index("s") * info.num_cores + lax.axis_index("c")
        tok0 = wid * tpw
        pltpu.sync_copy(idx_hbm.at[pl.ds((tok_base + tok0) * HK, tpw * HK)], idx_v)
        bufs, gs = (buf0, buf1), (g0, g1)

        def gather(g, b):
            return pltpu.make_async_copy(tab_hbm.at[idx_v.at[pl.ds(g * CH, CH)]], bufs[b], gs[b])

        gather(0, 0).start()

        @pl.loop(0, tpw)
        def _(t):
            pltpu.sync_copy(w_hbm.at[tok0 + t], w_v)

            @pl.loop(0, nsl)
            def _(c):
                o_v[pl.ds(pl.multiple_of(c * _SC_LANES, _SC_LANES), _SC_LANES)] = jnp.zeros((_SC_LANES,), F32)

            for ch in range(nch):
                b = ch % 2
                g = t * nch + ch
                gather(g, b).wait()

                @pl.when(g + 1 < tpw * nch)
                def _():
                    gather(g + 1, 1 - b).start()

                ws = [plsc.load_gather(w_v, [jnp.full((_SC_LANES,), HK + ch * CH + r, jnp.int32)])
                      for r in range(CH)]

                @pl.loop(0, nsl)
                def _(c):
                    off = pl.multiple_of(c * _SC_LANES, _SC_LANES)
                    o = o_v[pl.ds(off, _SC_LANES)]
                    for r in range(CH):
                        o = o + ws[r] * bufs[b][r, pl.ds(off, _SC_LANES)]
                    o_v[pl.ds(off, _SC_LANES)] = o

            pltpu.sync_copy(o_v, o_hbm.at[tok0 + t])

    return k(v_tab, idx_flat, wgt)


def _peer_fin_kernel(x_ref, p_ref, ga_ref, gf_ref, o_ref):
    o_ref[...] = _rms(x_ref[...] + ga_ref[0] * p_ref[...], gf_ref[...])


def _peer_fin(x1, po, mod3, g_final, S):
    T, D = x1.shape
    tm = 512
    per_b = S // tm
    return pl.pallas_call(
        _peer_fin_kernel,
        grid=(T // tm,),
        in_specs=[pl.BlockSpec((tm, D), lambda i: (i, 0)),
                  pl.BlockSpec((tm, D), lambda i: (i, 0)),
                  pl.BlockSpec((1, 1, D), lambda i: ((i // per_b) * 6 + 5, 0, 0)),
                  pl.BlockSpec((1, D), lambda i: (0, 0))],
        out_specs=pl.BlockSpec((tm, D), lambda i: (i, 0)),
        out_shape=jax.ShapeDtypeStruct((T, D), F32),
        compiler_params=_params(("parallel",), 40),
        name="peer_fin",
    )(x1, po, mod3, g_final)


def _pad_gate(w, lo):
    rank = w.shape[0]
    wh = w.reshape(rank, GLA_HEADS, GLA_DK).transpose(1, 0, 2)
    return jnp.zeros((GLA_HEADS, LANE, GLA_DK), F32).at[:, lo:lo + rank, :].set(wh)


def kernel(x, c, positions, w_ada, b_ada, g_norm_mix, w_in, w_gate_f, b_gate_f, w_gate_b, b_gate_b,
           g_gla_out, w_out, g_norm_ffn, w_peer_q, peer_sub_keys, peer_u, peer_v, g_final):
    B, S, D = x.shape
    T = B * S
    depth = w_ada.shape[0]
    assert depth == 1, "the final norm is fused into the last PEER call; one layer only"
    xt = x.reshape(T, D)
    cs, sn = _rope_tables(positions.reshape(T, 1))
    gz0 = 2 * GLA_HEADS * GLA_DK + 2 * GLA_HEADS * GLA_DV
    gz1 = gz0 + 2 * GLA_GATE_RANK
    for l in range(depth):
        mod3 = _ada(c, w_ada[l], b_ada[l]).reshape(B * 6, 1, D)
        w_main = jnp.concatenate([w_in[l][:, :gz0], w_in[l][:, gz1:]], axis=1).astype(BF16)
        w_z = jnp.pad(w_in[l][:, gz0:gz1], ((0, 0), (0, LANE - (gz1 - gz0)))).astype(BF16)
        proj, gz = _inproj(xt, g_norm_mix[l].reshape(1, D), mod3, w_main, w_z, S)
        o_gla = _gla(proj, gz,
                     _pad_gate(w_gate_f[l], 0), b_gate_f[l].reshape(GLA_HEADS, 1, GLA_DK),
                     _pad_gate(w_gate_b[l], GLA_GATE_RANK), b_gate_b[l].reshape(GLA_HEADS, 1, GLA_DK),
                     g_gla_out[l].reshape(GLA_HEADS, 1, GLA_DV), B, S)
        o_dil = _dil(proj, cs, sn, B, S)
        x1 = _outproj(o_gla, o_dil, w_out[l].astype(BF16), xt, mod3, S)
        qp = _pq(x1, g_norm_ffn[l].reshape(1, D), mod3, w_peer_q[l].astype(BF16), S)
        idx_t, gate_t = _topk(qp, peer_sub_keys[l].astype(BF16))
        idx_flat = idx_t.T.reshape(-1)
        E = peer_u.shape[1]
        u3 = peer_u[l].reshape(E, D // LANE, LANE)
        steps = T // _PEER_TB // _PEER_CHUNKS
        outs = []
        for ck in range(_PEER_CHUNKS):
            wgt = _peer_u(idx_flat, gate_t, x1, g_norm_ffn[l].reshape(1, D), mod3, u3, S, ck * steps, steps)
            outs.append(_sc_peer_v(peer_v[l], idx_flat, wgt, ck * steps * _PEER_TB))
        xt = _peer_fin(x1, jnp.concatenate(outs, axis=0), mod3, g_final.reshape(1, D), S)
    return xt.reshape(B, S, D)
```

```python
import functools
import math

import numpy as np
import jax
import jax.numpy as jnp
from jax import lax
from jax.experimental import pallas as pl
from jax.experimental.pallas import tpu as pltpu
from jax.experimental.pallas import tpu_sc as plsc

F32 = jnp.float32
BF16 = jnp.bfloat16
HIGHEST = lax.Precision.HIGHEST

NORM_EPS = 1e-6
GLA_HEADS = 4
GLA_DK = 128
GLA_DV = 256
GLA_GATE_RANK = 16
GLA_TAU = 16.0
GLA_CHUNK = 64
DIL_HD = 128
DIL_HEADS = 8
DIL_PATTERNS = ((128, 1), (512, 4), (2048, 16))
ROPE_THETA = 10000.0
NEG_INF = -1e30
PEER_HEADS = 8
PEER_NKEYS = 128
PEER_TOPK = 16
PEER_HALF = 128

LANE = 128
MIB = 1024 * 1024

_COL_GQ, _COL_GK, _COL_GV, _COL_GR, _COL_DQ, _COL_DK, _COL_DV = 0, 4, 8, 16, 24, 32, 40
_PROJ_W = 48 * LANE

_NT = (((1,), (1,)), ((), ()))
_TN = (((0,), (0,)), ((), ()))


def _params(sem, vmem_mib):
    return pltpu.CompilerParams(dimension_semantics=sem, vmem_limit_bytes=vmem_mib * MIB)


def _rms(x, g):
    return x * lax.rsqrt(jnp.mean(x * x, axis=-1, keepdims=True) + NORM_EPS) * g


def _silu(x):
    return x / (1.0 + jnp.exp(-x))


def _ada_kernel(c_ref, w_ref, b_ref, o_ref):
    s = _silu(c_ref[...]).astype(BF16)
    o_ref[...] = jnp.dot(s, w_ref[...].astype(BF16), preferred_element_type=F32) + b_ref[...]


def _ada(c, w, b):
    B, D = c.shape
    N = w.shape[1]
    tn = 1024
    cp = jnp.zeros((8, D), F32).at[:B].set(c)
    out = pl.pallas_call(
        _ada_kernel,
        grid=(N // tn,),
        in_specs=[pl.BlockSpec((8, D), lambda j: (0, 0)),
                  pl.BlockSpec((D, tn), lambda j: (0, j)),
                  pl.BlockSpec((1, tn), lambda j: (0, j))],
        out_specs=pl.BlockSpec((8, tn), lambda j: (0, j)),
        out_shape=jax.ShapeDtypeStruct((8, N), F32),
        compiler_params=_params(("parallel",), 40),
        name="ada",
    )(cp, w, b.reshape(1, N))
    return out[:B]


def _inproj_kernel(x_ref, g_ref, sc_ref, sh_ref, w_ref, wz_ref, o_ref, z_ref, h_scr):
    @pl.when(pl.program_id(1) == 0)
    def _():
        h = _rms(x_ref[...], g_ref[...]) * (1.0 + sc_ref[0]) + sh_ref[0]
        hb = h.astype(BF16)
        h_scr[...] = hb
        z_ref[...] = jnp.dot(hb, wz_ref[...], preferred_element_type=F32)

    o_ref[...] = jnp.dot(h_scr[...], w_ref[...], preferred_element_type=F32)


def _inproj(x2, g, mod3, w_main, w_z, S):
    T, D = x2.shape
    tm, tn = 1024, 768
    per_b = S // tm
    return pl.pallas_call(
        _inproj_kernel,
        grid=(T // tm, _PROJ_W // tn),
        in_specs=[pl.BlockSpec((tm, D), lambda i, j: (i, 0)),
                  pl.BlockSpec((1, D), lambda i, j: (0, 0)),
                  pl.BlockSpec((1, 1, D), lambda i, j: ((i // per_b) * 6 + 1, 0, 0)),
                  pl.BlockSpec((1, 1, D), lambda i, j: ((i // per_b) * 6 + 0, 0, 0)),
                  pl.BlockSpec((D, tn), lambda i, j: (0, j)),
                  pl.BlockSpec((D, LANE), lambda i, j: (0, 0))],
        out_specs=[pl.BlockSpec((tm, tn), lambda i, j: (i, j)),
                   pl.BlockSpec((tm, LANE), lambda i, j: (i, 0))],
        out_shape=[jax.ShapeDtypeStruct((T, _PROJ_W), F32),
                   jax.ShapeDtypeStruct((T, LANE), F32)],
        scratch_shapes=[pltpu.VMEM((tm, D), BF16)],
        compiler_params=_params(("parallel", "arbitrary"), 48),
        name="inproj",
    )(x2, g, mod3, mod3, w_main, w_z)


def _rope_kernel(pos_ref, f_ref, sg_ref, cs_ref, sn_ref):
    ang = pos_ref[...].astype(F32) * f_ref[...]
    cs_ref[...] = jnp.cos(ang)
    sn_ref[...] = jnp.sin(ang) * sg_ref[...]


def _rope_tables(pos_col):
    T = pos_col.shape[0]
    half = DIL_HD // 2
    inv = jnp.power(ROPE_THETA, -jnp.arange(half, dtype=F32) * 2.0 / DIL_HD)
    freq = jnp.concatenate([inv, inv]).reshape(1, DIL_HD)
    sign = jnp.concatenate([-jnp.ones((half,), F32), jnp.ones((half,), F32)]).reshape(1, DIL_HD)
    tm = 1024
    return pl.pallas_call(
        _rope_kernel,
        grid=(T // tm,),
        in_specs=[pl.BlockSpec((tm, 1), lambda i: (i, 0)),
                  pl.BlockSpec((1, DIL_HD), lambda i: (0, 0)),
                  pl.BlockSpec((1, DIL_HD), lambda i: (0, 0))],
        out_specs=[pl.BlockSpec((tm, DIL_HD), lambda i: (i, 0)),
                   pl.BlockSpec((tm, DIL_HD), lambda i: (i, 0))],
        out_shape=[jax.ShapeDtypeStruct((T, DIL_HD), F32)] * 2,
        compiler_params=_params(("parallel",), 32),
        name="rope",
    )(pos_col, freq, sign)


def _gla_kernel(q_ref, k_ref, v_ref, r_ref, z_ref, wgf_ref, bgf_ref, wgb_ref, bgb_ref, g_ref,
                o_ref, la_scr, o_scr):
    S = q_ref.shape[0]
    C = GLA_CHUNK
    n = S // C
    scale = GLA_DK ** -0.5
    row = lax.broadcasted_iota(jnp.int32, (C, C), 0)
    col = lax.broadcasted_iota(jnp.int32, (C, C), 1)

    def log_gate(w_ref, b_ref):
        zz = jnp.dot(z_ref[...], w_ref[0], precision=HIGHEST, preferred_element_type=F32) + b_ref[0]
        return (jnp.minimum(zz, 0.0) - jnp.log(1.0 + jnp.exp(-jnp.abs(zz)))) * (1.0 / GLA_TAU)

    def run(fwd):
        keep = (col <= row) if fwd else (col >= row)
        tri = keep.astype(F32)

        def body(i, st_t):
            c = i if fwd else n - 1 - i
            sl = pl.ds(pl.multiple_of(c * C, C), C)
            cum = jnp.dot(tri, la_scr[sl, :], precision=HIGHEST, preferred_element_type=F32)
            tot = cum[C - 1:C, :] if fwd else cum[0:1, :]
            kk = k_ref[sl, :]
            qd = (q_ref[sl, :] * scale * jnp.exp(cum)).astype(BF16)
            ki = (kk * jnp.exp(-cum)).astype(BF16)
            kte = (kk * jnp.exp(tot - cum)).astype(BF16)
            vb = v_ref[sl, :].astype(BF16)
            attn = lax.dot_general(qd, ki, _NT, preferred_element_type=F32)
            attn = jnp.where(keep, attn, 0.0).astype(BF16)
            o = jnp.dot(attn, vb, preferred_element_type=F32)
            o = o + lax.dot_general(qd, st_t.astype(BF16), _NT, preferred_element_type=F32)
            if fwd:
                o_scr[sl, :] = o
            else:
                o_scr[sl, :] = o_scr[sl, :] + o
            upd = lax.dot_general(vb, kte, _TN, preferred_element_type=F32)
            return st_t * jnp.exp(tot) + upd

        lax.fori_loop(0, n, body, jnp.zeros((GLA_DV, GLA_DK), F32))

    la_scr[...] = log_gate(wgf_ref, bgf_ref)
    run(True)
    la_scr[...] = log_gate(wgb_ref, bgb_ref)
    run(False)
    y = _rms(o_scr[...], g_ref[0])
    o_ref[...] = y * _silu(r_ref[...])


def _gla(proj, gz, wgf, bgf, wgb, bgb, g_out, B, S):
    T = proj.shape[0]
    H = GLA_HEADS
    return pl.pallas_call(
        _gla_kernel,
        grid=(B, H),
        in_specs=[pl.BlockSpec((S, GLA_DK), lambda b, h: (b, _COL_GQ + h)),
                  pl.BlockSpec((S, GLA_DK), lambda b, h: (b, _COL_GK + h)),
                  pl.BlockSpec((S, GLA_DV), lambda b, h: (b, _COL_GV // 2 + h)),
                  pl.BlockSpec((S, GLA_DV), lambda b, h: (b, _COL_GR // 2 + h)),
                  pl.BlockSpec((S, LANE), lambda b, h: (b, 0)),
                  pl.BlockSpec((1, LANE, GLA_DK), lambda b, h: (h, 0, 0)),
                  pl.BlockSpec((1, 1, GLA_DK), lambda b, h: (h, 0, 0)),
                  pl.BlockSpec((1, LANE, GLA_DK), lambda b, h: (h, 0, 0)),
                  pl.BlockSpec((1, 1, GLA_DK), lambda b, h: (h, 0, 0)),
                  pl.BlockSpec((1, 1, GLA_DV), lambda b, h: (h, 0, 0))],
        out_specs=pl.BlockSpec((S, GLA_DV), lambda b, h: (b, h)),
        out_shape=jax.ShapeDtypeStruct((T, GLA_HEADS * GLA_DV), F32),
        scratch_shapes=[pltpu.VMEM((S, GLA_DK), F32), pltpu.VMEM((S, GLA_DV), F32)],
        compiler_params=_params(("parallel", "parallel"), 56),
        name="gla",
    )(proj, proj, proj, proj, gz, wgf, bgf, wgb, bgb, g_out)


_DIL_QB = 256
_DIL_REACH = max(w // 2 for w, _ in DIL_PATTERNS) // _DIL_QB


def _dil_bias():
    qb = _DIL_QB
    d = np.arange(-_DIL_REACH, _DIL_REACH + 1)[:, None, None] * qb
    delta = d + np.arange(qb)[None, None, :] - np.arange(qb)[None, :, None]
    mult = np.zeros(delta.shape, np.float64)
    for window, dilation in DIL_PATTERNS:
        half = window // (2 * dilation)
        mult += ((delta % dilation) == 0) & (np.abs(delta) <= half * dilation)
    with np.errstate(divide="ignore"):
        bias = np.where(mult > 0, np.log(np.maximum(mult, 1.0)), NEG_INF)
    return jnp.asarray(bias, F32)


def _dil_kernel(q_ref, k_ref, v_ref, cs_ref, sn_ref, bias_ref, o_ref, kr_scr, vb_scr):
    qi = pl.program_id(2)
    nq = pl.num_programs(2)
    QB = _DIL_QB
    half = DIL_HD // 2

    @pl.when(qi == 0)
    def _():
        k = k_ref[...]
        kr_scr[...] = (k * cs_ref[...] + pltpu.roll(k, half, 1) * sn_ref[...]).astype(BF16)
        vb_scr[...] = v_ref[...].astype(BF16)

    r0 = pl.multiple_of(qi * QB, QB)
    q = q_ref[...]
    qr = q * cs_ref[pl.ds(r0, QB), :] + pltpu.roll(q, half, 1) * sn_ref[pl.ds(r0, QB), :]
    qb = (qr * (DIL_HD ** -0.5)).astype(BF16)

    m = jnp.full((QB, 1), NEG_INF, F32)
    l = jnp.zeros((QB, 1), F32)
    acc = jnp.zeros((QB, DIL_HD), F32)
    order = [0] + [s * d for d in range(1, _DIL_REACH + 1) for s in (-1, 1)]
    for d in order:
        kb = qi + d
        valid = jnp.logical_and(kb >= 0, kb < nq)
        k0 = pl.multiple_of(jnp.clip(kb, 0, nq - 1) * QB, QB)
        s = lax.dot_general(qb, kr_scr[pl.ds(k0, QB), :], _NT, preferred_element_type=F32)
        s = jnp.where(valid, s + bias_ref[d + _DIL_REACH], NEG_INF)
        m_new = jnp.maximum(m, jnp.max(s, axis=-1, keepdims=True))
        alpha = jnp.exp(m - m_new)
        p = jnp.exp(s - m_new)
        l = alpha * l + jnp.sum(p, axis=-1, keepdims=True)
        acc = alpha * acc + jnp.dot(p.astype(BF16), vb_scr[pl.ds(k0, QB), :],
                                    preferred_element_type=F32)
        m = m_new
    o_ref[...] = acc / l


def _dil(proj, cs, sn, B, S):
    T = proj.shape[0]
    QB = _DIL_QB
    nq = S // QB
    nb = 2 * _DIL_REACH + 1
    return pl.pallas_call(
        _dil_kernel,
        grid=(B, DIL_HEADS, nq),
        in_specs=[pl.BlockSpec((QB, DIL_HD), lambda b, h, i: (b * nq + i, _COL_DQ + h)),
                  pl.BlockSpec((S, DIL_HD), lambda b, h, i: (b, _COL_DK + h)),
                  pl.BlockSpec((S, DIL_HD), lambda b, h, i: (b, _COL_DV + h)),
                  pl.BlockSpec((S, DIL_HD), lambda b, h, i: (b, 0)),
                  pl.BlockSpec((S, DIL_HD), lambda b, h, i: (b, 0)),
                  pl.BlockSpec((nb, QB, QB), lambda b, h, i: (0, 0, 0))],
        out_specs=pl.BlockSpec((QB, DIL_HD), lambda b, h, i: (b * nq + i, h)),
        out_shape=jax.ShapeDtypeStruct((T, DIL_HEADS * DIL_HD), F32),
        scratch_shapes=[pltpu.VMEM((S, DIL_HD), BF16), pltpu.VMEM((S, DIL_HD), BF16)],
        compiler_params=_params(("parallel", "parallel", "arbitrary"), 48),
        name="dil",
    )(proj, proj, proj, cs, sn, _dil_bias())


def _outproj_kernel(og_ref, od_ref, w_ref, x_ref, ga_ref, o_ref):
    kg = og_ref.shape[1]
    mixed = jnp.dot(og_ref[...].astype(BF16), w_ref[:kg, :], preferred_element_type=F32)
    mixed = mixed + jnp.dot(od_ref[...].astype(BF16), w_ref[kg:, :], preferred_element_type=F32)
    o_ref[...] = x_ref[...] + ga_ref[0] * mixed


def _outproj(o_gla, o_dil, w_out, x2, mod3, S):
    T, D = x2.shape
    tm = 512
    per_b = S // tm
    kg, kd = o_gla.shape[1], o_dil.shape[1]
    return pl.pallas_call(
        _outproj_kernel,
        grid=(T // tm,),
        in_specs=[pl.BlockSpec((tm, kg), lambda i: (i, 0)),
                  pl.BlockSpec((tm, kd), lambda i: (i, 0)),
                  pl.BlockSpec((kg + kd, D), lambda i: (0, 0)),
                  pl.BlockSpec((tm, D), lambda i: (i, 0)),
                  pl.BlockSpec((1, 1, D), lambda i: ((i // per_b) * 6 + 2, 0, 0))],
        out_specs=pl.BlockSpec((tm, D), lambda i: (i, 0)),
        out_shape=jax.ShapeDtypeStruct((T, D), F32),
        compiler_params=_params(("parallel",), 48),
        name="outproj",
    )(o_gla, o_dil, w_out, x2, mod3)


def _pq_kernel(x_ref, g_ref, sc_ref, sh_ref, w_ref, o_ref):
    h = _rms(x_ref[...], g_ref[...]) * (1.0 + sc_ref[0]) + sh_ref[0]
    o_ref[...] = jnp.dot(h.astype(BF16), w_ref[...], preferred_element_type=F32)


def _pq(x1, g, mod3, wq, S):
    T, D = x1.shape
    N = wq.shape[1]
    tm = 512
    per_b = S // tm
    return pl.pallas_call(
        _pq_kernel,
        grid=(T // tm,),
        in_specs=[pl.BlockSpec((tm, D), lambda i: (i, 0)),
                  pl.BlockSpec((1, D), lambda i: (0, 0)),
                  pl.BlockSpec((1, 1, D), lambda i: ((i // per_b) * 6 + 4, 0, 0)),
                  pl.BlockSpec((1, 1, D), lambda i: ((i // per_b) * 6 + 3, 0, 0)),
                  pl.BlockSpec((D, N), lambda i: (0, 0))],
        out_specs=pl.BlockSpec((tm, N), lambda i: (i, 0)),
        out_shape=jax.ShapeDtypeStruct((T, N), F32),
        compiler_params=_params(("parallel",), 48),
        name="pq",
    )(x1, g, mod3, mod3, wq)


def _top_rows(s, k, payload=None):
    n_rows = s.shape[0]
    rid = lax.broadcasted_iota(jnp.int32, s.shape, 0)
    vals, picks = [], []
    for _ in range(k):
        m = jnp.max(s, axis=0, keepdims=True)
        pos = jnp.min(jnp.where(s == m, rid, n_rows), axis=0, keepdims=True)
        hit = rid == pos
        vals.append(m)
        if payload is None:
            picks.append(pos)
        else:
            picks.append(jnp.sum(jnp.where(hit, payload, 0), axis=0, keepdims=True))
        s = jnp.where(hit, -jnp.inf, s)
    return jnp.concatenate(vals, axis=0), jnp.concatenate(picks, axis=0)


def _topk_kernel(q_ref, keys_ref, idx_ref, gate_ref):
    K = PEER_TOPK
    for h in range(PEER_HEADS):
        tops = []
        for half in range(2):
            c0 = (h * 2 + half) * PEER_HALF
            qh = q_ref[:, c0:c0 + PEER_HALF].astype(BF16)
            sc = lax.dot_general(keys_ref[h, half], qh, _NT, preferred_element_type=F32)
            tops.append(_top_rows(sc, K))
        (s0, i0), (s1, i1) = tops
        cand_s = jnp.concatenate([s0[i:i + 1] + s1 for i in range(K)], axis=0)
        cand_i = jnp.concatenate([i0[i:i + 1] * PEER_NKEYS + i1 for i in range(K)], axis=0)
        best, idx = _top_rows(cand_s, K, payload=cand_i)
        e = jnp.exp(best - best[0:1])
        gate = e / jnp.sum(e, axis=0, keepdims=True)
        idx_ref[h * K:(h + 1) * K, :] = idx
        gate_ref[h * K:(h + 1) * K, :] = gate


def _topk(qp, keys_bf):
    T, N = qp.shape
    tt = 256
    HK = PEER_HEADS * PEER_TOPK
    return pl.pallas_call(
        _topk_kernel,
        grid=(T // tt,),
        in_specs=[pl.BlockSpec((tt, N), lambda i: (i, 0)),
                  pl.BlockSpec(keys_bf.shape, lambda i: (0, 0, 0, 0))],
        out_specs=[pl.BlockSpec((HK, tt), lambda i: (0, i)),
                   pl.BlockSpec((HK, tt), lambda i: (0, i))],
        out_shape=[jax.ShapeDtypeStruct((HK, T), jnp.int32),
                   jax.ShapeDtypeStruct((HK, T), F32)],
        compiler_params=_params(("parallel",), 32),
        name="topk",
    )(qp, keys_bf)


_PEER_TB = 128
_PEER_SUB = 8
_SUBLANES = 8
_PEER_CHUNKS = 8
_SC_LANES = 16
_SC_ROWS = 16


def _peer_u_kernel(idx_hbm, gate_ref, x1_ref, gn_ref, sc_ref, sh_ref, u_hbm, w_ref,
                   idx_smem, ub0, ub1, h_scr, sem_i, sem_u, *, step0):
    HK = PEER_HEADS * PEER_TOPK
    TB, SUB = _PEER_TB, _PEER_SUB
    R = SUB * HK
    N = TB * HK
    nsub = TB // SUB
    D = x1_ref.shape[1]
    nchunk = D // LANE
    tiles = HK // _SUBLANES
    i = pl.program_id(0)
    n = pl.num_programs(0)
    cur = lax.rem(i, 2) * N
    nxt = N - cur
    more = i + 1 < n
    ubufs = (ub0, ub1)

    def idx_copy(step, base):
        return pltpu.make_async_copy(idx_hbm.at[pl.ds((step0 + step) * N, N)],
                                     idx_smem.at[pl.ds(base, N)], sem_i)

    def issue_token(base, t, slot):
        for k in range(HK):
            e = idx_smem[base + t * HK + k]
            rt, s = t * tiles + k // _SUBLANES, k % _SUBLANES
            pltpu.make_async_copy(u_hbm.at[e], ubufs[slot].at[rt, :, s, :], sem_u.at[slot]).start()

    def wait(slot):
        pltpu.make_async_copy(ubufs[slot], ubufs[slot], sem_u.at[slot]).wait()

    @pl.when(i == 0)
    def _():
        first = idx_copy(0, 0)
        first.start()
        first.wait()
        for t in range(SUB):
            issue_token(0, t, 0)

    @pl.when(more)
    def _():
        idx_copy(i + 1, nxt).start()

    h_scr[...] = _rms(x1_ref[...], gn_ref[...]) * (1.0 + sc_ref[0]) + sh_ref[0]
    lane = lax.broadcasted_iota(jnp.int32, (HK, TB), 1)

    def compute_token(j, t, slot, wacc):
        ub = ubufs[slot]
        tok = j * SUB + t
        xt = h_scr[pl.ds(tok, 1), :]
        rows = slice(t * tiles, (t + 1) * tiles)
        part = ub[rows, 0].reshape(HK, LANE) * xt[:, 0:LANE]
        for c in range(1, nchunk):
            part = part + ub[rows, c].reshape(HK, LANE) * xt[:, c * LANE:(c + 1) * LANE]
        a = jnp.sum(part, axis=1, keepdims=True)
        hit = lane == tok
        g = jnp.sum(jnp.where(hit, gate_ref[...], 0.0), axis=1, keepdims=True)
        wgt = g * (0.5 * a * (1.0 + lax.erf(a * (2.0 ** -0.5))))
        return jnp.where(hit, wgt, wacc)

    def half(j, slot, next_base, wacc):
        wait(slot)
        for t in range(SUB):
            issue_token(next_base, t, 1 - slot)
            wacc = compute_token(j, t, slot, wacc)
        return wacc

    def pair(jj, wacc):
        j0 = 2 * jj
        wacc = half(j0, 0, cur + (j0 + 1) * R, wacc)
        last = jj == nsub // 2 - 1

        @pl.when(jnp.logical_and(last, more))
        def _():
            idx_copy(i + 1, nxt).wait()

        after = jnp.where(more, nxt, cur)
        return half(j0 + 1, 1, jnp.where(last, after, cur + (j0 + 2) * R), wacc)

    wacc = lax.fori_loop(0, nsub // 2, pair, jnp.zeros((HK, TB), F32))

    @pl.when(jnp.logical_not(more))
    def _():
        wait(0)

    w_ref[:, :HK] = jnp.zeros((TB, HK), F32)
    w_ref[:, HK:] = wacc.T


def _peer_u(idx_flat, gate_t, x1, g_norm, mod3, u3, S, step0, nsteps):
    T, D = x1.shape
    HK = PEER_HEADS * PEER_TOPK
    TB, SUB = _PEER_TB, _PEER_SUB
    per_b = S // TB
    modrow = lambda k: (lambda i: (((step0 + i) // per_b) * 6 + k, 0, 0))
    gbuf = pltpu.VMEM((SUB * HK // _SUBLANES, D // LANE, _SUBLANES, LANE), F32)
    return pl.pallas_call(
        functools.partial(_peer_u_kernel, step0=step0),
        grid=(nsteps,),
        in_specs=[pl.BlockSpec(memory_space=pl.ANY),
                  pl.BlockSpec((HK, TB), lambda i: (0, step0 + i)),
                  pl.BlockSpec((TB, D), lambda i: (step0 + i, 0)),
                  pl.BlockSpec((1, D), lambda i: (0, 0)),
                  pl.BlockSpec((1, 1, D), modrow(4)),
                  pl.BlockSpec((1, 1, D), modrow(3)),
                  pl.BlockSpec(memory_space=pl.ANY)],
        out_specs=pl.BlockSpec((TB, 2 * HK), lambda i: (i, 0)),
        out_shape=jax.ShapeDtypeStruct((nsteps * TB, 2 * HK), F32),
        scratch_shapes=[pltpu.SMEM((2 * TB * HK,), jnp.int32),
                        gbuf, gbuf,
                        pltpu.VMEM((TB, D), F32),
                        pltpu.SemaphoreType.DMA,
                        pltpu.SemaphoreType.DMA((2,))],
        compiler_params=_params(("arbitrary",), 40),
        name="peer_u",
    )(idx_flat, gate_t, x1, g_norm, mod3, mod3, u3)


def _sc_peer_v(v_tab, idx_flat, wgt, tok_base):
    E, D = v_tab.shape
    Tc, HK = wgt.shape[0], wgt.shape[1] // 2
    info = plsc.get_sparse_core_info()
    nw = info.num_cores * info.num_subcores
    tpw = Tc // nw
    CH = _SC_ROWS
    nch = HK // CH
    nsl = D // _SC_LANES
    mesh = plsc.VectorSubcoreMesh(core_axis_name="c", subcore_axis_name="s")

    @functools.partial(
        pl.kernel, mesh=mesh, out_type=jax.ShapeDtypeStruct((Tc, D), F32),
        scratch_types=[pltpu.VMEM((tpw * HK,), jnp.int32), pltpu.VMEM((2 * HK,), F32),
                       pltpu.VMEM((D,), F32),
                       pltpu.VMEM((CH, D), F32), pltpu.VMEM((CH, D), F32),
                       pltpu.SemaphoreType.DMA, pltpu.SemaphoreType.DMA],
        compiler_params=pltpu.CompilerParams(needs_layout_passes=False),
        name="sc_peer_v",
    )
    def k(tab_hbm, idx_hbm, w_hbm, o_hbm, idx_v, w_v, o_v, buf0, buf1, g0, g1):
        wid = lax.axis_index("s") * info.num_cores + lax.axis_index("c")
        tok0 = wid * tpw
        pltpu.sync_copy(idx_hbm.at[pl.ds((tok_base + tok0) * HK, tpw * HK)], idx_v)
        bufs, gs = (buf0, buf1), (g0, g1)

        def gather(g, b):
            return pltpu.make_async_copy(tab_hbm.at[idx_v.at[pl.ds(g * CH, CH)]], bufs[b], gs[b])

        gather(0, 0).start()

        @pl.loop(0, tpw)
        def _(t):
            pltpu.sync_copy(w_hbm.at[tok0 + t], w_v)

            @pl.loop(0, nsl)
            def _(c):
                o_v[pl.ds(pl.multiple_of(c * _SC_LANES, _SC_LANES), _SC_LANES)] = jnp.zeros((_SC_LANES,), F32)

            for ch in range(nch):
                b = ch % 2
                g = t * nch + ch
                gather(g, b).wait()

                @pl.when(g + 1 < tpw * nch)
                def _():
                    gather(g + 1, 1 - b).start()

                ws = [plsc.load_gather(w_v, [jnp.full((_SC_LANES,), HK + ch * CH + r, jnp.int32)])
                      for r in range(CH)]

                @pl.loop(0, nsl, step=2)
                def _(c):
                    for half in range(2):
                        off = pl.multiple_of((c + half) * _SC_LANES, _SC_LANES)
                        parts = [ws[r] * bufs[b][r, pl.ds(off, _SC_LANES)] for r in range(CH)]
                        while len(parts) > 1:
                            parts = [parts[p] + parts[p + 1] for p in range(0, len(parts), 2)]
                        o_v[pl.ds(off, _SC_LANES)] = o_v[pl.ds(off, _SC_LANES)] + parts[0]

            pltpu.sync_copy(o_v, o_hbm.at[tok0 + t])

    return k(v_tab, idx_flat, wgt)


def _peer_fin_kernel(x_ref, p_ref, ga_ref, gf_ref, o_ref):
    o_ref[...] = _rms(x_ref[...] + ga_ref[0] * p_ref[...], gf_ref[...])


def _peer_fin(x1, po, mod3, g_final, S):
    T, D = x1.shape
    tm = 512
    per_b = S // tm
    return pl.pallas_call(
        _peer_fin_kernel,
        grid=(T // tm,),
        in_specs=[pl.BlockSpec((tm, D), lambda i: (i, 0)),
                  pl.BlockSpec((tm, D), lambda i: (i, 0)),
                  pl.BlockSpec((1, 1, D), lambda i: ((i // per_b) * 6 + 5, 0, 0)),
                  pl.BlockSpec((1, D), lambda i: (0, 0))],
        out_specs=pl.BlockSpec((tm, D), lambda i: (i, 0)),
        out_shape=jax.ShapeDtypeStruct((T, D), F32),
        compiler_params=_params(("parallel",), 40),
        name="peer_fin",
    )(x1, po, mod3, g_final)


def _pad_gate(w, lo):
    rank = w.shape[0]
    wh = w.reshape(rank, GLA_HEADS, GLA_DK).transpose(1, 0, 2)
    return jnp.zeros((GLA_HEADS, LANE, GLA_DK), F32).at[:, lo:lo + rank, :].set(wh)


def kernel(x, c, positions, w_ada, b_ada, g_norm_mix, w_in, w_gate_f, b_gate_f, w_gate_b, b_gate_b,
           g_gla_out, w_out, g_norm_ffn, w_peer_q, peer_sub_keys, peer_u, peer_v, g_final):
    B, S, D = x.shape
    T = B * S
    depth = w_ada.shape[0]
    assert depth == 1, "the final norm is fused into the last PEER call; one layer only"
    xt = x.reshape(T, D)
    cs, sn = _rope_tables(positions.reshape(T, 1))
    gz0 = 2 * GLA_HEADS * GLA_DK + 2 * GLA_HEADS * GLA_DV
    gz1 = gz0 + 2 * GLA_GATE_RANK
    for l in range(depth):
        mod3 = _ada(c, w_ada[l], b_ada[l]).reshape(B * 6, 1, D)
        w_main = jnp.concatenate([w_in[l][:, :gz0], w_in[l][:, gz1:]], axis=1).astype(BF16)
        w_z = jnp.pad(w_in[l][:, gz0:gz1], ((0, 0), (0, LANE - (gz1 - gz0)))).astype(BF16)
        proj, gz = _inproj(xt, g_norm_mix[l].reshape(1, D), mod3, w_main, w_z, S)
        o_gla = _gla(proj, gz,
                     _pad_gate(w_gate_f[l], 0), b_gate_f[l].reshape(GLA_HEADS, 1, GLA_DK),
                     _pad_gate(w_gate_b[l], GLA_GATE_RANK), b_gate_b[l].reshape(GLA_HEADS, 1, GLA_DK),
                     g_gla_out[l].reshape(GLA_HEADS, 1, GLA_DV), B, S)
        o_dil = _dil(proj, cs, sn, B, S)
        x1 = _outproj(o_gla, o_dil, w_out[l].astype(BF16), xt, mod3, S)
        qp = _pq(x1, g_norm_ffn[l].reshape(1, D), mod3, w_peer_q[l].astype(BF16), S)
        idx_t, gate_t = _topk(qp, peer_sub_keys[l].astype(BF16))
        idx_flat = idx_t.T.reshape(-1)
        E = peer_u.shape[1]
        u3 = peer_u[l].reshape(E, D // LANE, LANE)
        steps = T // _PEER_TB // _PEER_CHUNKS
        outs = []
        for ck in range(_PEER_CHUNKS):
            wgt = _peer_u(idx_flat, gate_t, x1, g_norm_ffn[l].reshape(1, D), mod3, u3, S, ck * steps, steps)
            outs.append(_sc_peer_v(peer_v[l], idx_flat, wgt, ck * steps * _PEER_TB))
        xt = _peer_fin(x1, jnp.concatenate(outs, axis=0), mod3, g_final.reshape(1, D), S)
    return xt.reshape(B, S, D)
```

```python
import functools
import math

import numpy as np
import jax
import jax.numpy as jnp
from jax import lax
from jax.experimental import pallas as pl
from jax.experimental.pallas import tpu as pltpu
from jax.experimental.pallas import tpu_sc as plsc

F32 = jnp.float32
BF16 = jnp.bfloat16
HIGHEST = lax.Precision.HIGHEST

NORM_EPS = 1e-6
GLA_HEADS = 4
GLA_DK = 128
GLA_DV = 256
GLA_GATE_RANK = 16
GLA_TAU = 16.0
GLA_CHUNK = 64
DIL_HD = 128
DIL_HEADS = 8
DIL_PATTERNS = ((128, 1), (512, 4), (2048, 16))
ROPE_THETA = 10000.0
NEG_INF = -1e30
PEER_HEADS = 8
PEER_NKEYS = 128
PEER_TOPK = 16
PEER_HALF = 128

LANE = 128
MIB = 1024 * 1024

_COL_GQ, _COL_GK, _COL_GV, _COL_GR, _COL_DQ, _COL_DK, _COL_DV = 0, 4, 8, 16, 24, 32, 40
_PROJ_W = 48 * LANE

_NT = (((1,), (1,)), ((), ()))
_TN = (((0,), (0,)), ((), ()))


def _params(sem, vmem_mib):
    return pltpu.CompilerParams(dimension_semantics=sem, vmem_limit_bytes=vmem_mib * MIB)


def _rms(x, g):
    return x * lax.rsqrt(jnp.mean(x * x, axis=-1, keepdims=True) + NORM_EPS) * g


def _silu(x):
    return x / (1.0 + jnp.exp(-x))


def _ada_kernel(c_ref, w_ref, b_ref, o_ref):
    s = _silu(c_ref[...]).astype(BF16)
    o_ref[...] = jnp.dot(s, w_ref[...].astype(BF16), preferred_element_type=F32) + b_ref[...]


def _ada(c, w, b):
    B, D = c.shape
    N = w.shape[1]
    tn = 1024
    cp = jnp.zeros((8, D), F32).at[:B].set(c)
    out = pl.pallas_call(
        _ada_kernel,
        grid=(N // tn,),
        in_specs=[pl.BlockSpec((8, D), lambda j: (0, 0)),
                  pl.BlockSpec((D, tn), lambda j: (0, j)),
                  pl.BlockSpec((1, tn), lambda j: (0, j))],
        out_specs=pl.BlockSpec((8, tn), lambda j: (0, j)),
        out_shape=jax.ShapeDtypeStruct((8, N), F32),
        compiler_params=_params(("parallel",), 40),
        name="ada",
    )(cp, w, b.reshape(1, N))
    return out[:B]


def _inproj_kernel(x_ref, g_ref, sc_ref, sh_ref, w_ref, wz_ref, o_ref, z_ref, h_scr):
    @pl.when(pl.program_id(1) == 0)
    def _():
        h = _rms(x_ref[...], g_ref[...]) * (1.0 + sc_ref[0]) + sh_ref[0]
        hb = h.astype(BF16)
        h_scr[...] = hb
        z_ref[...] = jnp.dot(hb, wz_ref[...], preferred_element_type=F32)

    o_ref[...] = jnp.dot(h_scr[...], w_ref[...], preferred_element_type=F32)


def _inproj(x2, g, mod3, w_main, w_z, S):
    T, D = x2.shape
    tm, tn = 1024, 768
    per_b = S // tm
    return pl.pallas_call(
        _inproj_kernel,
        grid=(T // tm, _PROJ_W // tn),
        in_specs=[pl.BlockSpec((tm, D), lambda i, j: (i, 0)),
                  pl.BlockSpec((1, D), lambda i, j: (0, 0)),
                  pl.BlockSpec((1, 1, D), lambda i, j: ((i // per_b) * 6 + 1, 0, 0)),
                  pl.BlockSpec((1, 1, D), lambda i, j: ((i // per_b) * 6 + 0, 0, 0)),
                  pl.BlockSpec((D, tn), lambda i, j: (0, j)),
                  pl.BlockSpec((D, LANE), lambda i, j: (0, 0))],
        out_specs=[pl.BlockSpec((tm, tn), lambda i, j: (i, j)),
                   pl.BlockSpec((tm, LANE), lambda i, j: (i, 0))],
        out_shape=[jax.ShapeDtypeStruct((T, _PROJ_W), F32),
                   jax.ShapeDtypeStruct((T, LANE), F32)],
        scratch_shapes=[pltpu.VMEM((tm, D), BF16)],
        compiler_params=_params(("parallel", "arbitrary"), 48),
        name="inproj",
    )(x2, g, mod3, mod3, w_main, w_z)


def _rope_kernel(pos_ref, f_ref, sg_ref, cs_ref, sn_ref):
    ang = pos_ref[...].astype(F32) * f_ref[...]
    cs_ref[...] = jnp.cos(ang)
    sn_ref[...] = jnp.sin(ang) * sg_ref[...]


def _rope_tables(pos_col):
    T = pos_col.shape[0]
    half = DIL_HD // 2
    inv = jnp.power(ROPE_THETA, -jnp.arange(half, dtype=F32) * 2.0 / DIL_HD)
    freq = jnp.concatenate([inv, inv]).reshape(1, DIL_HD)
    sign = jnp.concatenate([-jnp.ones((half,), F32), jnp.ones((half,), F32)]).reshape(1, DIL_HD)
    tm = 1024
    return pl.pallas_call(
        _rope_kernel,
        grid=(T // tm,),
        in_specs=[pl.BlockSpec((tm, 1), lambda i: (i, 0)),
                  pl.BlockSpec((1, DIL_HD), lambda i: (0, 0)),
                  pl.BlockSpec((1, DIL_HD), lambda i: (0, 0))],
        out_specs=[pl.BlockSpec((tm, DIL_HD), lambda i: (i, 0)),
                   pl.BlockSpec((tm, DIL_HD), lambda i: (i, 0))],
        out_shape=[jax.ShapeDtypeStruct((T, DIL_HD), F32)] * 2,
        compiler_params=_params(("parallel",), 32),
        name="rope",
    )(pos_col, freq, sign)


def _gla_kernel(q_ref, k_ref, v_ref, r_ref, z_ref, wgf_ref, bgf_ref, wgb_ref, bgb_ref, g_ref,
                o_ref, la_scr, o_scr):
    S = q_ref.shape[0]
    C = GLA_CHUNK
    n = S // C
    scale = GLA_DK ** -0.5
    row = lax.broadcasted_iota(jnp.int32, (C, C), 0)
    col = lax.broadcasted_iota(jnp.int32, (C, C), 1)

    def log_gate(w_ref, b_ref):
        zz = jnp.dot(z_ref[...], w_ref[0], precision=HIGHEST, preferred_element_type=F32) + b_ref[0]
        return (jnp.minimum(zz, 0.0) - jnp.log(1.0 + jnp.exp(-jnp.abs(zz)))) * (1.0 / GLA_TAU)

    def run(fwd):
        keep = (col <= row) if fwd else (col >= row)
        tri = keep.astype(F32)

        def body(i, st_t):
            c = i if fwd else n - 1 - i
            sl = pl.ds(pl.multiple_of(c * C, C), C)
            cum = jnp.dot(tri, la_scr[sl, :], precision=HIGHEST, preferred_element_type=F32)
            tot = cum[C - 1:C, :] if fwd else cum[0:1, :]
            kk = k_ref[sl, :]
            qd = (q_ref[sl, :] * scale * jnp.exp(cum)).astype(BF16)
            ki = (kk * jnp.exp(-cum)).astype(BF16)
            kte = (kk * jnp.exp(tot - cum)).astype(BF16)
            vb = v_ref[sl, :].astype(BF16)
            attn = lax.dot_general(qd, ki, _NT, preferred_element_type=F32)
            attn = jnp.where(keep, attn, 0.0).astype(BF16)
            o = jnp.dot(attn, vb, preferred_element_type=F32)
            o = o + lax.dot_general(qd, st_t.astype(BF16), _NT, preferred_element_type=F32)
            if fwd:
                o_scr[sl, :] = o
            else:
                o_scr[sl, :] = o_scr[sl, :] + o
            upd = lax.dot_general(vb, kte, _TN, preferred_element_type=F32)
            return st_t * jnp.exp(tot) + upd

        lax.fori_loop(0, n, body, jnp.zeros((GLA_DV, GLA_DK), F32))

    la_scr[...] = log_gate(wgf_ref, bgf_ref)
    run(True)
    la_scr[...] = log_gate(wgb_ref, bgb_ref)
    run(False)
    y = _rms(o_scr[...], g_ref[0])
    o_ref[...] = y * _silu(r_ref[...])


def _gla(proj, gz, wgf, bgf, wgb, bgb, g_out, B, S):
    T = proj.shape[0]
    H = GLA_HEADS
    return pl.pallas_call(
        _gla_kernel,
        grid=(B, H),
        in_specs=[pl.BlockSpec((S, GLA_DK), lambda b, h: (b, _COL_GQ + h)),
                  pl.BlockSpec((S, GLA_DK), lambda b, h: (b, _COL_GK + h)),
                  pl.BlockSpec((S, GLA_DV), lambda b, h: (b, _COL_GV // 2 + h)),
                  pl.BlockSpec((S, GLA_DV), lambda b, h: (b, _COL_GR // 2 + h)),
                  pl.BlockSpec((S, LANE), lambda b, h: (b, 0)),
                  pl.BlockSpec((1, LANE, GLA_DK), lambda b, h: (h, 0, 0)),
                  pl.BlockSpec((1, 1, GLA_DK), lambda b, h: (h, 0, 0)),
                  pl.BlockSpec((1, LANE, GLA_DK), lambda b, h: (h, 0, 0)),
                  pl.BlockSpec((1, 1, GLA_DK), lambda b, h: (h, 0, 0)),
                  pl.BlockSpec((1, 1, GLA_DV), lambda b, h: (h, 0, 0))],
        out_specs=pl.BlockSpec((S, GLA_DV), lambda b, h: (b, h)),
        out_shape=jax.ShapeDtypeStruct((T, GLA_HEADS * GLA_DV), F32),
        scratch_shapes=[pltpu.VMEM((S, GLA_DK), F32), pltpu.VMEM((S, GLA_DV), F32)],
        compiler_params=_params(("parallel", "parallel"), 56),
        name="gla",
    )(proj, proj, proj, proj, gz, wgf, bgf, wgb, bgb, g_out)


_DIL_QB = 256
_DIL_REACH = max(w // 2 for w, _ in DIL_PATTERNS) // _DIL_QB


def _dil_bias():
    qb = _DIL_QB
    d = np.arange(-_DIL_REACH, _DIL_REACH + 1)[:, None, None] * qb
    delta = d + np.arange(qb)[None, None, :] - np.arange(qb)[None, :, None]
    mult = np.zeros(delta.shape, np.float64)
    for window, dilation in DIL_PATTERNS:
        half = window // (2 * dilation)
        mult += ((delta % dilation) == 0) & (np.abs(delta) <= half * dilation)
    with np.errstate(divide="ignore"):
        bias = np.where(mult > 0, np.log(np.maximum(mult, 1.0)), NEG_INF)
    return jnp.asarray(bias, F32)


def _dil_kernel(q_ref, k_ref, v_ref, cs_ref, sn_ref, bias_ref, o_ref, kr_scr, vb_scr):
    qi = pl.program_id(2)
    nq = pl.num_programs(2)
    QB = _DIL_QB
    half = DIL_HD // 2

    @pl.when(qi == 0)
    def _():
        k = k_ref[...]
        kr_scr[...] = (k * cs_ref[...] + pltpu.roll(k, half, 1) * sn_ref[...]).astype(BF16)
        vb_scr[...] = v_ref[...].astype(BF16)

    r0 = pl.multiple_of(qi * QB, QB)
    q = q_ref[...]
    qr = q * cs_ref[pl.ds(r0, QB), :] + pltpu.roll(q, half, 1) * sn_ref[pl.ds(r0, QB), :]
    qb = (qr * (DIL_HD ** -0.5)).astype(BF16)

    m = jnp.full((QB, 1), NEG_INF, F32)
    l = jnp.zeros((QB, 1), F32)
    acc = jnp.zeros((QB, DIL_HD), F32)
    order = [0] + [s * d for d in range(1, _DIL_REACH + 1) for s in (-1, 1)]
    for d in order:
        kb = qi + d
        valid = jnp.logical_and(kb >= 0, kb < nq)
        k0 = pl.multiple_of(jnp.clip(kb, 0, nq - 1) * QB, QB)
        s = lax.dot_general(qb, kr_scr[pl.ds(k0, QB), :], _NT, preferred_element_type=F32)
        s = jnp.where(valid, s + bias_ref[d + _DIL_REACH], NEG_INF)
        m_new = jnp.maximum(m, jnp.max(s, axis=-1, keepdims=True))
        alpha = jnp.exp(m - m_new)
        p = jnp.exp(s - m_new)
        l = alpha * l + jnp.sum(p, axis=-1, keepdims=True)
        acc = alpha * acc + jnp.dot(p.astype(BF16), vb_scr[pl.ds(k0, QB), :],
                                    preferred_element_type=F32)
        m = m_new
    o_ref[...] = acc / l


def _dil(proj, cs, sn, B, S):
    T = proj.shape[0]
    QB = _DIL_QB
    nq = S // QB
    nb = 2 * _DIL_REACH + 1
    return pl.pallas_call(
        _dil_kernel,
        grid=(B, DIL_HEADS, nq),
        in_specs=[pl.BlockSpec((QB, DIL_HD), lambda b, h, i: (b * nq + i, _COL_DQ + h)),
                  pl.BlockSpec((S, DIL_HD), lambda b, h, i: (b, _COL_DK + h)),
                  pl.BlockSpec((S, DIL_HD), lambda b, h, i: (b, _COL_DV + h)),
                  pl.BlockSpec((S, DIL_HD), lambda b, h, i: (b, 0)),
                  pl.BlockSpec((S, DIL_HD), lambda b, h, i: (b, 0)),
                  pl.BlockSpec((nb, QB, QB), lambda b, h, i: (0, 0, 0))],
        out_specs=pl.BlockSpec((QB, DIL_HD), lambda b, h, i: (b * nq + i, h)),
        out_shape=jax.ShapeDtypeStruct((T, DIL_HEADS * DIL_HD), F32),
        scratch_shapes=[pltpu.VMEM((S, DIL_HD), BF16), pltpu.VMEM((S, DIL_HD), BF16)],
        compiler_params=_params(("parallel", "parallel", "arbitrary"), 48),
        name="dil",
    )(proj, proj, proj, cs, sn, _dil_bias())


def _outproj_kernel(og_ref, od_ref, w_ref, x_ref, ga_ref, o_ref):
    kg = og_ref.shape[1]
    mixed = jnp.dot(og_ref[...].astype(BF16), w_ref[:kg, :], preferred_element_type=F32)
    mixed = mixed + jnp.dot(od_ref[...].astype(BF16), w_ref[kg:, :], preferred_element_type=F32)
    o_ref[...] = x_ref[...] + ga_ref[0] * mixed


def _outproj(o_gla, o_dil, w_out, x2, mod3, S):
    T, D = x2.shape
    tm = 512
    per_b = S // tm
    kg, kd = o_gla.shape[1], o_dil.shape[1]
    return pl.pallas_call(
        _outproj_kernel,
        grid=(T // tm,),
        in_specs=[pl.BlockSpec((tm, kg), lambda i: (i, 0)),
                  pl.BlockSpec((tm, kd), lambda i: (i, 0)),
                  pl.BlockSpec((kg + kd, D), lambda i: (0, 0)),
                  pl.BlockSpec((tm, D), lambda i: (i, 0)),
                  pl.BlockSpec((1, 1, D), lambda i: ((i // per_b) * 6 + 2, 0, 0))],
        out_specs=pl.BlockSpec((tm, D), lambda i: (i, 0)),
        out_shape=jax.ShapeDtypeStruct((T, D), F32),
        compiler_params=_params(("parallel",), 48),
        name="outproj",
    )(o_gla, o_dil, w_out, x2, mod3)


def _pq_kernel(x_ref, g_ref, sc_ref, sh_ref, w_ref, o_ref):
    h = _rms(x_ref[...], g_ref[...]) * (1.0 + sc_ref[0]) + sh_ref[0]
    o_ref[...] = jnp.dot(h.astype(BF16), w_ref[...], preferred_element_type=F32)


def _pq(x1, g, mod3, wq, S):
    T, D = x1.shape
    N = wq.shape[1]
    tm = 512
    per_b = S // tm
    return pl.pallas_call(
        _pq_kernel,
        grid=(T // tm,),
        in_specs=[pl.BlockSpec((tm, D), lambda i: (i, 0)),
                  pl.BlockSpec((1, D), lambda i: (0, 0)),
                  pl.BlockSpec((1, 1, D), lambda i: ((i // per_b) * 6 + 4, 0, 0)),
                  pl.BlockSpec((1, 1, D), lambda i: ((i // per_b) * 6 + 3, 0, 0)),
                  pl.BlockSpec((D, N), lambda i: (0, 0))],
        out_specs=pl.BlockSpec((tm, N), lambda i: (i, 0)),
        out_shape=jax.ShapeDtypeStruct((T, N), F32),
        compiler_params=_params(("parallel",), 48),
        name="pq",
    )(x1, g, mod3, mod3, wq)


def _top_rows(s, k, payload=None):
    n_rows = s.shape[0]
    rid = lax.broadcasted_iota(jnp.int32, s.shape, 0)
    vals, picks = [], []
    for _ in range(k):
        m = jnp.max(s, axis=0, keepdims=True)
        pos = jnp.min(jnp.where(s == m, rid, n_rows), axis=0, keepdims=True)
        hit = rid == pos
        vals.append(m)
        if payload is None:
            picks.append(pos)
        else:
            picks.append(jnp.sum(jnp.where(hit, payload, 0), axis=0, keepdims=True))
        s = jnp.where(hit, -jnp.inf, s)
    return jnp.concatenate(vals, axis=0), jnp.concatenate(picks, axis=0)


def _topk_kernel(q_ref, keys_ref, idx_ref, gate_ref):
    K = PEER_TOPK
    for h in range(PEER_HEADS):
        tops = []
        for half in range(2):
            c0 = (h * 2 + half) * PEER_HALF
            qh = q_ref[:, c0:c0 + PEER_HALF].astype(BF16)
            sc = lax.dot_general(keys_ref[h, half], qh, _NT, preferred_element_type=F32)
            tops.append(_top_rows(sc, K))
        (s0, i0), (s1, i1) = tops
        cand_s = jnp.concatenate([s0[i:i + 1] + s1 for i in range(K)], axis=0)
        cand_i = jnp.concatenate([i0[i:i + 1] * PEER_NKEYS + i1 for i in range(K)], axis=0)
        best, idx = _top_rows(cand_s, K, payload=cand_i)
        e = jnp.exp(best - best[0:1])
        gate = e / jnp.sum(e, axis=0, keepdims=True)
        idx_ref[h * K:(h + 1) * K, :] = idx
        gate_ref[h * K:(h + 1) * K, :] = gate


def _topk(qp, keys_bf):
    T, N = qp.shape
    tt = 256
    HK = PEER_HEADS * PEER_TOPK
    return pl.pallas_call(
        _topk_kernel,
        grid=(T // tt,),
        in_specs=[pl.BlockSpec((tt, N), lambda i: (i, 0)),
                  pl.BlockSpec(keys_bf.shape, lambda i: (0, 0, 0, 0))],
        out_specs=[pl.BlockSpec((HK, tt), lambda i: (0, i)),
                   pl.BlockSpec((HK, tt), lambda i: (0, i))],
        out_shape=[jax.ShapeDtypeStruct((HK, T), jnp.int32),
                   jax.ShapeDtypeStruct((HK, T), F32)],
        compiler_params=_params(("parallel",), 32),
        name="topk",
    )(qp, keys_bf)


_PEER_TB = 128
_PEER_SUB = 8
_SUBLANES = 8
_PEER_CHUNKS = 8
_SC_LANES = 16
_SC_ROWS = 16


def _peer_u_kernel(idx_hbm, gate_ref, x1_ref, gn_ref, sc_ref, sh_ref, u_hbm, w_ref,
                   idx_smem, ub0, ub1, h_scr, sem_i, sem_u, *, step0):
    HK = PEER_HEADS * PEER_TOPK
    TB, SUB = _PEER_TB, _PEER_SUB
    R = SUB * HK
    N = TB * HK
    nsub = TB // SUB
    D = x1_ref.shape[1]
    nchunk = D // LANE
    tiles = HK // _SUBLANES
    i = pl.program_id(0)
    n = pl.num_programs(0)
    cur = lax.rem(i, 2) * N
    nxt = N - cur
    more = i + 1 < n
    ubufs = (ub0, ub1)

    def idx_copy(step, base):
        return pltpu.make_async_copy(idx_hbm.at[pl.ds((step0 + step) * N, N)],
                                     idx_smem.at[pl.ds(base, N)], sem_i)

    def issue_token(base, t, slot):
        for k in range(HK):
            e = idx_smem[base + t * HK + k]
            rt, s = t * tiles + k // _SUBLANES, k % _SUBLANES
            pltpu.make_async_copy(u_hbm.at[e], ubufs[slot].at[rt, :, s, :], sem_u.at[slot]).start()

    def wait(slot):
        pltpu.make_async_copy(ubufs[slot], ubufs[slot], sem_u.at[slot]).wait()

    @pl.when(i == 0)
    def _():
        first = idx_copy(0, 0)
        first.start()
        first.wait()
        for t in range(SUB):
            issue_token(0, t, 0)

    @pl.when(more)
    def _():
        idx_copy(i + 1, nxt).start()

    h_scr[...] = _rms(x1_ref[...], gn_ref[...]) * (1.0 + sc_ref[0]) + sh_ref[0]
    lane = lax.broadcasted_iota(jnp.int32, (HK, TB), 1)

    def compute_token(j, t, slot, wacc):
        ub = ubufs[slot]
        tok = j * SUB + t
        xt = h_scr[pl.ds(tok, 1), :]
        rows = slice(t * tiles, (t + 1) * tiles)
        part = ub[rows, 0].reshape(HK, LANE) * xt[:, 0:LANE]
        for c in range(1, nchunk):
            part = part + ub[rows, c].reshape(HK, LANE) * xt[:, c * LANE:(c + 1) * LANE]
        a = jnp.sum(part, axis=1, keepdims=True)
        hit = lane == tok
        g = jnp.sum(jnp.where(hit, gate_ref[...], 0.0), axis=1, keepdims=True)
        wgt = g * (0.5 * a * (1.0 + lax.erf(a * (2.0 ** -0.5))))
        return jnp.where(hit, wgt, wacc)

    def half(j, slot, next_base, wacc):
        wait(slot)
        for t in range(SUB):
            issue_token(next_base, t, 1 - slot)
            wacc = compute_token(j, t, slot, wacc)
        return wacc

    def pair(jj, wacc):
        j0 = 2 * jj
        wacc = half(j0, 0, cur + (j0 + 1) * R, wacc)
        last = jj == nsub // 2 - 1

        @pl.when(jnp.logical_and(last, more))
        def _():
            idx_copy(i + 1, nxt).wait()

        after = jnp.where(more, nxt, cur)
        return half(j0 + 1, 1, jnp.where(last, after, cur + (j0 + 2) * R), wacc)

    wacc = lax.fori_loop(0, nsub // 2, pair, jnp.zeros((HK, TB), F32))

    @pl.when(jnp.logical_not(more))
    def _():
        wait(0)

    w_ref[:, :HK] = jnp.zeros((TB, HK), F32)
    w_ref[:, HK:] = wacc.T


def _peer_u(idx_flat, gate_t, x1, g_norm, mod3, u3, S, step0, nsteps):
    T, D = x1.shape
    HK = PEER_HEADS * PEER_TOPK
    TB, SUB = _PEER_TB, _PEER_SUB
    per_b = S // TB
    modrow = lambda k: (lambda i: (((step0 + i) // per_b) * 6 + k, 0, 0))
    gbuf = pltpu.VMEM((SUB * HK // _SUBLANES, D // LANE, _SUBLANES, LANE), F32)
    return pl.pallas_call(
        functools.partial(_peer_u_kernel, step0=step0),
        grid=(nsteps,),
        in_specs=[pl.BlockSpec(memory_space=pl.ANY),
                  pl.BlockSpec((HK, TB), lambda i: (0, step0 + i)),
                  pl.BlockSpec((TB, D), lambda i: (step0 + i, 0)),
                  pl.BlockSpec((1, D), lambda i: (0, 0)),
                  pl.BlockSpec((1, 1, D), modrow(4)),
                  pl.BlockSpec((1, 1, D), modrow(3)),
                  pl.BlockSpec(memory_space=pl.ANY)],
        out_specs=pl.BlockSpec((TB, 2 * HK), lambda i: (i, 0)),
        out_shape=jax.ShapeDtypeStruct((nsteps * TB, 2 * HK), F32),
        scratch_shapes=[pltpu.SMEM((2 * TB * HK,), jnp.int32),
                        gbuf, gbuf,
                        pltpu.VMEM((TB, D), F32),
                        pltpu.SemaphoreType.DMA,
                        pltpu.SemaphoreType.DMA((2,))],
        compiler_params=_params(("arbitrary",), 40),
        name="peer_u",
    )(idx_flat, gate_t, x1, g_norm, mod3, mod3, u3)


def _sc_peer_v(v_tab, idx_flat, wgt, tok_base):
    E, nblk, _ = v_tab.shape
    D = nblk * LANE
    Tc, HK = wgt.shape[0], wgt.shape[1] // 2
    info = plsc.get_sparse_core_info()
    nw = info.num_cores * info.num_subcores
    tpw = Tc // nw
    CH = _SC_ROWS
    nch = HK // CH
    nsl = D // _SC_LANES
    mesh = plsc.VectorSubcoreMesh(core_axis_name="c", subcore_axis_name="s")

    @functools.partial(
        pl.kernel, mesh=mesh, out_type=jax.ShapeDtypeStruct((Tc, D), F32),
        scratch_types=[pltpu.VMEM((tpw * HK,), jnp.int32), pltpu.VMEM((2 * HK,), F32),
                       pltpu.VMEM((D,), F32),
                       pltpu.VMEM((CH, nblk, LANE), F32), pltpu.VMEM((CH, nblk, LANE), F32),
                       pltpu.SemaphoreType.DMA, pltpu.SemaphoreType.DMA],
        compiler_params=pltpu.CompilerParams(needs_layout_passes=False),
        name="sc_peer_v",
    )
    def k(tab_hbm, idx_hbm, w_hbm, o_hbm, idx_v, w_v, o_v, buf0, buf1, g0, g1):
        wid = lax.axis_index("s") * info.num_cores + lax.axis_index("c")
        tok0 = wid * tpw
        pltpu.sync_copy(idx_hbm.at[pl.ds((tok_base + tok0) * HK, tpw * HK)], idx_v)
        bufs, gs = (buf0, buf1), (g0, g1)

        def gather(g, b):
            return pltpu.make_async_copy(tab_hbm.at[idx_v.at[pl.ds(g * CH, CH)]], bufs[b], gs[b])

        gather(0, 0).start()

        @pl.loop(0, tpw)
        def _(t):
            pltpu.sync_copy(w_hbm.at[tok0 + t], w_v)

            @pl.loop(0, nsl)
            def _(c):
                o_v[pl.ds(pl.multiple_of(c * _SC_LANES, _SC_LANES), _SC_LANES)] = jnp.zeros((_SC_LANES,), F32)

            for ch in range(nch):
                b = ch % 2
                g = t * nch + ch
                gather(g, b).wait()

                @pl.when(g + 1 < tpw * nch)
                def _():
                    gather(g + 1, 1 - b).start()

                ws = [plsc.load_gather(w_v, [jnp.full((_SC_LANES,), HK + ch * CH + r, jnp.int32)])
                      for r in range(CH)]

                @pl.loop(0, nsl, step=2)
                def _(c):
                    for half in range(2):
                        off = pl.multiple_of((c + half) * _SC_LANES, _SC_LANES)
                        blk = (c + half) // (LANE // _SC_LANES)
                        lo = pl.multiple_of(off - blk * LANE, _SC_LANES)
                        parts = [ws[r] * bufs[b][r, blk, pl.ds(lo, _SC_LANES)] for r in range(CH)]
                        while len(parts) > 1:
                            parts = [parts[p] + parts[p + 1] for p in range(0, len(parts), 2)]
                        o_v[pl.ds(off, _SC_LANES)] = o_v[pl.ds(off, _SC_LANES)] + parts[0]

            pltpu.sync_copy(o_v, o_hbm.at[tok0 + t])

    return k(v_tab, idx_flat, wgt)


def _peer_fin_kernel(x_ref, p_ref, ga_ref, gf_ref, o_ref):
    o_ref[...] = _rms(x_ref[...] + ga_ref[0] * p_ref[...], gf_ref[...])


def _peer_fin(x1, po, mod3, g_final, S):
    T, D = x1.shape
    tm = 512
    per_b = S // tm
    return pl.pallas_call(
        _peer_fin_kernel,
        grid=(T // tm,),
        in_specs=[pl.BlockSpec((tm, D), lambda i: (i, 0)),
                  pl.BlockSpec((tm, D), lambda i: (i, 0)),
                  pl.BlockSpec((1, 1, D), lambda i: ((i // per_b) * 6 + 5, 0, 0)),
                  pl.BlockSpec((1, D), lambda i: (0, 0))],
        out_specs=pl.BlockSpec((tm, D), lambda i: (i, 0)),
        out_shape=jax.ShapeDtypeStruct((T, D), F32),
        compiler_params=_params(("parallel",), 40),
        name="peer_fin",
    )(x1, po, mod3, g_final)


def _pad_gate(w, lo):
    rank = w.shape[0]
    wh = w.reshape(rank, GLA_HEADS, GLA_DK).transpose(1, 0, 2)
    return jnp.zeros((GLA_HEADS, LANE, GLA_DK), F32).at[:, lo:lo + rank, :].set(wh)


def kernel(x, c, positions, w_ada, b_ada, g_norm_mix, w_in, w_gate_f, b_gate_f, w_gate_b, b_gate_b,
           g_gla_out, w_out, g_norm_ffn, w_peer_q, peer_sub_keys, peer_u, peer_v, g_final):
    B, S, D = x.shape
    T = B * S
    depth = w_ada.shape[0]
    assert depth == 1, "the final norm is fused into the last PEER call; one layer only"
    xt = x.reshape(T, D)
    cs, sn = _rope_tables(positions.reshape(T, 1))
    gz0 = 2 * GLA_HEADS * GLA_DK + 2 * GLA_HEADS * GLA_DV
    gz1 = gz0 + 2 * GLA_GATE_RANK
    for l in range(depth):
        mod3 = _ada(c, w_ada[l], b_ada[l]).reshape(B * 6, 1, D)
        w_main = jnp.concatenate([w_in[l][:, :gz0], w_in[l][:, gz1:]], axis=1).astype(BF16)
        w_z = jnp.pad(w_in[l][:, gz0:gz1], ((0, 0), (0, LANE - (gz1 - gz0)))).astype(BF16)
        proj, gz = _inproj(xt, g_norm_mix[l].reshape(1, D), mod3, w_main, w_z, S)
        o_gla = _gla(proj, gz,
                     _pad_gate(w_gate_f[l], 0), b_gate_f[l].reshape(GLA_HEADS, 1, GLA_DK),
                     _pad_gate(w_gate_b[l], GLA_GATE_RANK), b_gate_b[l].reshape(GLA_HEADS, 1, GLA_DK),
                     g_gla_out[l].reshape(GLA_HEADS, 1, GLA_DV), B, S)
        o_dil = _dil(proj, cs, sn, B, S)
        x1 = _outproj(o_gla, o_dil, w_out[l].astype(BF16), xt, mod3, S)
        qp = _pq(x1, g_norm_ffn[l].reshape(1, D), mod3, w_peer_q[l].astype(BF16), S)
        idx_t, gate_t = _topk(qp, peer_sub_keys[l].astype(BF16))
        idx_flat = idx_t.T.reshape(-1)
        E = peer_u.shape[1]
        u3 = peer_u[l].reshape(E, D // LANE, LANE)
        v3 = peer_v[l].reshape(E, D // LANE, LANE)
        steps = T // _PEER_TB // _PEER_CHUNKS
        outs = []
        for ck in range(_PEER_CHUNKS):
            wgt = _peer_u(idx_flat, gate_t, x1, g_norm_ffn[l].reshape(1, D), mod3, u3, S, ck * steps, steps)
            outs.append(_sc_peer_v(v3, idx_flat, wgt, ck * steps * _PEER_TB))
        xt = _peer_fin(x1, jnp.concatenate(outs, axis=0), mod3, g_final.reshape(1, D), S)
    return xt.reshape(B, S, D)
```

```python
import functools
import math

import numpy as np
import jax
import jax.numpy as jnp
from jax import lax
from jax.experimental import pallas as pl
from jax.experimental.pallas import tpu as pltpu
from jax.experimental.pallas import tpu_sc as plsc

F32 = jnp.float32
BF16 = jnp.bfloat16
HIGHEST = lax.Precision.HIGHEST

NORM_EPS = 1e-6
GLA_HEADS = 4
GLA_DK = 128
GLA_DV = 256
GLA_GATE_RANK = 16
GLA_TAU = 16.0
GLA_CHUNK = 64
DIL_HD = 128
DIL_HEADS = 8
DIL_PATTERNS = ((128, 1), (512, 4), (2048, 16))
ROPE_THETA = 10000.0
NEG_INF = -1e30
PEER_HEADS = 8
PEER_NKEYS = 128
PEER_TOPK = 16
PEER_HALF = 128

LANE = 128
MIB = 1024 * 1024

_COL_GQ, _COL_GK, _COL_GV, _COL_GR, _COL_DQ, _COL_DK, _COL_DV = 0, 4, 8, 16, 24, 32, 40
_PROJ_W = 48 * LANE

_NT = (((1,), (1,)), ((), ()))
_TN = (((0,), (0,)), ((), ()))


def _params(sem, vmem_mib):
    return pltpu.CompilerParams(dimension_semantics=sem, vmem_limit_bytes=vmem_mib * MIB)


def _rms(x, g):
    return x * lax.rsqrt(jnp.mean(x * x, axis=-1, keepdims=True) + NORM_EPS) * g


def _silu(x):
    return x / (1.0 + jnp.exp(-x))


def _ada_kernel(c_ref, w_ref, b_ref, o_ref):
    s = _silu(c_ref[...]).astype(BF16)
    o_ref[...] = jnp.dot(s, w_ref[...].astype(BF16), preferred_element_type=F32) + b_ref[...]


def _ada(c, w, b):
    B, D = c.shape
    N = w.shape[1]
    tn = 1024
    cp = jnp.zeros((8, D), F32).at[:B].set(c)
    out = pl.pallas_call(
        _ada_kernel,
        grid=(N // tn,),
        in_specs=[pl.BlockSpec((8, D), lambda j: (0, 0)),
                  pl.BlockSpec((D, tn), lambda j: (0, j)),
                  pl.BlockSpec((1, tn), lambda j: (0, j))],
        out_specs=pl.BlockSpec((8, tn), lambda j: (0, j)),
        out_shape=jax.ShapeDtypeStruct((8, N), F32),
        compiler_params=_params(("parallel",), 40),
        name="ada",
    )(cp, w, b.reshape(1, N))
    return out[:B]


def _inproj_kernel(x_ref, g_ref, sc_ref, sh_ref, w_ref, wz_ref, o_ref, z_ref, h_scr):
    @pl.when(pl.program_id(1) == 0)
    def _():
        h = _rms(x_ref[...], g_ref[...]) * (1.0 + sc_ref[0]) + sh_ref[0]
        hb = h.astype(BF16)
        h_scr[...] = hb
        z_ref[...] = jnp.dot(hb, wz_ref[...], preferred_element_type=F32)

    o_ref[...] = jnp.dot(h_scr[...], w_ref[...], preferred_element_type=F32)


def _inproj(x2, g, mod3, w_main, w_z, S):
    T, D = x2.shape
    tm, tn = 1024, 768
    per_b = S // tm
    return pl.pallas_call(
        _inproj_kernel,
        grid=(T // tm, _PROJ_W // tn),
        in_specs=[pl.BlockSpec((tm, D), lambda i, j: (i, 0)),
                  pl.BlockSpec((1, D), lambda i, j: (0, 0)),
                  pl.BlockSpec((1, 1, D), lambda i, j: ((i // per_b) * 6 + 1, 0, 0)),
                  pl.BlockSpec((1, 1, D), lambda i, j: ((i // per_b) * 6 + 0, 0, 0)),
                  pl.BlockSpec((D, tn), lambda i, j: (0, j)),
                  pl.BlockSpec((D, LANE), lambda i, j: (0, 0))],
        out_specs=[pl.BlockSpec((tm, tn), lambda i, j: (i, j)),
                   pl.BlockSpec((tm, LANE), lambda i, j: (i, 0))],
        out_shape=[jax.ShapeDtypeStruct((T, _PROJ_W), F32),
                   jax.ShapeDtypeStruct((T, LANE), F32)],
        scratch_shapes=[pltpu.VMEM((tm, D), BF16)],
        compiler_params=_params(("parallel", "arbitrary"), 48),
        name="inproj",
    )(x2, g, mod3, mod3, w_main, w_z)


def _rope_kernel(pos_ref, f_ref, sg_ref, cs_ref, sn_ref):
    ang = pos_ref[...].astype(F32) * f_ref[...]
    cs_ref[...] = jnp.cos(ang)
    sn_ref[...] = jnp.sin(ang) * sg_ref[...]


def _rope_tables(pos_col):
    T = pos_col.shape[0]
    half = DIL_HD // 2
    inv = jnp.power(ROPE_THETA, -jnp.arange(half, dtype=F32) * 2.0 / DIL_HD)
    freq = jnp.concatenate([inv, inv]).reshape(1, DIL_HD)
    sign = jnp.concatenate([-jnp.ones((half,), F32), jnp.ones((half,), F32)]).reshape(1, DIL_HD)
    tm = 1024
    return pl.pallas_call(
        _rope_kernel,
        grid=(T // tm,),
        in_specs=[pl.BlockSpec((tm, 1), lambda i: (i, 0)),
                  pl.BlockSpec((1, DIL_HD), lambda i: (0, 0)),
                  pl.BlockSpec((1, DIL_HD), lambda i: (0, 0))],
        out_specs=[pl.BlockSpec((tm, DIL_HD), lambda i: (i, 0)),
                   pl.BlockSpec((tm, DIL_HD), lambda i: (i, 0))],
        out_shape=[jax.ShapeDtypeStruct((T, DIL_HD), F32)] * 2,
        compiler_params=_params(("parallel",), 32),
        name="rope",
    )(pos_col, freq, sign)


def _gla_kernel(q_ref, k_ref, v_ref, r_ref, z_ref, wgf_ref, bgf_ref, wgb_ref, bgb_ref, g_ref,
                o_ref, laf_scr, lab_scr, of_scr):
    S = q_ref.shape[0]
    C = GLA_CHUNK
    n = S // C
    scale = GLA_DK ** -0.5
    row = lax.broadcasted_iota(jnp.int32, (C, C), 0)
    col = lax.broadcasted_iota(jnp.int32, (C, C), 1)

    def log_gate(w_ref, b_ref):
        zz = jnp.dot(z_ref[...], w_ref[0], precision=HIGHEST, preferred_element_type=F32) + b_ref[0]
        return (jnp.minimum(zz, 0.0) - jnp.log(1.0 + jnp.exp(-jnp.abs(zz)))) * (1.0 / GLA_TAU)

    def chunk(fwd, c, st_t, la_scr, o_scr):
        keep = (col <= row) if fwd else (col >= row)
        sl = pl.ds(pl.multiple_of(c * C, C), C)
        cum = jnp.dot(keep.astype(F32), la_scr[sl, :], precision=HIGHEST, preferred_element_type=F32)
        tot = cum[C - 1:C, :] if fwd else cum[0:1, :]
        kk = k_ref[sl, :]
        qd = (q_ref[sl, :] * scale * jnp.exp(cum)).astype(BF16)
        ki = (kk * jnp.exp(-cum)).astype(BF16)
        kte = (kk * jnp.exp(tot - cum)).astype(BF16)
        vb = v_ref[sl, :].astype(BF16)
        attn = lax.dot_general(qd, ki, _NT, preferred_element_type=F32)
        attn = jnp.where(keep, attn, 0.0).astype(BF16)
        o = jnp.dot(attn, vb, preferred_element_type=F32)
        o_scr[sl, :] = o + lax.dot_general(qd, st_t.astype(BF16), _NT, preferred_element_type=F32)
        upd = lax.dot_general(vb, kte, _TN, preferred_element_type=F32)
        return st_t * jnp.exp(tot) + upd

    laf_scr[...] = log_gate(wgf_ref, bgf_ref)
    lab_scr[...] = log_gate(wgb_ref, bgb_ref)

    def body(i, states):
        return (chunk(True, i, states[0], laf_scr, of_scr),
                chunk(False, n - 1 - i, states[1], lab_scr, o_ref))

    zero = jnp.zeros((GLA_DV, GLA_DK), F32)
    lax.fori_loop(0, n, body, (zero, zero))
    y = _rms(of_scr[...] + o_ref[...], g_ref[0])
    o_ref[...] = y * _silu(r_ref[...])


def _gla(proj, gz, wgf, bgf, wgb, bgb, g_out, B, S):
    T = proj.shape[0]
    H = GLA_HEADS
    return pl.pallas_call(
        _gla_kernel,
        grid=(B, H),
        in_specs=[pl.BlockSpec((S, GLA_DK), lambda b, h: (b, _COL_GQ + h)),
                  pl.BlockSpec((S, GLA_DK), lambda b, h: (b, _COL_GK + h)),
                  pl.BlockSpec((S, GLA_DV), lambda b, h: (b, _COL_GV // 2 + h)),
                  pl.BlockSpec((S, GLA_DV), lambda b, h: (b, _COL_GR // 2 + h)),
                  pl.BlockSpec((S, LANE), lambda b, h: (b, 0)),
                  pl.BlockSpec((1, LANE, GLA_DK), lambda b, h: (h, 0, 0)),
                  pl.BlockSpec((1, 1, GLA_DK), lambda b, h: (h, 0, 0)),
                  pl.BlockSpec((1, LANE, GLA_DK), lambda b, h: (h, 0, 0)),
                  pl.BlockSpec((1, 1, GLA_DK), lambda b, h: (h, 0, 0)),
                  pl.BlockSpec((1, 1, GLA_DV), lambda b, h: (h, 0, 0))],
        out_specs=pl.BlockSpec((S, GLA_DV), lambda b, h: (b, h)),
        out_shape=jax.ShapeDtypeStruct((T, GLA_HEADS * GLA_DV), F32),
        scratch_shapes=[pltpu.VMEM((S, GLA_DK), F32), pltpu.VMEM((S, GLA_DK), F32),
                        pltpu.VMEM((S, GLA_DV), F32)],
        compiler_params=_params(("parallel", "parallel"), 56),
        name="gla",
    )(proj, proj, proj, proj, gz, wgf, bgf, wgb, bgb, g_out)


_DIL_QB = 256
_DIL_REACH = max(w // 2 for w, _ in DIL_PATTERNS) // _DIL_QB


def _dil_bias():
    qb = _DIL_QB
    d = np.arange(-_DIL_REACH, _DIL_REACH + 1)[:, None, None] * qb
    delta = d + np.arange(qb)[None, None, :] - np.arange(qb)[None, :, None]
    mult = np.zeros(delta.shape, np.float64)
    for window, dilation in DIL_PATTERNS:
        half = window // (2 * dilation)
        mult += ((delta % dilation) == 0) & (np.abs(delta) <= half * dilation)
    with np.errstate(divide="ignore"):
        bias = np.where(mult > 0, np.log(np.maximum(mult, 1.0)), NEG_INF)
    return jnp.asarray(bias, F32)


def _dil_kernel(q_ref, k_ref, v_ref, cs_ref, sn_ref, bias_ref, o_ref, kr_scr, vb_scr):
    qi = pl.program_id(2)
    nq = pl.num_programs(2)
    QB = _DIL_QB
    half = DIL_HD // 2

    @pl.when(qi == 0)
    def _():
        k = k_ref[...]
        kr_scr[...] = (k * cs_ref[...] + pltpu.roll(k, half, 1) * sn_ref[...]).astype(BF16)
        vb_scr[...] = v_ref[...].astype(BF16)

    r0 = pl.multiple_of(qi * QB, QB)
    q = q_ref[...]
    qr = q * cs_ref[pl.ds(r0, QB), :] + pltpu.roll(q, half, 1) * sn_ref[pl.ds(r0, QB), :]
    qb = (qr * (DIL_HD ** -0.5)).astype(BF16)

    m = jnp.full((QB, 1), NEG_INF, F32)
    l = jnp.zeros((QB, 1), F32)
    acc = jnp.zeros((QB, DIL_HD), F32)
    order = [0] + [s * d for d in range(1, _DIL_REACH + 1) for s in (-1, 1)]
    for d in order:
        kb = qi + d
        valid = jnp.logical_and(kb >= 0, kb < nq)
        k0 = pl.multiple_of(jnp.clip(kb, 0, nq - 1) * QB, QB)
        s = lax.dot_general(qb, kr_scr[pl.ds(k0, QB), :], _NT, preferred_element_type=F32)
        s = jnp.where(valid, s + bias_ref[d + _DIL_REACH], NEG_INF)
        m_new = jnp.maximum(m, jnp.max(s, axis=-1, keepdims=True))
        alpha = jnp.exp(m - m_new)
        p = jnp.exp(s - m_new)
        l = alpha * l + jnp.sum(p, axis=-1, keepdims=True)
        acc = alpha * acc + jnp.dot(p.astype(BF16), vb_scr[pl.ds(k0, QB), :],
                                    preferred_element_type=F32)
        m = m_new
    o_ref[...] = acc / l


def _dil(proj, cs, sn, B, S):
    T = proj.shape[0]
    QB = _DIL_QB
    nq = S // QB
    nb = 2 * _DIL_REACH + 1
    return pl.pallas_call(
        _dil_kernel,
        grid=(B, DIL_HEADS, nq),
        in_specs=[pl.BlockSpec((QB, DIL_HD), lambda b, h, i: (b * nq + i, _COL_DQ + h)),
                  pl.BlockSpec((S, DIL_HD), lambda b, h, i: (b, _COL_DK + h)),
                  pl.BlockSpec((S, DIL_HD), lambda b, h, i: (b, _COL_DV + h)),
                  pl.BlockSpec((S, DIL_HD), lambda b, h, i: (b, 0)),
                  pl.BlockSpec((S, DIL_HD), lambda b, h, i: (b, 0)),
                  pl.BlockSpec((nb, QB, QB), lambda b, h, i: (0, 0, 0))],
        out_specs=pl.BlockSpec((QB, DIL_HD), lambda b, h, i: (b * nq + i, h)),
        out_shape=jax.ShapeDtypeStruct((T, DIL_HEADS * DIL_HD), F32),
        scratch_shapes=[pltpu.VMEM((S, DIL_HD), BF16), pltpu.VMEM((S, DIL_HD), BF16)],
        compiler_params=_params(("parallel", "parallel", "arbitrary"), 48),
        name="dil",
    )(proj, proj, proj, cs, sn, _dil_bias())


def _outproj_kernel(og_ref, od_ref, w_ref, x_ref, ga_ref, o_ref):
    kg = og_ref.shape[1]
    mixed = jnp.dot(og_ref[...].astype(BF16), w_ref[:kg, :], preferred_element_type=F32)
    mixed = mixed + jnp.dot(od_ref[...].astype(BF16), w_ref[kg:, :], preferred_element_type=F32)
    o_ref[...] = x_ref[...] + ga_ref[0] * mixed


def _outproj(o_gla, o_dil, w_out, x2, mod3, S):
    T, D = x2.shape
    tm = 512
    per_b = S // tm
    kg, kd = o_gla.shape[1], o_dil.shape[1]
    return pl.pallas_call(
        _outproj_kernel,
        grid=(T // tm,),
        in_specs=[pl.BlockSpec((tm, kg), lambda i: (i, 0)),
                  pl.BlockSpec((tm, kd), lambda i: (i, 0)),
                  pl.BlockSpec((kg + kd, D), lambda i: (0, 0)),
                  pl.BlockSpec((tm, D), lambda i: (i, 0)),
                  pl.BlockSpec((1, 1, D), lambda i: ((i // per_b) * 6 + 2, 0, 0))],
        out_specs=pl.BlockSpec((tm, D), lambda i: (i, 0)),
        out_shape=jax.ShapeDtypeStruct((T, D), F32),
        compiler_params=_params(("parallel",), 48),
        name="outproj",
    )(o_gla, o_dil, w_out, x2, mod3)


def _pq_kernel(x_ref, g_ref, sc_ref, sh_ref, w_ref, o_ref):
    h = _rms(x_ref[...], g_ref[...]) * (1.0 + sc_ref[0]) + sh_ref[0]
    o_ref[...] = jnp.dot(h.astype(BF16), w_ref[...], preferred_element_type=F32)


def _pq(x1, g, mod3, wq, S):
    T, D = x1.shape
    N = wq.shape[1]
    tm = 512
    per_b = S // tm
    return pl.pallas_call(
        _pq_kernel,
        grid=(T // tm,),
        in_specs=[pl.BlockSpec((tm, D), lambda i: (i, 0)),
                  pl.BlockSpec((1, D), lambda i: (0, 0)),
                  pl.BlockSpec((1, 1, D), lambda i: ((i // per_b) * 6 + 4, 0, 0)),
                  pl.BlockSpec((1, 1, D), lambda i: ((i // per_b) * 6 + 3, 0, 0)),
                  pl.BlockSpec((D, N), lambda i: (0, 0))],
        out_specs=pl.BlockSpec((tm, N), lambda i: (i, 0)),
        out_shape=jax.ShapeDtypeStruct((T, N), F32),
        compiler_params=_params(("parallel",), 48),
        name="pq",
    )(x1, g, mod3, mod3, wq)


def _top_rows(s, k, payload=None):
    n_rows = s.shape[0]
    rid = lax.broadcasted_iota(jnp.int32, s.shape, 0)
    vals, picks = [], []
    for _ in range(k):
        m = jnp.max(s, axis=0, keepdims=True)
        pos = jnp.min(jnp.where(s == m, rid, n_rows), axis=0, keepdims=True)
        hit = rid == pos
        vals.append(m)
        if payload is None:
            picks.append(pos)
        else:
            picks.append(jnp.sum(jnp.where(hit, payload, 0), axis=0, keepdims=True))
        s = jnp.where(hit, -jnp.inf, s)
    return jnp.concatenate(vals, axis=0), jnp.concatenate(picks, axis=0)


def _staircase(a, b, combine, fill):
    K = a.shape[0]
    half = K // 2
    jrow = lax.broadcasted_iota(jnp.int32, (half, a.shape[1]), 0)
    pieces = [combine(a[0:1], b)]
    for i in range(1, half):
        piece = combine(a[i:i + 1], b[0:half])
        width = K // (i + 1)
        pieces.append(piece if width >= half else jnp.where(jrow < width, piece, fill))
    pieces.append(combine(a[half:K], b[0:1]))
    return jnp.concatenate(pieces, axis=0)


def _topk_kernel(q_ref, keys_ref, idx_ref, gate_ref):
    K = PEER_TOPK
    for h in range(PEER_HEADS):
        tops = []
        for half in range(2):
            c0 = (h * 2 + half) * PEER_HALF
            qh = q_ref[:, c0:c0 + PEER_HALF].astype(BF16)
            sc = lax.dot_general(keys_ref[h, half], qh, _NT, preferred_element_type=F32)
            tops.append(_top_rows(sc, K))
        (s0, i0), (s1, i1) = tops
        cand_s = _staircase(s0, s1, lambda a, b: a + b, -jnp.inf)
        cand_i = _staircase(i0, i1, lambda a, b: a * PEER_NKEYS + b, 0)
        best, idx = _top_rows(cand_s, K, payload=cand_i)
        e = jnp.exp(best - best[0:1])
        gate = e / jnp.sum(e, axis=0, keepdims=True)
        idx_ref[h * K:(h + 1) * K, :] = idx
        gate_ref[h * K:(h + 1) * K, :] = gate


def _topk(qp, keys_bf):
    T, N = qp.shape
    tt = 256
    HK = PEER_HEADS * PEER_TOPK
    return pl.pallas_call(
        _topk_kernel,
        grid=(T // tt,),
        in_specs=[pl.BlockSpec((tt, N), lambda i: (i, 0)),
                  pl.BlockSpec(keys_bf.shape, lambda i: (0, 0, 0, 0))],
        out_specs=[pl.BlockSpec((HK, tt), lambda i: (0, i)),
                   pl.BlockSpec((HK, tt), lambda i: (0, i))],
        out_shape=[jax.ShapeDtypeStruct((HK, T), jnp.int32),
                   jax.ShapeDtypeStruct((HK, T), F32)],
        compiler_params=_params(("parallel",), 32),
        name="topk",
    )(qp, keys_bf)


_PEER_TB = 128
_PEER_SUB = 8
_SUBLANES = 8
_PEER_CHUNKS = 16
_SC_LANES = 16
_SC_ROWS = 16


def _peer_u_kernel(idx_hbm, gate_ref, x1_ref, gn_ref, sc_ref, sh_ref, u_hbm, w_ref,
                   idx_smem, ub0, ub1, h_scr, sem_i, sem_u, *, step0):
    HK = PEER_HEADS * PEER_TOPK
    TB, SUB = _PEER_TB, _PEER_SUB
    R = SUB * HK
    N = TB * HK
    nsub = TB // SUB
    D = x1_ref.shape[1]
    nchunk = D // LANE
    tiles = HK // _SUBLANES
    i = pl.program_id(0)
    n = pl.num_programs(0)
    cur = lax.rem(i, 2) * N
    nxt = N - cur
    more = i + 1 < n
    ubufs = (ub0, ub1)

    def idx_copy(step, base):
        return pltpu.make_async_copy(idx_hbm.at[pl.ds((step0 + step) * N, N)],
                                     idx_smem.at[pl.ds(base, N)], sem_i)

    def issue_token(base, t, slot):
        for k in range(HK):
            e = idx_smem[base + t * HK + k]
            rt, s = t * tiles + k // _SUBLANES, k % _SUBLANES
            pltpu.make_async_copy(u_hbm.at[e], ubufs[slot].at[rt, :, s, :], sem_u.at[slot]).start()

    def wait(slot):
        pltpu.make_async_copy(ubufs[slot], ubufs[slot], sem_u.at[slot]).wait()

    @pl.when(i == 0)
    def _():
        first = idx_copy(0, 0)
        first.start()
        first.wait()
        for t in range(SUB):
            issue_token(0, t, 0)

    @pl.when(more)
    def _():
        idx_copy(i + 1, nxt).start()

    h_scr[...] = _rms(x1_ref[...], gn_ref[...]) * (1.0 + sc_ref[0]) + sh_ref[0]
    lane = lax.broadcasted_iota(jnp.int32, (HK, TB), 1)

    def compute_token(j, t, slot, wacc):
        ub = ubufs[slot]
        tok = j * SUB + t
        xt = h_scr[pl.ds(tok, 1), :]
        rows = slice(t * tiles, (t + 1) * tiles)
        part = ub[rows, 0].reshape(HK, LANE) * xt[:, 0:LANE]
        for c in range(1, nchunk):
            part = part + ub[rows, c].reshape(HK, LANE) * xt[:, c * LANE:(c + 1) * LANE]
        a = jnp.sum(part, axis=1, keepdims=True)
        hit = lane == tok
        g = jnp.sum(jnp.where(hit, gate_ref[...], 0.0), axis=1, keepdims=True)
        wgt = g * (0.5 * a * (1.0 + lax.erf(a * (2.0 ** -0.5))))
        return jnp.where(hit, wgt, wacc)

    def half(j, slot, next_base, wacc):
        wait(slot)
        for t in range(SUB):
            issue_token(next_base, t, 1 - slot)
            wacc = compute_token(j, t, slot, wacc)
        return wacc

    def pair(jj, wacc):
        j0 = 2 * jj
        wacc = half(j0, 0, cur + (j0 + 1) * R, wacc)
        last = jj == nsub // 2 - 1

        @pl.when(jnp.logical_and(last, more))
        def _():
            idx_copy(i + 1, nxt).wait()

        after = jnp.where(more, nxt, cur)
        return half(j0 + 1, 1, jnp.where(last, after, cur + (j0 + 2) * R), wacc)

    wacc = lax.fori_loop(0, nsub // 2, pair, jnp.zeros((HK, TB), F32))

    @pl.when(jnp.logical_not(more))
    def _():
        wait(0)

    w_ref[:, :HK] = jnp.zeros((TB, HK), F32)
    w_ref[:, HK:] = wacc.T


def _peer_u(idx_flat, gate_t, x1, g_norm, mod3, u3, S, step0, nsteps):
    T, D = x1.shape
    HK = PEER_HEADS * PEER_TOPK
    TB, SUB = _PEER_TB, _PEER_SUB
    per_b = S // TB
    modrow = lambda k: (lambda i: (((step0 + i) // per_b) * 6 + k, 0, 0))
    gbuf = pltpu.VMEM((SUB * HK // _SUBLANES, D // LANE, _SUBLANES, LANE), F32)
    return pl.pallas_call(
        functools.partial(_peer_u_kernel, step0=step0),
        grid=(nsteps,),
        in_specs=[pl.BlockSpec(memory_space=pl.ANY),
                  pl.BlockSpec((HK, TB), lambda i: (0, step0 + i)),
                  pl.BlockSpec((TB, D), lambda i: (step0 + i, 0)),
                  pl.BlockSpec((1, D), lambda i: (0, 0)),
                  pl.BlockSpec((1, 1, D), modrow(4)),
                  pl.BlockSpec((1, 1, D), modrow(3)),
                  pl.BlockSpec(memory_space=pl.ANY)],
        out_specs=pl.BlockSpec((TB, 2 * HK), lambda i: (i, 0)),
        out_shape=jax.ShapeDtypeStruct((nsteps * TB, 2 * HK), F32),
        scratch_shapes=[pltpu.SMEM((2 * TB * HK,), jnp.int32),
                        gbuf, gbuf,
                        pltpu.VMEM((TB, D), F32),
                        pltpu.SemaphoreType.DMA,
                        pltpu.SemaphoreType.DMA((2,))],
        compiler_params=_params(("arbitrary",), 40),
        name="peer_u",
    )(idx_flat, gate_t, x1, g_norm, mod3, mod3, u3)


def _sc_peer_v(v_tab, idx_flat, wgt, tok_base):
    E, nblk, _ = v_tab.shape
    D = nblk * LANE
    Tc, HK = wgt.shape[0], wgt.shape[1] // 2
    info = plsc.get_sparse_core_info()
    nw = info.num_cores * info.num_subcores
    tpw = Tc // nw
    CH = _SC_ROWS
    nch = HK // CH
    nsl = D // _SC_LANES
    mesh = plsc.VectorSubcoreMesh(core_axis_name="c", subcore_axis_name="s")

    @functools.partial(
        pl.kernel, mesh=mesh, out_type=jax.ShapeDtypeStruct((Tc, D), F32),
        scratch_types=[pltpu.VMEM((tpw * HK,), jnp.int32), pltpu.VMEM((2 * HK,), F32),
                       pltpu.VMEM((D,), F32),
                       pltpu.VMEM((CH, nblk, LANE), F32), pltpu.VMEM((CH, nblk, LANE), F32),
                       pltpu.SemaphoreType.DMA, pltpu.SemaphoreType.DMA],
        compiler_params=pltpu.CompilerParams(needs_layout_passes=False),
        name="sc_peer_v",
    )
    def k(tab_hbm, idx_hbm, w_hbm, o_hbm, idx_v, w_v, o_v, buf0, buf1, g0, g1):
        wid = lax.axis_index("s") * info.num_cores + lax.axis_index("c")
        tok0 = wid * tpw
        pltpu.sync_copy(idx_hbm.at[pl.ds((tok_base + tok0) * HK, tpw * HK)], idx_v)
        bufs, gs = (buf0, buf1), (g0, g1)

        def gather(g, b):
            return pltpu.make_async_copy(tab_hbm.at[idx_v.at[pl.ds(g * CH, CH)]], bufs[b], gs[b])

        gather(0, 0).start()

        @pl.loop(0, tpw)
        def _(t):
            pltpu.sync_copy(w_hbm.at[tok0 + t], w_v)

            @pl.loop(0, nsl)
            def _(c):
                o_v[pl.ds(pl.multiple_of(c * _SC_LANES, _SC_LANES), _SC_LANES)] = jnp.zeros((_SC_LANES,), F32)

            for ch in range(nch):
                b = ch % 2
                g = t * nch + ch
                gather(g, b).wait()

                @pl.when(g + 1 < tpw * nch)
                def _():
                    gather(g + 1, 1 - b).start()

                ws = [plsc.load_gather(w_v, [jnp.full((_SC_LANES,), HK + ch * CH + r, jnp.int32)])
                      for r in range(CH)]

                @pl.loop(0, nsl, step=2)
                def _(c):
                    for half in range(2):
                        off = pl.multiple_of((c + half) * _SC_LANES, _SC_LANES)
                        blk = (c + half) // (LANE // _SC_LANES)
                        lo = pl.multiple_of(off - blk * LANE, _SC_LANES)
                        parts = [ws[r] * bufs[b][r, blk, pl.ds(lo, _SC_LANES)] for r in range(CH)]
                        while len(parts) > 1:
                            parts = [parts[p] + parts[p + 1] for p in range(0, len(parts), 2)]
                        o_v[pl.ds(off, _SC_LANES)] = o_v[pl.ds(off, _SC_LANES)] + parts[0]

            pltpu.sync_copy(o_v, o_hbm.at[tok0 + t])

    return k(v_tab, idx_flat, wgt)


def _peer_fin_kernel(x_ref, p_ref, ga_ref, gf_ref, o_ref):
    o_ref[...] = _rms(x_ref[...] + ga_ref[0] * p_ref[...], gf_ref[...])


def _peer_fin(x1, po, mod3, g_final, S):
    T, D = x1.shape
    tm = 512
    per_b = S // tm
    return pl.pallas_call(
        _peer_fin_kernel,
        grid=(T // tm,),
        in_specs=[pl.BlockSpec((tm, D), lambda i: (i, 0)),
                  pl.BlockSpec((tm, D), lambda i: (i, 0)),
                  pl.BlockSpec((1, 1, D), lambda i: ((i // per_b) * 6 + 5, 0, 0)),
                  pl.BlockSpec((1, D), lambda i: (0, 0))],
        out_specs=pl.BlockSpec((tm, D), lambda i: (i, 0)),
        out_shape=jax.ShapeDtypeStruct((T, D), F32),
        compiler_params=_params(("parallel",), 40),
        name="peer_fin",
    )(x1, po, mod3, g_final)


def _pad_gate(w, lo):
    rank = w.shape[0]
    wh = w.reshape(rank, GLA_HEADS, GLA_DK).transpose(1, 0, 2)
    return jnp.zeros((GLA_HEADS, LANE, GLA_DK), F32).at[:, lo:lo + rank, :].set(wh)


def kernel(x, c, positions, w_ada, b_ada, g_norm_mix, w_in, w_gate_f, b_gate_f, w_gate_b, b_gate_b,
           g_gla_out, w_out, g_norm_ffn, w_peer_q, peer_sub_keys, peer_u, peer_v, g_final):
    B, S, D = x.shape
    T = B * S
    depth = w_ada.shape[0]
    assert depth == 1, "the final norm is fused into the last PEER call; one layer only"
    xt = x.reshape(T, D)
    cs, sn = _rope_tables(positions.reshape(T, 1))
    gz0 = 2 * GLA_HEADS * GLA_DK + 2 * GLA_HEADS * GLA_DV
    gz1 = gz0 + 2 * GLA_GATE_RANK
    for l in range(depth):
        mod3 = _ada(c, w_ada[l], b_ada[l]).reshape(B * 6, 1, D)
        w_main = jnp.concatenate([w_in[l][:, :gz0], w_in[l][:, gz1:]], axis=1).astype(BF16)
        w_z = jnp.pad(w_in[l][:, gz0:gz1], ((0, 0), (0, LANE - (gz1 - gz0)))).astype(BF16)
        proj, gz = _inproj(xt, g_norm_mix[l].reshape(1, D), mod3, w_main, w_z, S)
        o_gla = _gla(proj, gz,
                     _pad_gate(w_gate_f[l], 0), b_gate_f[l].reshape(GLA_HEADS, 1, GLA_DK),
                     _pad_gate(w_gate_b[l], GLA_GATE_RANK), b_gate_b[l].reshape(GLA_HEADS, 1, GLA_DK),
                     g_gla_out[l].reshape(GLA_HEADS, 1, GLA_DV), B, S)
        o_dil = _dil(proj, cs, sn, B, S)
        x1 = _outproj(o_gla, o_dil, w_out[l].astype(BF16), xt, mod3, S)
        qp = _pq(x1, g_norm_ffn[l].reshape(1, D), mod3, w_peer_q[l].astype(BF16), S)
        idx_t, gate_t = _topk(qp, peer_sub_keys[l].astype(BF16))
        idx_flat = idx_t.T.reshape(-1)
        E = peer_u.shape[1]
        u3 = peer_u[l].reshape(E, D // LANE, LANE)
        v3 = peer_v[l].reshape(E, D // LANE, LANE)
        steps = T // _PEER_TB // _PEER_CHUNKS
        outs = []
        for ck in range(_PEER_CHUNKS):
            wgt = _peer_u(idx_flat, gate_t, x1, g_norm_ffn[l].reshape(1, D), mod3, u3, S, ck * steps, steps)
            outs.append(_sc_peer_v(v3, idx_flat, wgt, ck * steps * _PEER_TB))
        xt = _peer_fin(x1, jnp.concatenate(outs, axis=0), mod3, g_final.reshape(1, D), S)
    return xt.reshape(B, S, D)
```

```python
import functools
import math

import numpy as np
import jax
import jax.numpy as jnp
from jax import lax
from jax.experimental import pallas as pl
from jax.experimental.pallas import tpu as pltpu
from jax.experimental.pallas import tpu_sc as plsc

F32 = jnp.float32
BF16 = jnp.bfloat16
HIGHEST = lax.Precision.HIGHEST

NORM_EPS = 1e-6
GLA_HEADS = 4
GLA_DK = 128
GLA_DV = 256
GLA_GATE_RANK = 16
GLA_TAU = 16.0
GLA_CHUNK = 64
DIL_HD = 128
DIL_HEADS = 8
DIL_PATTERNS = ((128, 1), (512, 4), (2048, 16))
ROPE_THETA = 10000.0
NEG_INF = -1e30
PEER_HEADS = 8
PEER_NKEYS = 128
PEER_TOPK = 16
PEER_HALF = 128

LANE = 128
MIB = 1024 * 1024

_COL_GQ, _COL_GK, _COL_GV, _COL_GR, _COL_DQ, _COL_DK, _COL_DV = 0, 4, 8, 16, 24, 32, 40
_PROJ_W = 48 * LANE

_NT = (((1,), (1,)), ((), ()))
_TN = (((0,), (0,)), ((), ()))


def _params(sem, vmem_mib):
    return pltpu.CompilerParams(dimension_semantics=sem, vmem_limit_bytes=vmem_mib * MIB)


def _rms(x, g):
    return x * lax.rsqrt(jnp.mean(x * x, axis=-1, keepdims=True) + NORM_EPS) * g


def _silu(x):
    return x / (1.0 + jnp.exp(-x))


def _ada_kernel(c_ref, w_ref, b_ref, o_ref):
    s = _silu(c_ref[...]).astype(BF16)
    o_ref[...] = jnp.dot(s, w_ref[...].astype(BF16), preferred_element_type=F32) + b_ref[...]


def _ada(c, w, b):
    B, D = c.shape
    N = w.shape[1]
    tn = 1024
    cp = jnp.zeros((8, D), F32).at[:B].set(c)
    out = pl.pallas_call(
        _ada_kernel,
        grid=(N // tn,),
        in_specs=[pl.BlockSpec((8, D), lambda j: (0, 0)),
                  pl.BlockSpec((D, tn), lambda j: (0, j)),
                  pl.BlockSpec((1, tn), lambda j: (0, j))],
        out_specs=pl.BlockSpec((8, tn), lambda j: (0, j)),
        out_shape=jax.ShapeDtypeStruct((8, N), F32),
        compiler_params=_params(("parallel",), 40),
        name="ada",
    )(cp, w, b.reshape(1, N))
    return out[:B]


def _inproj_kernel(x_ref, g_ref, sc_ref, sh_ref, w_ref, wz_ref, o_ref, z_ref, h_scr):
    @pl.when(pl.program_id(1) == 0)
    def _():
        h = _rms(x_ref[...], g_ref[...]) * (1.0 + sc_ref[0]) + sh_ref[0]
        hb = h.astype(BF16)
        h_scr[...] = hb
        z_ref[...] = jnp.dot(hb, wz_ref[...], preferred_element_type=F32)

    o_ref[...] = jnp.dot(h_scr[...], w_ref[...], preferred_element_type=F32)


def _inproj(x2, g, mod3, w_main, w_z, S):
    T, D = x2.shape
    tm, tn = 1024, 768
    per_b = S // tm
    return pl.pallas_call(
        _inproj_kernel,
        grid=(T // tm, _PROJ_W // tn),
        in_specs=[pl.BlockSpec((tm, D), lambda i, j: (i, 0)),
                  pl.BlockSpec((1, D), lambda i, j: (0, 0)),
                  pl.BlockSpec((1, 1, D), lambda i, j: ((i // per_b) * 6 + 1, 0, 0)),
                  pl.BlockSpec((1, 1, D), lambda i, j: ((i // per_b) * 6 + 0, 0, 0)),
                  pl.BlockSpec((D, tn), lambda i, j: (0, j)),
                  pl.BlockSpec((D, LANE), lambda i, j: (0, 0))],
        out_specs=[pl.BlockSpec((tm, tn), lambda i, j: (i, j)),
                   pl.BlockSpec((tm, LANE), lambda i, j: (i, 0))],
        out_shape=[jax.ShapeDtypeStruct((T, _PROJ_W), F32),
                   jax.ShapeDtypeStruct((T, LANE), F32)],
        scratch_shapes=[pltpu.VMEM((tm, D), BF16)],
        compiler_params=_params(("parallel", "arbitrary"), 48),
        name="inproj",
    )(x2, g, mod3, mod3, w_main, w_z)


def _rope_kernel(pos_ref, f_ref, sg_ref, cs_ref, sn_ref):
    ang = pos_ref[...].astype(F32) * f_ref[...]
    cs_ref[...] = jnp.cos(ang)
    sn_ref[...] = jnp.sin(ang) * sg_ref[...]


def _rope_tables(pos_col):
    T = pos_col.shape[0]
    half = DIL_HD // 2
    inv = jnp.power(ROPE_THETA, -jnp.arange(half, dtype=F32) * 2.0 / DIL_HD)
    freq = jnp.concatenate([inv, inv]).reshape(1, DIL_HD)
    sign = jnp.concatenate([-jnp.ones((half,), F32), jnp.ones((half,), F32)]).reshape(1, DIL_HD)
    tm = 1024
    return pl.pallas_call(
        _rope_kernel,
        grid=(T // tm,),
        in_specs=[pl.BlockSpec((tm, 1), lambda i: (i, 0)),
                  pl.BlockSpec((1, DIL_HD), lambda i: (0, 0)),
                  pl.BlockSpec((1, DIL_HD), lambda i: (0, 0))],
        out_specs=[pl.BlockSpec((tm, DIL_HD), lambda i: (i, 0)),
                   pl.BlockSpec((tm, DIL_HD), lambda i: (i, 0))],
        out_shape=[jax.ShapeDtypeStruct((T, DIL_HD), F32)] * 2,
        compiler_params=_params(("parallel",), 32),
        name="rope",
    )(pos_col, freq, sign)


def _gla_kernel(q_ref, k_ref, v_ref, r_ref, z_ref, wgf_ref, bgf_ref, wgb_ref, bgb_ref, g_ref,
                o_ref, laf_scr, lab_scr, of_scr):
    S = q_ref.shape[0]
    C = GLA_CHUNK
    n = S // C
    scale = GLA_DK ** -0.5
    row = lax.broadcasted_iota(jnp.int32, (C, C), 0)
    col = lax.broadcasted_iota(jnp.int32, (C, C), 1)

    def log_gate(w_ref, b_ref):
        zz = jnp.dot(z_ref[...], w_ref[0], precision=HIGHEST, preferred_element_type=F32) + b_ref[0]
        return (jnp.minimum(zz, 0.0) - jnp.log(1.0 + jnp.exp(-jnp.abs(zz)))) * (1.0 / GLA_TAU)

    def chunk(fwd, c, st_t, la_scr, o_scr):
        keep = (col <= row) if fwd else (col >= row)
        sl = pl.ds(pl.multiple_of(c * C, C), C)
        cum = jnp.dot(keep.astype(F32), la_scr[sl, :], precision=HIGHEST, preferred_element_type=F32)
        tot = cum[C - 1:C, :] if fwd else cum[0:1, :]
        kk = k_ref[sl, :]
        qd = (q_ref[sl, :] * scale * jnp.exp(cum)).astype(BF16)
        ki = (kk * jnp.exp(-cum)).astype(BF16)
        kte = (kk * jnp.exp(tot - cum)).astype(BF16)
        vb = v_ref[sl, :].astype(BF16)
        attn = lax.dot_general(qd, ki, _NT, preferred_element_type=F32)
        attn = jnp.where(keep, attn, 0.0).astype(BF16)
        o = jnp.dot(attn, vb, preferred_element_type=F32)
        o_scr[sl, :] = o + lax.dot_general(qd, st_t.astype(BF16), _NT, preferred_element_type=F32)
        upd = lax.dot_general(vb, kte, _TN, preferred_element_type=F32)
        return st_t * jnp.exp(tot) + upd

    laf_scr[...] = log_gate(wgf_ref, bgf_ref)
    lab_scr[...] = log_gate(wgb_ref, bgb_ref)

    def body(i, states):
        return (chunk(True, i, states[0], laf_scr, of_scr),
                chunk(False, n - 1 - i, states[1], lab_scr, o_ref))

    zero = jnp.zeros((GLA_DV, GLA_DK), F32)
    lax.fori_loop(0, n, body, (zero, zero))
    y = _rms(of_scr[...] + o_ref[...], g_ref[0])
    o_ref[...] = y * _silu(r_ref[...])


def _gla(proj, gz, wgf, bgf, wgb, bgb, g_out, B, S):
    T = proj.shape[0]
    H = GLA_HEADS
    return pl.pallas_call(
        _gla_kernel,
        grid=(B, H),
        in_specs=[pl.BlockSpec((S, GLA_DK), lambda b, h: (b, _COL_GQ + h)),
                  pl.BlockSpec((S, GLA_DK), lambda b, h: (b, _COL_GK + h)),
                  pl.BlockSpec((S, GLA_DV), lambda b, h: (b, _COL_GV // 2 + h)),
                  pl.BlockSpec((S, GLA_DV), lambda b, h: (b, _COL_GR // 2 + h)),
                  pl.BlockSpec((S, LANE), lambda b, h: (b, 0)),
                  pl.BlockSpec((1, LANE, GLA_DK), lambda b, h: (h, 0, 0)),
                  pl.BlockSpec((1, 1, GLA_DK), lambda b, h: (h, 0, 0)),
                  pl.BlockSpec((1, LANE, GLA_DK), lambda b, h: (h, 0, 0)),
                  pl.BlockSpec((1, 1, GLA_DK), lambda b, h: (h, 0, 0)),
                  pl.BlockSpec((1, 1, GLA_DV), lambda b, h: (h, 0, 0))],
        out_specs=pl.BlockSpec((S, GLA_DV), lambda b, h: (b, h)),
        out_shape=jax.ShapeDtypeStruct((T, GLA_HEADS * GLA_DV), F32),
        scratch_shapes=[pltpu.VMEM((S, GLA_DK), F32), pltpu.VMEM((S, GLA_DK), F32),
                        pltpu.VMEM((S, GLA_DV), F32)],
        compiler_params=_params(("parallel", "parallel"), 56),
        name="gla",
    )(proj, proj, proj, proj, gz, wgf, bgf, wgb, bgb, g_out)


_DIL_QB = 256
_DIL_REACH = max(w // 2 for w, _ in DIL_PATTERNS) // _DIL_QB


def _dil_bias():
    qb = _DIL_QB
    d = np.arange(-_DIL_REACH, _DIL_REACH + 1)[:, None, None] * qb
    delta = d + np.arange(qb)[None, None, :] - np.arange(qb)[None, :, None]
    mult = np.zeros(delta.shape, np.float64)
    for window, dilation in DIL_PATTERNS:
        half = window // (2 * dilation)
        mult += ((delta % dilation) == 0) & (np.abs(delta) <= half * dilation)
    with np.errstate(divide="ignore"):
        bias = np.where(mult > 0, np.log(np.maximum(mult, 1.0)), NEG_INF)
    return jnp.asarray(bias, F32)


def _dil_kernel(q_ref, k_ref, v_ref, cs_ref, sn_ref, bias_ref, o_ref, kr_scr, vb_scr):
    qi = pl.program_id(2)
    nq = pl.num_programs(2)
    QB = _DIL_QB
    half = DIL_HD // 2

    @pl.when(qi == 0)
    def _():
        k = k_ref[...]
        kr_scr[...] = (k * cs_ref[...] + pltpu.roll(k, half, 1) * sn_ref[...]).astype(BF16)
        vb_scr[...] = v_ref[...].astype(BF16)

    r0 = pl.multiple_of(qi * QB, QB)
    q = q_ref[...]
    qr = q * cs_ref[pl.ds(r0, QB), :] + pltpu.roll(q, half, 1) * sn_ref[pl.ds(r0, QB), :]
    qb = (qr * (DIL_HD ** -0.5)).astype(BF16)

    m = jnp.full((QB, 1), NEG_INF, F32)
    l = jnp.zeros((QB, 1), F32)
    acc = jnp.zeros((QB, DIL_HD), F32)
    order = [0] + [s * d for d in range(1, _DIL_REACH + 1) for s in (-1, 1)]
    for d in order:
        kb = qi + d
        valid = jnp.logical_and(kb >= 0, kb < nq)
        k0 = pl.multiple_of(jnp.clip(kb, 0, nq - 1) * QB, QB)
        s = lax.dot_general(qb, kr_scr[pl.ds(k0, QB), :], _NT, preferred_element_type=F32)
        s = jnp.where(valid, s + bias_ref[d + _DIL_REACH], NEG_INF)
        m_new = jnp.maximum(m, jnp.max(s, axis=-1, keepdims=True))
        alpha = jnp.exp(m - m_new)
        p = jnp.exp(s - m_new)
        l = alpha * l + jnp.sum(p, axis=-1, keepdims=True)
        acc = alpha * acc + jnp.dot(p.astype(BF16), vb_scr[pl.ds(k0, QB), :],
                                    preferred_element_type=F32)
        m = m_new
    o_ref[...] = acc / l


def _dil(proj, cs, sn, B, S):
    T = proj.shape[0]
    QB = _DIL_QB
    nq = S // QB
    nb = 2 * _DIL_REACH + 1
    return pl.pallas_call(
        _dil_kernel,
        grid=(B, DIL_HEADS, nq),
        in_specs=[pl.BlockSpec((QB, DIL_HD), lambda b, h, i: (b * nq + i, _COL_DQ + h)),
                  pl.BlockSpec((S, DIL_HD), lambda b, h, i: (b, _COL_DK + h)),
                  pl.BlockSpec((S, DIL_HD), lambda b, h, i: (b, _COL_DV + h)),
                  pl.BlockSpec((S, DIL_HD), lambda b, h, i: (b, 0)),
                  pl.BlockSpec((S, DIL_HD), lambda b, h, i: (b, 0)),
                  pl.BlockSpec((nb, QB, QB), lambda b, h, i: (0, 0, 0))],
        out_specs=pl.BlockSpec((QB, DIL_HD), lambda b, h, i: (b * nq + i, h)),
        out_shape=jax.ShapeDtypeStruct((T, DIL_HEADS * DIL_HD), F32),
        scratch_shapes=[pltpu.VMEM((S, DIL_HD), BF16), pltpu.VMEM((S, DIL_HD), BF16)],
        compiler_params=_params(("parallel", "parallel", "arbitrary"), 48),
        name="dil",
    )(proj, proj, proj, cs, sn, _dil_bias())


def _outproj_kernel(og_ref, od_ref, w_ref, x_ref, ga_ref, o_ref):
    kg = og_ref.shape[1]
    mixed = jnp.dot(og_ref[...].astype(BF16), w_ref[:kg, :], preferred_element_type=F32)
    mixed = mixed + jnp.dot(od_ref[...].astype(BF16), w_ref[kg:, :], preferred_element_type=F32)
    o_ref[...] = x_ref[...] + ga_ref[0] * mixed


def _outproj(o_gla, o_dil, w_out, x2, mod3, S):
    T, D = x2.shape
    tm = 512
    per_b = S // tm
    kg, kd = o_gla.shape[1], o_dil.shape[1]
    return pl.pallas_call(
        _outproj_kernel,
        grid=(T // tm,),
        in_specs=[pl.BlockSpec((tm, kg), lambda i: (i, 0)),
                  pl.BlockSpec((tm, kd), lambda i: (i, 0)),
                  pl.BlockSpec((kg + kd, D), lambda i: (0, 0)),
                  pl.BlockSpec((tm, D), lambda i: (i, 0)),
                  pl.BlockSpec((1, 1, D), lambda i: ((i // per_b) * 6 + 2, 0, 0))],
        out_specs=pl.BlockSpec((tm, D), lambda i: (i, 0)),
        out_shape=jax.ShapeDtypeStruct((T, D), F32),
        compiler_params=_params(("parallel",), 48),
        name="outproj",
    )(o_gla, o_dil, w_out, x2, mod3)


def _pq_kernel(x_ref, g_ref, sc_ref, sh_ref, w_ref, o_ref):
    h = _rms(x_ref[...], g_ref[...]) * (1.0 + sc_ref[0]) + sh_ref[0]
    o_ref[...] = jnp.dot(h.astype(BF16), w_ref[...], preferred_element_type=F32)


def _pq(x1, g, mod3, wq, S):
    T, D = x1.shape
    N = wq.shape[1]
    tm = 512
    per_b = S // tm
    return pl.pallas_call(
        _pq_kernel,
        grid=(T // tm,),
        in_specs=[pl.BlockSpec((tm, D), lambda i: (i, 0)),
                  pl.BlockSpec((1, D), lambda i: (0, 0)),
                  pl.BlockSpec((1, 1, D), lambda i: ((i // per_b) * 6 + 4, 0, 0)),
                  pl.BlockSpec((1, 1, D), lambda i: ((i // per_b) * 6 + 3, 0, 0)),
                  pl.BlockSpec((D, N), lambda i: (0, 0))],
        out_specs=pl.BlockSpec((tm, N), lambda i: (i, 0)),
        out_shape=jax.ShapeDtypeStruct((T, N), F32),
        compiler_params=_params(("parallel",), 48),
        name="pq",
    )(x1, g, mod3, mod3, wq)


def _top_rows(s, k, payload=None):
    n_rows = s.shape[0]
    rid = lax.broadcasted_iota(jnp.int32, s.shape, 0)
    vals, picks = [], []
    for _ in range(k):
        m = jnp.max(s, axis=0, keepdims=True)
        pos = jnp.min(jnp.where(s == m, rid, n_rows), axis=0, keepdims=True)
        hit = rid == pos
        vals.append(m)
        if payload is None:
            picks.append(pos)
        else:
            picks.append(jnp.sum(jnp.where(hit, payload, 0), axis=0, keepdims=True))
        s = jnp.where(hit, -jnp.inf, s)
    return jnp.concatenate(vals, axis=0), jnp.concatenate(picks, axis=0)


def _staircase(a, b, combine, fill):
    K = a.shape[0]
    half = K // 2
    jrow = lax.broadcasted_iota(jnp.int32, (half, a.shape[1]), 0)
    pieces = [combine(a[0:1], b)]
    for i in range(1, half):
        piece = combine(a[i:i + 1], b[0:half])
        width = K // (i + 1)
        pieces.append(piece if width >= half else jnp.where(jrow < width, piece, fill))
    pieces.append(combine(a[half:K], b[0:1]))
    return jnp.concatenate(pieces, axis=0)


def _topk_kernel(q_ref, keys_ref, idx_ref, gate_ref):
    K = PEER_TOPK
    for h in range(PEER_HEADS):
        tops = []
        for half in range(2):
            c0 = (h * 2 + half) * PEER_HALF
            qh = q_ref[:, c0:c0 + PEER_HALF].astype(BF16)
            sc = lax.dot_general(keys_ref[h, half], qh, _NT, preferred_element_type=F32)
            tops.append(_top_rows(sc, K))
        (s0, i0), (s1, i1) = tops
        cand_s = _staircase(s0, s1, lambda a, b: a + b, -jnp.inf)
        cand_i = _staircase(i0, i1, lambda a, b: a * PEER_NKEYS + b, 0)
        best, idx = _top_rows(cand_s, K, payload=cand_i)
        e = jnp.exp(best - best[0:1])
        gate = e / jnp.sum(e, axis=0, keepdims=True)
        idx_ref[h * K:(h + 1) * K, :] = idx
        gate_ref[h * K:(h + 1) * K, :] = gate


def _topk(qp, keys_bf):
    T, N = qp.shape
    tt = 256
    HK = PEER_HEADS * PEER_TOPK
    return pl.pallas_call(
        _topk_kernel,
        grid=(T // tt,),
        in_specs=[pl.BlockSpec((tt, N), lambda i: (i, 0)),
                  pl.BlockSpec(keys_bf.shape, lambda i: (0, 0, 0, 0))],
        out_specs=[pl.BlockSpec((HK, tt), lambda i: (0, i)),
                   pl.BlockSpec((HK, tt), lambda i: (0, i))],
        out_shape=[jax.ShapeDtypeStruct((HK, T), jnp.int32),
                   jax.ShapeDtypeStruct((HK, T), F32)],
        compiler_params=_params(("parallel",), 32),
        name="topk",
    )(qp, keys_bf)


_PEER_TB = 128
_PEER_SUB = 8
_SUBLANES = 8
_PEER_CHUNKS = 16
_SC_LANES = 16
_SC_ROWS = 16


def _peer_u_kernel(idx_hbm, gate_ref, x1_ref, gn_ref, sc_ref, sh_ref, u_hbm, w_ref,
                   idx_smem, ub0, ub1, h_scr, sem_i, sem_u, *, step0):
    HK = PEER_HEADS * PEER_TOPK
    TB, SUB = _PEER_TB, _PEER_SUB
    R = SUB * HK
    N = TB * HK
    nsub = TB // SUB
    D = x1_ref.shape[1]
    nchunk = D // LANE
    tiles = HK // _SUBLANES
    i = pl.program_id(0)
    n = pl.num_programs(0)
    cur = lax.rem(i, 2) * N
    nxt = N - cur
    more = i + 1 < n
    ubufs = (ub0, ub1)

    def idx_copy(step, base):
        return pltpu.make_async_copy(idx_hbm.at[pl.ds((step0 + step) * N, N)],
                                     idx_smem.at[pl.ds(base, N)], sem_i)

    def issue_token(base, t, slot):
        for k in range(HK):
            e = idx_smem[base + t * HK + k]
            rt, s = t * tiles + k // _SUBLANES, k % _SUBLANES
            pltpu.make_async_copy(u_hbm.at[e], ubufs[slot].at[rt, :, s, :],
                                  sem_u.at[slot]).start(priority=k % 2)

    def wait(slot):
        pltpu.make_async_copy(ubufs[slot], ubufs[slot], sem_u.at[slot]).wait()

    @pl.when(i == 0)
    def _():
        first = idx_copy(0, 0)
        first.start()
        first.wait()
        for t in range(SUB):
            issue_token(0, t, 0)

    @pl.when(more)
    def _():
        idx_copy(i + 1, nxt).start()

    h_scr[...] = _rms(x1_ref[...], gn_ref[...]) * (1.0 + sc_ref[0]) + sh_ref[0]
    lane = lax.broadcasted_iota(jnp.int32, (HK, TB), 1)

    def compute_token(j, t, slot, wacc):
        ub = ubufs[slot]
        tok = j * SUB + t
        xt = h_scr[pl.ds(tok, 1), :]
        rows = slice(t * tiles, (t + 1) * tiles)
        part = ub[rows, 0].reshape(HK, LANE) * xt[:, 0:LANE]
        for c in range(1, nchunk):
            part = part + ub[rows, c].reshape(HK, LANE) * xt[:, c * LANE:(c + 1) * LANE]
        a = jnp.sum(part, axis=1, keepdims=True)
        hit = lane == tok
        g = jnp.sum(jnp.where(hit, gate_ref[...], 0.0), axis=1, keepdims=True)
        wgt = g * (0.5 * a * (1.0 + lax.erf(a * (2.0 ** -0.5))))
        return jnp.where(hit, wgt, wacc)

    def half(j, slot, next_base, wacc):
        wait(slot)
        for t in range(SUB):
            issue_token(next_base, t, 1 - slot)
            wacc = compute_token(j, t, slot, wacc)
        return wacc

    def pair(jj, wacc):
        j0 = 2 * jj
        wacc = half(j0, 0, cur + (j0 + 1) * R, wacc)
        last = jj == nsub // 2 - 1

        @pl.when(jnp.logical_and(last, more))
        def _():
            idx_copy(i + 1, nxt).wait()

        after = jnp.where(more, nxt, cur)
        return half(j0 + 1, 1, jnp.where(last, after, cur + (j0 + 2) * R), wacc)

    wacc = lax.fori_loop(0, nsub // 2, pair, jnp.zeros((HK, TB), F32))

    @pl.when(jnp.logical_not(more))
    def _():
        wait(0)

    w_ref[:, :HK] = jnp.zeros((TB, HK), F32)
    w_ref[:, HK:] = wacc.T


def _peer_u(idx_flat, gate_t, x1, g_norm, mod3, u3, S, step0, nsteps):
    T, D = x1.shape
    HK = PEER_HEADS * PEER_TOPK
    TB, SUB = _PEER_TB, _PEER_SUB
    per_b = S // TB
    modrow = lambda k: (lambda i: (((step0 + i) // per_b) * 6 + k, 0, 0))
    gbuf = pltpu.VMEM((SUB * HK // _SUBLANES, D // LANE, _SUBLANES, LANE), F32)
    return pl.pallas_call(
        functools.partial(_peer_u_kernel, step0=step0),
        grid=(nsteps,),
        in_specs=[pl.BlockSpec(memory_space=pl.ANY),
                  pl.BlockSpec((HK, TB), lambda i: (0, step0 + i)),
                  pl.BlockSpec((TB, D), lambda i: (step0 + i, 0)),
                  pl.BlockSpec((1, D), lambda i: (0, 0)),
                  pl.BlockSpec((1, 1, D), modrow(4)),
                  pl.BlockSpec((1, 1, D), modrow(3)),
                  pl.BlockSpec(memory_space=pl.ANY)],
        out_specs=pl.BlockSpec((TB, 2 * HK), lambda i: (i, 0)),
        out_shape=jax.ShapeDtypeStruct((nsteps * TB, 2 * HK), F32),
        scratch_shapes=[pltpu.SMEM((2 * TB * HK,), jnp.int32),
                        gbuf, gbuf,
                        pltpu.VMEM((TB, D), F32),
                        pltpu.SemaphoreType.DMA,
                        pltpu.SemaphoreType.DMA((2,))],
        compiler_params=_params(("arbitrary",), 40),
        name="peer_u",
    )(idx_flat, gate_t, x1, g_norm, mod3, mod3, u3)


def _sc_peer_v(v_tab, idx_flat, wgt, tok_base):
    E, nblk, _ = v_tab.shape
    D = nblk * LANE
    Tc, HK = wgt.shape[0], wgt.shape[1] // 2
    info = plsc.get_sparse_core_info()
    nw = info.num_cores * info.num_subcores
    tpw = Tc // nw
    CH = _SC_ROWS
    nch = HK // CH
    nsl = D // _SC_LANES
    mesh = plsc.VectorSubcoreMesh(core_axis_name="c", subcore_axis_name="s")

    @functools.partial(
        pl.kernel, mesh=mesh, out_type=jax.ShapeDtypeStruct((Tc, D), F32),
        scratch_types=[pltpu.VMEM((tpw * HK,), jnp.int32), pltpu.VMEM((2 * HK,), F32),
                       pltpu.VMEM((D,), F32),
                       pltpu.VMEM((CH, nblk, LANE), F32), pltpu.VMEM((CH, nblk, LANE), F32),
                       pltpu.SemaphoreType.DMA, pltpu.SemaphoreType.DMA],
        compiler_params=pltpu.CompilerParams(needs_layout_passes=False),
        name="sc_peer_v",
    )
    def k(tab_hbm, idx_hbm, w_hbm, o_hbm, idx_v, w_v, o_v, buf0, buf1, g0, g1):
        wid = lax.axis_index("s") * info.num_cores + lax.axis_index("c")
        tok0 = wid * tpw
        pltpu.sync_copy(idx_hbm.at[pl.ds((tok_base + tok0) * HK, tpw * HK)], idx_v)
        bufs, gs = (buf0, buf1), (g0, g1)

        def gather(g, b):
            return pltpu.make_async_copy(tab_hbm.at[idx_v.at[pl.ds(g * CH, CH)]], bufs[b], gs[b])

        gather(0, 0).start()

        @pl.loop(0, tpw)
        def _(t):
            pltpu.sync_copy(w_hbm.at[tok0 + t], w_v)

            @pl.loop(0, nsl)
            def _(c):
                o_v[pl.ds(pl.multiple_of(c * _SC_LANES, _SC_LANES), _SC_LANES)] = jnp.zeros((_SC_LANES,), F32)

            for ch in range(nch):
                b = ch % 2
                g = t * nch + ch
                gather(g, b).wait()

                @pl.when(g + 1 < tpw * nch)
                def _():
                    gather(g + 1, 1 - b).start()

                ws = [plsc.load_gather(w_v, [jnp.full((_SC_LANES,), HK + ch * CH + r, jnp.int32)])
                      for r in range(CH)]

                @pl.loop(0, nsl, step=2)
                def _(c):
                    for half in range(2):
                        off = pl.multiple_of((c + half) * _SC_LANES, _SC_LANES)
                        blk = (c + half) // (LANE // _SC_LANES)
                        lo = pl.multiple_of(off - blk * LANE, _SC_LANES)
                        parts = [ws[r] * bufs[b][r, blk, pl.ds(lo, _SC_LANES)] for r in range(CH)]
                        while len(parts) > 1:
                            parts = [parts[p] + parts[p + 1] for p in range(0, len(parts), 2)]
                        o_v[pl.ds(off, _SC_LANES)] = o_v[pl.ds(off, _SC_LANES)] + parts[0]

            pltpu.sync_copy(o_v, o_hbm.at[tok0 + t])

    return k(v_tab, idx_flat, wgt)


def _peer_fin_kernel(x_ref, p_ref, ga_ref, gf_ref, o_ref):
    o_ref[...] = _rms(x_ref[...] + ga_ref[0] * p_ref[...], gf_ref[...])


def _peer_fin(x1, po, mod3, g_final, S):
    T, D = x1.shape
    tm = 512
    per_b = S // tm
    return pl.pallas_call(
        _peer_fin_kernel,
        grid=(T // tm,),
        in_specs=[pl.BlockSpec((tm, D), lambda i: (i, 0)),
                  pl.BlockSpec((tm, D), lambda i: (i, 0)),
                  pl.BlockSpec((1, 1, D), lambda i: ((i // per_b) * 6 + 5, 0, 0)),
                  pl.BlockSpec((1, D), lambda i: (0, 0))],
        out_specs=pl.BlockSpec((tm, D), lambda i: (i, 0)),
        out_shape=jax.ShapeDtypeStruct((T, D), F32),
        compiler_params=_params(("parallel",), 40),
        name="peer_fin",
    )(x1, po, mod3, g_final)


def _pad_gate(w, lo):
    rank = w.shape[0]
    wh = w.reshape(rank, GLA_HEADS, GLA_DK).transpose(1, 0, 2)
    return jnp.zeros((GLA_HEADS, LANE, GLA_DK), F32).at[:, lo:lo + rank, :].set(wh)


def kernel(x, c, positions, w_ada, b_ada, g_norm_mix, w_in, w_gate_f, b_gate_f, w_gate_b, b_gate_b,
           g_gla_out, w_out, g_norm_ffn, w_peer_q, peer_sub_keys, peer_u, peer_v, g_final):
    B, S, D = x.shape
    T = B * S
    depth = w_ada.shape[0]
    assert depth == 1, "the final norm is fused into the last PEER call; one layer only"
    xt = x.reshape(T, D)
    cs, sn = _rope_tables(positions.reshape(T, 1))
    gz0 = 2 * GLA_HEADS * GLA_DK + 2 * GLA_HEADS * GLA_DV
    gz1 = gz0 + 2 * GLA_GATE_RANK
    for l in range(depth):
        mod3 = _ada(c, w_ada[l], b_ada[l]).reshape(B * 6, 1, D)
        w_main = jnp.concatenate([w_in[l][:, :gz0], w_in[l][:, gz1:]], axis=1).astype(BF16)
        w_z = jnp.pad(w_in[l][:, gz0:gz1], ((0, 0), (0, LANE - (gz1 - gz0)))).astype(BF16)
        proj, gz = _inproj(xt, g_norm_mix[l].reshape(1, D), mod3, w_main, w_z, S)
        o_gla = _gla(proj, gz,
                     _pad_gate(w_gate_f[l], 0), b_gate_f[l].reshape(GLA_HEADS, 1, GLA_DK),
                     _pad_gate(w_gate_b[l], GLA_GATE_RANK), b_gate_b[l].reshape(GLA_HEADS, 1, GLA_DK),
                     g_gla_out[l].reshape(GLA_HEADS, 1, GLA_DV), B, S)
        o_dil = _dil(proj, cs, sn, B, S)
        x1 = _outproj(o_gla, o_dil, w_out[l].astype(BF16), xt, mod3, S)
        qp = _pq(x1, g_norm_ffn[l].reshape(1, D), mod3, w_peer_q[l].astype(BF16), S)
        idx_t, gate_t = _topk(qp, peer_sub_keys[l].astype(BF16))
        idx_flat = idx_t.T.reshape(-1)
        E = peer_u.shape[1]
        u3 = peer_u[l].reshape(E, D // LANE, LANE)
        v3 = peer_v[l].reshape(E, D // LANE, LANE)
        steps = T // _PEER_TB // _PEER_CHUNKS
        outs = []
        for ck in range(_PEER_CHUNKS):
            wgt = _peer_u(idx_flat, gate_t, x1, g_norm_ffn[l].reshape(1, D), mod3, u3, S, ck * steps, steps)
            outs.append(_sc_peer_v(v3, idx_flat, wgt, ck * steps * _PEER_TB))
        xt = _peer_fin(x1, jnp.concatenate(outs, axis=0), mod3, g_final.reshape(1, D), S)
    return xt.reshape(B, S, D)
```

```python
import functools
import math

import numpy as np
import jax
import jax.numpy as jnp
from jax import lax
from jax.experimental import pallas as pl
from jax.experimental.pallas import tpu as pltpu
from jax.experimental.pallas import tpu_sc as plsc

F32 = jnp.float32
BF16 = jnp.bfloat16
HIGHEST = lax.Precision.HIGHEST

NORM_EPS = 1e-6
GLA_HEADS = 4
GLA_DK = 128
GLA_DV = 256
GLA_GATE_RANK = 16
GLA_TAU = 16.0
GLA_CHUNK = 64
DIL_HD = 128
DIL_HEADS = 8
DIL_PATTERNS = ((128, 1), (512, 4), (2048, 16))
ROPE_THETA = 10000.0
NEG_INF = -1e30
PEER_HEADS = 8
PEER_NKEYS = 128
PEER_TOPK = 16
PEER_HALF = 128

LANE = 128
MIB = 1024 * 1024

_COL_GQ, _COL_GK, _COL_GV, _COL_GR, _COL_DQ, _COL_DK, _COL_DV = 0, 4, 8, 16, 24, 32, 40
_PROJ_W = 48 * LANE

_NT = (((1,), (1,)), ((), ()))
_TN = (((0,), (0,)), ((), ()))


def _params(sem, vmem_mib):
    return pltpu.CompilerParams(dimension_semantics=sem, vmem_limit_bytes=vmem_mib * MIB)


def _rms(x, g):
    return x * lax.rsqrt(jnp.mean(x * x, axis=-1, keepdims=True) + NORM_EPS) * g


def _silu(x):
    return x / (1.0 + jnp.exp(-x))


def _ada_kernel(c_ref, w_ref, b_ref, o_ref):
    s = _silu(c_ref[...]).astype(BF16)
    o_ref[...] = jnp.dot(s, w_ref[...].astype(BF16), preferred_element_type=F32) + b_ref[...]


def _ada(c, w, b):
    B, D = c.shape
    N = w.shape[1]
    tn = 1024
    cp = jnp.zeros((8, D), F32).at[:B].set(c)
    out = pl.pallas_call(
        _ada_kernel,
        grid=(N // tn,),
        in_specs=[pl.BlockSpec((8, D), lambda j: (0, 0)),
                  pl.BlockSpec((D, tn), lambda j: (0, j)),
                  pl.BlockSpec((1, tn), lambda j: (0, j))],
        out_specs=pl.BlockSpec((8, tn), lambda j: (0, j)),
        out_shape=jax.ShapeDtypeStruct((8, N), F32),
        compiler_params=_params(("parallel",), 40),
        name="ada",
    )(cp, w, b.reshape(1, N))
    return out[:B]


def _inproj_kernel(x_ref, g_ref, sc_ref, sh_ref, w_ref, wz_ref, o_ref, z_ref, h_scr):
    @pl.when(pl.program_id(1) == 0)
    def _():
        h = _rms(x_ref[...], g_ref[...]) * (1.0 + sc_ref[0]) + sh_ref[0]
        hb = h.astype(BF16)
        h_scr[...] = hb
        z_ref[...] = jnp.dot(hb, wz_ref[...], preferred_element_type=F32)

    o_ref[...] = jnp.dot(h_scr[...], w_ref[...], preferred_element_type=F32)


def _inproj(x2, g, mod3, w_main, w_z, S):
    T, D = x2.shape
    tm, tn = 1024, 768
    per_b = S // tm
    return pl.pallas_call(
        _inproj_kernel,
        grid=(T // tm, _PROJ_W // tn),
        in_specs=[pl.BlockSpec((tm, D), lambda i, j: (i, 0)),
                  pl.BlockSpec((1, D), lambda i, j: (0, 0)),
                  pl.BlockSpec((1, 1, D), lambda i, j: ((i // per_b) * 6 + 1, 0, 0)),
                  pl.BlockSpec((1, 1, D), lambda i, j: ((i // per_b) * 6 + 0, 0, 0)),
                  pl.BlockSpec((D, tn), lambda i, j: (0, j)),
                  pl.BlockSpec((D, LANE), lambda i, j: (0, 0))],
        out_specs=[pl.BlockSpec((tm, tn), lambda i, j: (i, j)),
                   pl.BlockSpec((tm, LANE), lambda i, j: (i, 0))],
        out_shape=[jax.ShapeDtypeStruct((T, _PROJ_W), F32),
                   jax.ShapeDtypeStruct((T, LANE), F32)],
        scratch_shapes=[pltpu.VMEM((tm, D), BF16)],
        compiler_params=_params(("parallel", "arbitrary"), 48),
        name="inproj",
    )(x2, g, mod3, mod3, w_main, w_z)


def _rope_kernel(pos_ref, f_ref, sg_ref, cs_ref, sn_ref):
    ang = pos_ref[...].astype(F32) * f_ref[...]
    cs_ref[...] = jnp.cos(ang)
    sn_ref[...] = jnp.sin(ang) * sg_ref[...]


def _rope_tables(pos_col):
    T = pos_col.shape[0]
    half = DIL_HD // 2
    inv = jnp.power(ROPE_THETA, -jnp.arange(half, dtype=F32) * 2.0 / DIL_HD)
    freq = jnp.concatenate([inv, inv]).reshape(1, DIL_HD)
    sign = jnp.concatenate([-jnp.ones((half,), F32), jnp.ones((half,), F32)]).reshape(1, DIL_HD)
    tm = 1024
    return pl.pallas_call(
        _rope_kernel,
        grid=(T // tm,),
        in_specs=[pl.BlockSpec((tm, 1), lambda i: (i, 0)),
                  pl.BlockSpec((1, DIL_HD), lambda i: (0, 0)),
                  pl.BlockSpec((1, DIL_HD), lambda i: (0, 0))],
        out_specs=[pl.BlockSpec((tm, DIL_HD), lambda i: (i, 0)),
                   pl.BlockSpec((tm, DIL_HD), lambda i: (i, 0))],
        out_shape=[jax.ShapeDtypeStruct((T, DIL_HD), F32)] * 2,
        compiler_params=_params(("parallel",), 32),
        name="rope",
    )(pos_col, freq, sign)


def _gla_kernel(q_ref, k_ref, v_ref, r_ref, z_ref, wgf_ref, bgf_ref, wgb_ref, bgb_ref, g_ref,
                o_ref, laf_scr, lab_scr, of_scr):
    S = q_ref.shape[0]
    C = GLA_CHUNK
    n = S // C
    scale = GLA_DK ** -0.5
    row = lax.broadcasted_iota(jnp.int32, (C, C), 0)
    col = lax.broadcasted_iota(jnp.int32, (C, C), 1)

    def log_gate(w_ref, b_ref):
        zz = jnp.dot(z_ref[...], w_ref[0], precision=HIGHEST, preferred_element_type=F32) + b_ref[0]
        return (jnp.minimum(zz, 0.0) - jnp.log(1.0 + jnp.exp(-jnp.abs(zz)))) * (1.0 / GLA_TAU)

    def chunk(fwd, c, st_t, la_scr, o_scr):
        keep = (col <= row) if fwd else (col >= row)
        sl = pl.ds(pl.multiple_of(c * C, C), C)
        cum = jnp.dot(keep.astype(F32), la_scr[sl, :], precision=HIGHEST, preferred_element_type=F32)
        tot = cum[C - 1:C, :] if fwd else cum[0:1, :]
        kk = k_ref[sl, :]
        qd = (q_ref[sl, :] * scale * jnp.exp(cum)).astype(BF16)
        ki = (kk * jnp.exp(-cum)).astype(BF16)
        kte = (kk * jnp.exp(tot - cum)).astype(BF16)
        vb = v_ref[sl, :].astype(BF16)
        attn = lax.dot_general(qd, ki, _NT, preferred_element_type=F32)
        attn = jnp.where(keep, attn, 0.0).astype(BF16)
        o = jnp.dot(attn, vb, preferred_element_type=F32)
        o_scr[sl, :] = o + lax.dot_general(qd, st_t.astype(BF16), _NT, preferred_element_type=F32)
        upd = lax.dot_general(vb, kte, _TN, preferred_element_type=F32)
        return st_t * jnp.exp(tot) + upd

    laf_scr[...] = log_gate(wgf_ref, bgf_ref)
    lab_scr[...] = log_gate(wgb_ref, bgb_ref)

    def body(i, states):
        return (chunk(True, i, states[0], laf_scr, of_scr),
                chunk(False, n - 1 - i, states[1], lab_scr, o_ref))

    zero = jnp.zeros((GLA_DV, GLA_DK), F32)
    lax.fori_loop(0, n, body, (zero, zero))
    y = _rms(of_scr[...] + o_ref[...], g_ref[0])
    o_ref[...] = y * _silu(r_ref[...])


def _gla(proj, gz, wgf, bgf, wgb, bgb, g_out, B, S):
    T = proj.shape[0]
    H = GLA_HEADS
    return pl.pallas_call(
        _gla_kernel,
        grid=(B, H),
        in_specs=[pl.BlockSpec((S, GLA_DK), lambda b, h: (b, _COL_GQ + h)),
                  pl.BlockSpec((S, GLA_DK), lambda b, h: (b, _COL_GK + h)),
                  pl.BlockSpec((S, GLA_DV), lambda b, h: (b, _COL_GV // 2 + h)),
                  pl.BlockSpec((S, GLA_DV), lambda b, h: (b, _COL_GR // 2 + h)),
                  pl.BlockSpec((S, LANE), lambda b, h: (b, 0)),
                  pl.BlockSpec((1, LANE, GLA_DK), lambda b, h: (h, 0, 0)),
                  pl.BlockSpec((1, 1, GLA_DK), lambda b, h: (h, 0, 0)),
                  pl.BlockSpec((1, LANE, GLA_DK), lambda b, h: (h, 0, 0)),
                  pl.BlockSpec((1, 1, GLA_DK), lambda b, h: (h, 0, 0)),
                  pl.BlockSpec((1, 1, GLA_DV), lambda b, h: (h, 0, 0))],
        out_specs=pl.BlockSpec((S, GLA_DV), lambda b, h: (b, h)),
        out_shape=jax.ShapeDtypeStruct((T, GLA_HEADS * GLA_DV), F32),
        scratch_shapes=[pltpu.VMEM((S, GLA_DK), F32), pltpu.VMEM((S, GLA_DK), F32),
                        pltpu.VMEM((S, GLA_DV), F32)],
        compiler_params=_params(("parallel", "parallel"), 56),
        name="gla",
    )(proj, proj, proj, proj, gz, wgf, bgf, wgb, bgb, g_out)


_DIL_QB = 256
_DIL_REACH = max(w // 2 for w, _ in DIL_PATTERNS) // _DIL_QB


def _dil_bias():
    qb = _DIL_QB
    d = np.arange(-_DIL_REACH, _DIL_REACH + 1)[:, None, None] * qb
    delta = d + np.arange(qb)[None, None, :] - np.arange(qb)[None, :, None]
    mult = np.zeros(delta.shape, np.float64)
    for window, dilation in DIL_PATTERNS:
        half = window // (2 * dilation)
        mult += ((delta % dilation) == 0) & (np.abs(delta) <= half * dilation)
    with np.errstate(divide="ignore"):
        bias = np.where(mult > 0, np.log(np.maximum(mult, 1.0)), NEG_INF)
    return jnp.asarray(bias, F32)


def _dil_kernel(q_ref, k_ref, v_ref, cs_ref, sn_ref, bias_ref, o_ref, kr_scr, vb_scr):
    qi = pl.program_id(2)
    nq = pl.num_programs(2)
    QB = _DIL_QB
    half = DIL_HD // 2

    @pl.when(qi == 0)
    def _():
        k = k_ref[...]
        kr_scr[...] = (k * cs_ref[...] + pltpu.roll(k, half, 1) * sn_ref[...]).astype(BF16)
        vb_scr[...] = v_ref[...].astype(BF16)

    r0 = pl.multiple_of(qi * QB, QB)
    q = q_ref[...]
    qr = q * cs_ref[pl.ds(r0, QB), :] + pltpu.roll(q, half, 1) * sn_ref[pl.ds(r0, QB), :]
    qb = (qr * (DIL_HD ** -0.5)).astype(BF16)

    m = jnp.full((QB, 1), NEG_INF, F32)
    l = jnp.zeros((QB, 1), F32)
    acc = jnp.zeros((QB, DIL_HD), F32)
    order = [0] + [s * d for d in range(1, _DIL_REACH + 1) for s in (-1, 1)]
    for d in order:
        kb = qi + d
        valid = jnp.logical_and(kb >= 0, kb < nq)
        k0 = pl.multiple_of(jnp.clip(kb, 0, nq - 1) * QB, QB)
        s = lax.dot_general(qb, kr_scr[pl.ds(k0, QB), :], _NT, preferred_element_type=F32)
        s = jnp.where(valid, s + bias_ref[d + _DIL_REACH], NEG_INF)
        m_new = jnp.maximum(m, jnp.max(s, axis=-1, keepdims=True))
        alpha = jnp.exp(m - m_new)
        p = jnp.exp(s - m_new)
        l = alpha * l + jnp.sum(p, axis=-1, keepdims=True)
        acc = alpha * acc + jnp.dot(p.astype(BF16), vb_scr[pl.ds(k0, QB), :],
                                    preferred_element_type=F32)
        m = m_new
    o_ref[...] = acc / l


def _dil(proj, cs, sn, B, S):
    T = proj.shape[0]
    QB = _DIL_QB
    nq = S // QB
    nb = 2 * _DIL_REACH + 1
    return pl.pallas_call(
        _dil_kernel,
        grid=(B, DIL_HEADS, nq),
        in_specs=[pl.BlockSpec((QB, DIL_HD), lambda b, h, i: (b * nq + i, _COL_DQ + h)),
                  pl.BlockSpec((S, DIL_HD), lambda b, h, i: (b, _COL_DK + h)),
                  pl.BlockSpec((S, DIL_HD), lambda b, h, i: (b, _COL_DV + h)),
                  pl.BlockSpec((S, DIL_HD), lambda b, h, i: (b, 0)),
                  pl.BlockSpec((S, DIL_HD), lambda b, h, i: (b, 0)),
                  pl.BlockSpec((nb, QB, QB), lambda b, h, i: (0, 0, 0))],
        out_specs=pl.BlockSpec((QB, DIL_HD), lambda b, h, i: (b * nq + i, h)),
        out_shape=jax.ShapeDtypeStruct((T, DIL_HEADS * DIL_HD), F32),
        scratch_shapes=[pltpu.VMEM((S, DIL_HD), BF16), pltpu.VMEM((S, DIL_HD), BF16)],
        compiler_params=_params(("parallel", "parallel", "arbitrary"), 48),
        name="dil",
    )(proj, proj, proj, cs, sn, _dil_bias())


def _outproj_kernel(og_ref, od_ref, w_ref, x_ref, ga_ref, o_ref):
    kg = og_ref.shape[1]
    mixed = jnp.dot(og_ref[...].astype(BF16), w_ref[:kg, :], preferred_element_type=F32)
    mixed = mixed + jnp.dot(od_ref[...].astype(BF16), w_ref[kg:, :], preferred_element_type=F32)
    o_ref[...] = x_ref[...] + ga_ref[0] * mixed


def _outproj(o_gla, o_dil, w_out, x2, mod3, S):
    T, D = x2.shape
    tm = 512
    per_b = S // tm
    kg, kd = o_gla.shape[1], o_dil.shape[1]
    return pl.pallas_call(
        _outproj_kernel,
        grid=(T // tm,),
        in_specs=[pl.BlockSpec((tm, kg), lambda i: (i, 0)),
                  pl.BlockSpec((tm, kd), lambda i: (i, 0)),
                  pl.BlockSpec((kg + kd, D), lambda i: (0, 0)),
                  pl.BlockSpec((tm, D), lambda i: (i, 0)),
                  pl.BlockSpec((1, 1, D), lambda i: ((i // per_b) * 6 + 2, 0, 0))],
        out_specs=pl.BlockSpec((tm, D), lambda i: (i, 0)),
        out_shape=jax.ShapeDtypeStruct((T, D), F32),
        compiler_params=_params(("parallel",), 48),
        name="outproj",
    )(o_gla, o_dil, w_out, x2, mod3)


def _pq_kernel(x_ref, g_ref, sc_ref, sh_ref, w_ref, o_ref, h_ref):
    h = _rms(x_ref[...], g_ref[...]) * (1.0 + sc_ref[0]) + sh_ref[0]
    h_ref[...] = h
    o_ref[...] = jnp.dot(h.astype(BF16), w_ref[...], preferred_element_type=F32)


def _pq(x1, g, mod3, wq, S):
    T, D = x1.shape
    N = wq.shape[1]
    tm = 512
    per_b = S // tm
    return pl.pallas_call(
        _pq_kernel,
        grid=(T // tm,),
        in_specs=[pl.BlockSpec((tm, D), lambda i: (i, 0)),
                  pl.BlockSpec((1, D), lambda i: (0, 0)),
                  pl.BlockSpec((1, 1, D), lambda i: ((i // per_b) * 6 + 4, 0, 0)),
                  pl.BlockSpec((1, 1, D), lambda i: ((i // per_b) * 6 + 3, 0, 0)),
                  pl.BlockSpec((D, N), lambda i: (0, 0))],
        out_specs=[pl.BlockSpec((tm, N), lambda i: (i, 0)),
                   pl.BlockSpec((tm, D), lambda i: (i, 0))],
        out_shape=[jax.ShapeDtypeStruct((T, N), F32), jax.ShapeDtypeStruct((T, D), F32)],
        compiler_params=_params(("parallel",), 56),
        name="pq",
    )(x1, g, mod3, mod3, wq)


def _top_rows(s, k, payload=None):
    n_rows = s.shape[0]
    rid = lax.broadcasted_iota(jnp.int32, s.shape, 0)
    vals, picks = [], []
    for _ in range(k):
        m = jnp.max(s, axis=0, keepdims=True)
        pos = jnp.min(jnp.where(s == m, rid, n_rows), axis=0, keepdims=True)
        hit = rid == pos
        vals.append(m)
        if payload is None:
            picks.append(pos)
        else:
            picks.append(jnp.sum(jnp.where(hit, payload, 0), axis=0, keepdims=True))
        s = jnp.where(hit, -jnp.inf, s)
    return jnp.concatenate(vals, axis=0), jnp.concatenate(picks, axis=0)


def _staircase(a, b, combine, fill):
    K = a.shape[0]
    half = K // 2
    jrow = lax.broadcasted_iota(jnp.int32, (half, a.shape[1]), 0)
    pieces = [combine(a[0:1], b)]
    for i in range(1, half):
        piece = combine(a[i:i + 1], b[0:half])
        width = K // (i + 1)
        pieces.append(piece if width >= half else jnp.where(jrow < width, piece, fill))
    pieces.append(combine(a[half:K], b[0:1]))
    return jnp.concatenate(pieces, axis=0)


def _topk_kernel(q_ref, keys_ref, idx_ref, gate_ref):
    K = PEER_TOPK
    for h in range(PEER_HEADS):
        tops = []
        for half in range(2):
            c0 = (h * 2 + half) * PEER_HALF
            qh = q_ref[:, c0:c0 + PEER_HALF].astype(BF16)
            sc = lax.dot_general(keys_ref[h, half], qh, _NT, preferred_element_type=F32)
            tops.append(_top_rows(sc, K))
        (s0, i0), (s1, i1) = tops
        cand_s = _staircase(s0, s1, lambda a, b: a + b, -jnp.inf)
        cand_i = _staircase(i0, i1, lambda a, b: a * PEER_NKEYS + b, 0)
        best, idx = _top_rows(cand_s, K, payload=cand_i)
        e = jnp.exp(best - best[0:1])
        gate = e / jnp.sum(e, axis=0, keepdims=True)
        idx_ref[h * K:(h + 1) * K, :] = idx
        gate_ref[h * K:(h + 1) * K, :] = gate


def _topk(qp, keys_bf):
    T, N = qp.shape
    tt = 256
    HK = PEER_HEADS * PEER_TOPK
    return pl.pallas_call(
        _topk_kernel,
        grid=(T // tt,),
        in_specs=[pl.BlockSpec((tt, N), lambda i: (i, 0)),
                  pl.BlockSpec(keys_bf.shape, lambda i: (0, 0, 0, 0))],
        out_specs=[pl.BlockSpec((HK, tt), lambda i: (0, i)),
                   pl.BlockSpec((HK, tt), lambda i: (0, i))],
        out_shape=[jax.ShapeDtypeStruct((HK, T), jnp.int32),
                   jax.ShapeDtypeStruct((HK, T), F32)],
        compiler_params=_params(("parallel",), 32),
        name="topk",
    )(qp, keys_bf)


_PEER_TB = 128
_PEER_SUB = 8
_SUBLANES = 8
_PEER_CHUNKS = 16
_PEER_SC_U_CHUNKS = 3
_SC_LANES = 16
_SC_ROWS = 16


def _peer_u_kernel(idx_hbm, gate_ref, x1_ref, gn_ref, sc_ref, sh_ref, u_hbm, w_ref,
                   idx_smem, ub0, ub1, h_scr, sem_i, sem_u, *, step0):
    HK = PEER_HEADS * PEER_TOPK
    TB, SUB = _PEER_TB, _PEER_SUB
    R = SUB * HK
    N = TB * HK
    nsub = TB // SUB
    D = x1_ref.shape[1]
    nchunk = D // LANE
    tiles = HK // _SUBLANES
    i = pl.program_id(0)
    n = pl.num_programs(0)
    cur = lax.rem(i, 2) * N
    nxt = N - cur
    more = i + 1 < n
    ubufs = (ub0, ub1)

    def idx_copy(step, base):
        return pltpu.make_async_copy(idx_hbm.at[pl.ds((step0 + step) * N, N)],
                                     idx_smem.at[pl.ds(base, N)], sem_i)

    def issue_token(base, t, slot):
        for k in range(HK):
            e = idx_smem[base + t * HK + k]
            rt, s = t * tiles + k // _SUBLANES, k % _SUBLANES
            pltpu.make_async_copy(u_hbm.at[e], ubufs[slot].at[rt, :, s, :],
                                  sem_u.at[slot]).start(priority=k % 2)

    def wait(slot):
        pltpu.make_async_copy(ubufs[slot], ubufs[slot], sem_u.at[slot]).wait()

    @pl.when(i == 0)
    def _():
        first = idx_copy(0, 0)
        first.start()
        first.wait()
        for t in range(SUB):
            issue_token(0, t, 0)

    @pl.when(more)
    def _():
        idx_copy(i + 1, nxt).start()

    h_scr[...] = _rms(x1_ref[...], gn_ref[...]) * (1.0 + sc_ref[0]) + sh_ref[0]
    lane = lax.broadcasted_iota(jnp.int32, (HK, TB), 1)

    def compute_token(j, t, slot, wacc):
        ub = ubufs[slot]
        tok = j * SUB + t
        xt = h_scr[pl.ds(tok, 1), :]
        rows = slice(t * tiles, (t + 1) * tiles)
        part = ub[rows, 0].reshape(HK, LANE) * xt[:, 0:LANE]
        for c in range(1, nchunk):
            part = part + ub[rows, c].reshape(HK, LANE) * xt[:, c * LANE:(c + 1) * LANE]
        a = jnp.sum(part, axis=1, keepdims=True)
        hit = lane == tok
        g = jnp.sum(jnp.where(hit, gate_ref[...], 0.0), axis=1, keepdims=True)
        wgt = g * (0.5 * a * (1.0 + lax.erf(a * (2.0 ** -0.5))))
        return jnp.where(hit, wgt, wacc)

    def half(j, slot, next_base, wacc):
        wait(slot)
        for t in range(SUB):
            issue_token(next_base, t, 1 - slot)
            wacc = compute_token(j, t, slot, wacc)
        return wacc

    def pair(jj, wacc):
        j0 = 2 * jj
        wacc = half(j0, 0, cur + (j0 + 1) * R, wacc)
        last = jj == nsub // 2 - 1

        @pl.when(jnp.logical_and(last, more))
        def _():
            idx_copy(i + 1, nxt).wait()

        after = jnp.where(more, nxt, cur)
        return half(j0 + 1, 1, jnp.where(last, after, cur + (j0 + 2) * R), wacc)

    wacc = lax.fori_loop(0, nsub // 2, pair, jnp.zeros((HK, TB), F32))

    @pl.when(jnp.logical_not(more))
    def _():
        wait(0)

    w_ref[:, :HK] = jnp.zeros((TB, HK), F32)
    w_ref[:, HK:] = wacc.T


def _peer_u(idx_flat, gate_t, x1, g_norm, mod3, u3, S, step0, nsteps):
    T, D = x1.shape
    HK = PEER_HEADS * PEER_TOPK
    TB, SUB = _PEER_TB, _PEER_SUB
    per_b = S // TB
    modrow = lambda k: (lambda i: (((step0 + i) // per_b) * 6 + k, 0, 0))
    gbuf = pltpu.VMEM((SUB * HK // _SUBLANES, D // LANE, _SUBLANES, LANE), F32)
    return pl.pallas_call(
        functools.partial(_peer_u_kernel, step0=step0),
        grid=(nsteps,),
        in_specs=[pl.BlockSpec(memory_space=pl.ANY),
                  pl.BlockSpec((HK, TB), lambda i: (0, step0 + i)),
                  pl.BlockSpec((TB, D), lambda i: (step0 + i, 0)),
                  pl.BlockSpec((1, D), lambda i: (0, 0)),
                  pl.BlockSpec((1, 1, D), modrow(4)),
                  pl.BlockSpec((1, 1, D), modrow(3)),
                  pl.BlockSpec(memory_space=pl.ANY)],
        out_specs=pl.BlockSpec((TB, 2 * HK), lambda i: (i, 0)),
        out_shape=jax.ShapeDtypeStruct((nsteps * TB, 2 * HK), F32),
        scratch_shapes=[pltpu.SMEM((2 * TB * HK,), jnp.int32),
                        gbuf, gbuf,
                        pltpu.VMEM((TB, D), F32),
                        pltpu.SemaphoreType.DMA,
                        pltpu.SemaphoreType.DMA((2,))],
        compiler_params=_params(("arbitrary",), 40),
        name="peer_u",
    )(idx_flat, gate_t, x1, g_norm, mod3, mod3, u3)


def _sc_peer_u(u_tab, idx_flat, h, tok_base, ntok):
    E, nblk, _ = u_tab.shape
    D = nblk * LANE
    HK = PEER_HEADS * PEER_TOPK
    info = plsc.get_sparse_core_info()
    nw = info.num_cores * info.num_subcores
    tpw = ntok // nw
    CH = _SC_ROWS
    nch = HK // CH
    nsl = D // _SC_LANES
    GRP = 8
    mesh = plsc.VectorSubcoreMesh(core_axis_name="c", subcore_axis_name="s")

    @functools.partial(
        pl.kernel, mesh=mesh, out_type=jax.ShapeDtypeStruct((ntok, HK), F32),
        scratch_types=[pltpu.VMEM((tpw * HK,), jnp.int32), pltpu.VMEM((D,), F32), pltpu.VMEM((HK,), F32),
                       pltpu.VMEM((CH, nblk, LANE), F32), pltpu.VMEM((CH, nblk, LANE), F32),
                       pltpu.SemaphoreType.DMA, pltpu.SemaphoreType.DMA],
        compiler_params=pltpu.CompilerParams(needs_layout_passes=False),
        name="sc_peer_u",
    )
    def k(tab_hbm, idx_hbm, h_hbm, a_hbm, idx_v, x_v, a_v, buf0, buf1, g0, g1):
        wid = lax.axis_index("s") * info.num_cores + lax.axis_index("c")
        tok0 = tok_base + wid * tpw
        pltpu.sync_copy(idx_hbm.at[pl.ds(tok0 * HK, tpw * HK)], idx_v)
        bufs, gs = (buf0, buf1), (g0, g1)
        lane = lax.broadcasted_iota(jnp.int32, (_SC_LANES,), 0)

        def gather(g, b):
            return pltpu.make_async_copy(tab_hbm.at[idx_v.at[pl.ds(g * CH, CH)]], bufs[b], gs[b])

        gather(0, 0).start()

        @pl.loop(0, tpw)
        def _(t):
            pltpu.sync_copy(h_hbm.at[tok0 + t], x_v)
            for ch in range(nch):
                b = ch % 2
                g = t * nch + ch
                gather(g, b).wait()

                @pl.when(g + 1 < tpw * nch)
                def _():
                    gather(g + 1, 1 - b).start()

                a_vec = jnp.zeros((_SC_LANES,), F32)
                for grp in range(CH // GRP):
                    def body(c, accs):
                        blk = c // (LANE // _SC_LANES)
                        lo = pl.multiple_of(c * _SC_LANES - blk * LANE, _SC_LANES)
                        xv = x_v[pl.ds(pl.multiple_of(c * _SC_LANES, _SC_LANES), _SC_LANES)]
                        return tuple(accs[r] + bufs[b][grp * GRP + r, blk, pl.ds(lo, _SC_LANES)] * xv
                                     for r in range(GRP))

                    accs = lax.fori_loop(0, nsl, body,
                                         tuple(jnp.zeros((_SC_LANES,), F32) for _ in range(GRP)))
                    for r in range(GRP):
                        a_vec = jnp.where(lane == grp * GRP + r, jnp.sum(accs[r]), a_vec)
                a_v[pl.ds(ch * CH, CH)] = a_vec
            pltpu.sync_copy(a_v, a_hbm.at[tok0 - tok_base + t])

    return k(u_tab, idx_flat, h)


def _peer_gate_kernel(a_ref, gate_ref, w_ref):
    HK = a_ref.shape[1]
    a = a_ref[...]
    w_ref[:, :HK] = jnp.zeros(a.shape, F32)
    w_ref[:, HK:] = gate_ref[...].T * (0.5 * a * (1.0 + lax.erf(a * (2.0 ** -0.5))))


def _peer_gate(a, gate_t, step0):
    Tc, HK = a.shape
    TB = _PEER_TB
    return pl.pallas_call(
        _peer_gate_kernel,
        grid=(Tc // TB,),
        in_specs=[pl.BlockSpec((TB, HK), lambda i: (i, 0)),
                  pl.BlockSpec((HK, TB), lambda i: (0, step0 + i))],
        out_specs=pl.BlockSpec((TB, 2 * HK), lambda i: (i, 0)),
        out_shape=jax.ShapeDtypeStruct((Tc, 2 * HK), F32),
        compiler_params=_params(("parallel",), 32),
        name="peer_gate",
    )(a, gate_t)


def _sc_peer_v(v_tab, idx_flat, wgt, tok_base):
    E, nblk, _ = v_tab.shape
    D = nblk * LANE
    Tc, HK = wgt.shape[0], wgt.shape[1] // 2
    info = plsc.get_sparse_core_info()
    nw = info.num_cores * info.num_subcores
    tpw = Tc // nw
    CH = _SC_ROWS
    nch = HK // CH
    nsl = D // _SC_LANES
    mesh = plsc.VectorSubcoreMesh(core_axis_name="c", subcore_axis_name="s")

    @functools.partial(
        pl.kernel, mesh=mesh, out_type=jax.ShapeDtypeStruct((Tc, D), F32),
        scratch_types=[pltpu.VMEM((tpw * HK,), jnp.int32), pltpu.VMEM((2 * HK,), F32),
                       pltpu.VMEM((D,), F32),
                       pltpu.VMEM((CH, nblk, LANE), F32), pltpu.VMEM((CH, nblk, LANE), F32),
                       pltpu.SemaphoreType.DMA, pltpu.SemaphoreType.DMA],
        compiler_params=pltpu.CompilerParams(needs_layout_passes=False),
        name="sc_peer_v",
    )
    def k(tab_hbm, idx_hbm, w_hbm, o_hbm, idx_v, w_v, o_v, buf0, buf1, g0, g1):
        wid = lax.axis_index("s") * info.num_cores + lax.axis_index("c")
        tok0 = wid * tpw
        pltpu.sync_copy(idx_hbm.at[pl.ds((tok_base + tok0) * HK, tpw * HK)], idx_v)
        bufs, gs = (buf0, buf1), (g0, g1)

        def gather(g, b):
            return pltpu.make_async_copy(tab_hbm.at[idx_v.at[pl.ds(g * CH, CH)]], bufs[b], gs[b])

        gather(0, 0).start()

        @pl.loop(0, tpw)
        def _(t):
            pltpu.sync_copy(w_hbm.at[tok0 + t], w_v)

            @pl.loop(0, nsl)
            def _(c):
                o_v[pl.ds(pl.multiple_of(c * _SC_LANES, _SC_LANES), _SC_LANES)] = jnp.zeros((_SC_LANES,), F32)

            for ch in range(nch):
                b = ch % 2
                g = t * nch + ch
                gather(g, b).wait()

                @pl.when(g + 1 < tpw * nch)
                def _():
                    gather(g + 1, 1 - b).start()

                ws = [plsc.load_gather(w_v, [jnp.full((_SC_LANES,), HK + ch * CH + r, jnp.int32)])
                      for r in range(CH)]

                @pl.loop(0, nsl, step=2)
                def _(c):
                    for half in range(2):
                        off = pl.multiple_of((c + half) * _SC_LANES, _SC_LANES)
                        blk = (c + half) // (LANE // _SC_LANES)
                        lo = pl.multiple_of(off - blk * LANE, _SC_LANES)
                        parts = [ws[r] * bufs[b][r, blk, pl.ds(lo, _SC_LANES)] for r in range(CH)]
                        while len(parts) > 1:
                            parts = [parts[p] + parts[p + 1] for p in range(0, len(parts), 2)]
                        o_v[pl.ds(off, _SC_LANES)] = o_v[pl.ds(off, _SC_LANES)] + parts[0]

            pltpu.sync_copy(o_v, o_hbm.at[tok0 + t])

    return k(v_tab, idx_flat, wgt)


def _peer_fin_kernel(x_ref, p_ref, ga_ref, gf_ref, o_ref):
    o_ref[...] = _rms(x_ref[...] + ga_ref[0] * p_ref[...], gf_ref[...])


def _peer_fin(x1, po, mod3, g_final, S):
    T, D = x1.shape
    tm = 512
    per_b = S // tm
    return pl.pallas_call(
        _peer_fin_kernel,
        grid=(T // tm,),
        in_specs=[pl.BlockSpec((tm, D), lambda i: (i, 0)),
                  pl.BlockSpec((tm, D), lambda i: (i, 0)),
                  pl.BlockSpec((1, 1, D), lambda i: ((i // per_b) * 6 + 5, 0, 0)),
                  pl.BlockSpec((1, D), lambda i: (0, 0))],
        out_specs=pl.BlockSpec((tm, D), lambda i: (i, 0)),
        out_shape=jax.ShapeDtypeStruct((T, D), F32),
        compiler_params=_params(("parallel",), 40),
        name="peer_fin",
    )(x1, po, mod3, g_final)


def _pad_gate(w, lo):
    rank = w.shape[0]
    wh = w.reshape(rank, GLA_HEADS, GLA_DK).transpose(1, 0, 2)
    return jnp.zeros((GLA_HEADS, LANE, GLA_DK), F32).at[:, lo:lo + rank, :].set(wh)


def kernel(x, c, positions, w_ada, b_ada, g_norm_mix, w_in, w_gate_f, b_gate_f, w_gate_b, b_gate_b,
           g_gla_out, w_out, g_norm_ffn, w_peer_q, peer_sub_keys, peer_u, peer_v, g_final):
    B, S, D = x.shape
    T = B * S
    depth = w_ada.shape[0]
    assert depth == 1, "the final norm is fused into the last PEER call; one layer only"
    xt = x.reshape(T, D)
    cs, sn = _rope_tables(positions.reshape(T, 1))
    gz0 = 2 * GLA_HEADS * GLA_DK + 2 * GLA_HEADS * GLA_DV
    gz1 = gz0 + 2 * GLA_GATE_RANK
    for l in range(depth):
        mod3 = _ada(c, w_ada[l], b_ada[l]).reshape(B * 6, 1, D)
        w_main = jnp.concatenate([w_in[l][:, :gz0], w_in[l][:, gz1:]], axis=1).astype(BF16)
        w_z = jnp.pad(w_in[l][:, gz0:gz1], ((0, 0), (0, LANE - (gz1 - gz0)))).astype(BF16)
        proj, gz = _inproj(xt, g_norm_mix[l].reshape(1, D), mod3, w_main, w_z, S)
        o_gla = _gla(proj, gz,
                     _pad_gate(w_gate_f[l], 0), b_gate_f[l].reshape(GLA_HEADS, 1, GLA_DK),
                     _pad_gate(w_gate_b[l], GLA_GATE_RANK), b_gate_b[l].reshape(GLA_HEADS, 1, GLA_DK),
                     g_gla_out[l].reshape(GLA_HEADS, 1, GLA_DV), B, S)
        o_dil = _dil(proj, cs, sn, B, S)
        x1 = _outproj(o_gla, o_dil, w_out[l].astype(BF16), xt, mod3, S)
        qp, h2 = _pq(x1, g_norm_ffn[l].reshape(1, D), mod3, w_peer_q[l].astype(BF16), S)
        idx_t, gate_t = _topk(qp, peer_sub_keys[l].astype(BF16))
        idx_flat = idx_t.T.reshape(-1)
        E = peer_u.shape[1]
        u3 = peer_u[l].reshape(E, D // LANE, LANE)
        v3 = peer_v[l].reshape(E, D // LANE, LANE)
        steps = T // _PEER_TB // _PEER_CHUNKS
        tc_chunks = _PEER_CHUNKS - _PEER_SC_U_CHUNKS
        chunk_tok = steps * _PEER_TB
        a_sc = _sc_peer_u(u3, idx_flat, h2, tc_chunks * chunk_tok, _PEER_SC_U_CHUNKS * chunk_tok)
        outs = []
        for ck in range(_PEER_CHUNKS):
            if ck < tc_chunks:
                wgt = _peer_u(idx_flat, gate_t, x1, g_norm_ffn[l].reshape(1, D), mod3, u3, S,
                              ck * steps, steps)
            else:
                lo = (ck - tc_chunks) * chunk_tok
                wgt = _peer_gate(a_sc[lo:lo + chunk_tok], gate_t, ck * steps)
            outs.append(_sc_peer_v(v3, idx_flat, wgt, ck * chunk_tok))
        xt = _peer_fin(x1, jnp.concatenate(outs, axis=0), mod3, g_final.reshape(1, D), S)
    return xt.reshape(B, S, D)
```

```python
import functools
import math

import numpy as np
import jax
import jax.numpy as jnp
from jax import lax
from jax.experimental import pallas as pl
from jax.experimental.pallas import tpu as pltpu
from jax.experimental.pallas import tpu_sc as plsc

F32 = jnp.float32
BF16 = jnp.bfloat16
HIGHEST = lax.Precision.HIGHEST

NORM_EPS = 1e-6
GLA_HEADS = 4
GLA_DK = 128
GLA_DV = 256
GLA_GATE_RANK = 16
GLA_TAU = 16.0
GLA_CHUNK = 64
DIL_HD = 128
DIL_HEADS = 8
DIL_PATTERNS = ((128, 1), (512, 4), (2048, 16))
ROPE_THETA = 10000.0
NEG_INF = -1e30
PEER_HEADS = 8
PEER_NKEYS = 128
PEER_TOPK = 16
PEER_HALF = 128

LANE = 128
MIB = 1024 * 1024

_COL_GQ, _COL_GK, _COL_GV, _COL_GR, _COL_DQ, _COL_DK, _COL_DV = 0, 4, 8, 16, 24, 32, 40
_PROJ_W = 48 * LANE

_NT = (((1,), (1,)), ((), ()))
_TN = (((0,), (0,)), ((), ()))


def _params(sem, vmem_mib):
    return pltpu.CompilerParams(dimension_semantics=sem, vmem_limit_bytes=vmem_mib * MIB)


def _rms(x, g):
    return x * lax.rsqrt(jnp.mean(x * x, axis=-1, keepdims=True) + NORM_EPS) * g


def _silu(x):
    return x / (1.0 + jnp.exp(-x))


def _ada_kernel(c_ref, w_ref, b_ref, o_ref):
    s = _silu(c_ref[...]).astype(BF16)
    o_ref[...] = jnp.dot(s, w_ref[...].astype(BF16), preferred_element_type=F32) + b_ref[...]


def _ada(c, w, b):
    B, D = c.shape
    N = w.shape[1]
    tn = 1024
    cp = jnp.zeros((8, D), F32).at[:B].set(c)
    out = pl.pallas_call(
        _ada_kernel,
        grid=(N // tn,),
        in_specs=[pl.BlockSpec((8, D), lambda j: (0, 0)),
                  pl.BlockSpec((D, tn), lambda j: (0, j)),
                  pl.BlockSpec((1, tn), lambda j: (0, j))],
        out_specs=pl.BlockSpec((8, tn), lambda j: (0, j)),
        out_shape=jax.ShapeDtypeStruct((8, N), F32),
        compiler_params=_params(("parallel",), 40),
        name="ada",
    )(cp, w, b.reshape(1, N))
    return out[:B]


def _inproj_kernel(x_ref, g_ref, sc_ref, sh_ref, w_ref, wz_ref, o_ref, z_ref, h_scr):
    @pl.when(pl.program_id(1) == 0)
    def _():
        h = _rms(x_ref[...], g_ref[...]) * (1.0 + sc_ref[0]) + sh_ref[0]
        hb = h.astype(BF16)
        h_scr[...] = hb
        z_ref[...] = jnp.dot(hb, wz_ref[...], preferred_element_type=F32)

    o_ref[...] = jnp.dot(h_scr[...], w_ref[...], preferred_element_type=F32)


def _inproj(x2, g, mod3, w_main, w_z, S):
    T, D = x2.shape
    tm, tn = 1024, 768
    per_b = S // tm
    return pl.pallas_call(
        _inproj_kernel,
        grid=(T // tm, _PROJ_W // tn),
        in_specs=[pl.BlockSpec((tm, D), lambda i, j: (i, 0)),
                  pl.BlockSpec((1, D), lambda i, j: (0, 0)),
                  pl.BlockSpec((1, 1, D), lambda i, j: ((i // per_b) * 6 + 1, 0, 0)),
                  pl.BlockSpec((1, 1, D), lambda i, j: ((i // per_b) * 6 + 0, 0, 0)),
                  pl.BlockSpec((D, tn), lambda i, j: (0, j)),
                  pl.BlockSpec((D, LANE), lambda i, j: (0, 0))],
        out_specs=[pl.BlockSpec((tm, tn), lambda i, j: (i, j)),
                   pl.BlockSpec((tm, LANE), lambda i, j: (i, 0))],
        out_shape=[jax.ShapeDtypeStruct((T, _PROJ_W), F32),
                   jax.ShapeDtypeStruct((T, LANE), F32)],
        scratch_shapes=[pltpu.VMEM((tm, D), BF16)],
        compiler_params=_params(("parallel", "arbitrary"), 48),
        name="inproj",
    )(x2, g, mod3, mod3, w_main, w_z)


def _rope_kernel(pos_ref, f_ref, sg_ref, cs_ref, sn_ref):
    ang = pos_ref[...].astype(F32) * f_ref[...]
    cs_ref[...] = jnp.cos(ang)
    sn_ref[...] = jnp.sin(ang) * sg_ref[...]


def _rope_tables(pos_col):
    T = pos_col.shape[0]
    half = DIL_HD // 2
    inv = jnp.power(ROPE_THETA, -jnp.arange(half, dtype=F32) * 2.0 / DIL_HD)
    freq = jnp.concatenate([inv, inv]).reshape(1, DIL_HD)
    sign = jnp.concatenate([-jnp.ones((half,), F32), jnp.ones((half,), F32)]).reshape(1, DIL_HD)
    tm = 1024
    return pl.pallas_call(
        _rope_kernel,
        grid=(T // tm,),
        in_specs=[pl.BlockSpec((tm, 1), lambda i: (i, 0)),
                  pl.BlockSpec((1, DIL_HD), lambda i: (0, 0)),
                  pl.BlockSpec((1, DIL_HD), lambda i: (0, 0))],
        out_specs=[pl.BlockSpec((tm, DIL_HD), lambda i: (i, 0)),
                   pl.BlockSpec((tm, DIL_HD), lambda i: (i, 0))],
        out_shape=[jax.ShapeDtypeStruct((T, DIL_HD), F32)] * 2,
        compiler_params=_params(("parallel",), 32),
        name="rope",
    )(pos_col, freq, sign)


def _gla_kernel(q_ref, k_ref, v_ref, r_ref, z_ref, wgf_ref, bgf_ref, wgb_ref, bgb_ref, g_ref,
                o_ref, laf_scr, lab_scr, of_scr):
    S = q_ref.shape[0]
    C = GLA_CHUNK
    n = S // C
    scale = GLA_DK ** -0.5
    row = lax.broadcasted_iota(jnp.int32, (C, C), 0)
    col = lax.broadcasted_iota(jnp.int32, (C, C), 1)

    def log_gate(w_ref, b_ref):
        zz = jnp.dot(z_ref[...], w_ref[0], precision=HIGHEST, preferred_element_type=F32) + b_ref[0]
        return (jnp.minimum(zz, 0.0) - jnp.log(1.0 + jnp.exp(-jnp.abs(zz)))) * (1.0 / GLA_TAU)

    def chunk(fwd, c, st_t, la_scr, o_scr):
        keep = (col <= row) if fwd else (col >= row)
        sl = pl.ds(pl.multiple_of(c * C, C), C)
        cum = jnp.dot(keep.astype(F32), la_scr[sl, :], precision=HIGHEST, preferred_element_type=F32)
        tot = cum[C - 1:C, :] if fwd else cum[0:1, :]
        kk = k_ref[sl, :]
        qd = (q_ref[sl, :] * scale * jnp.exp(cum)).astype(BF16)
        ki = (kk * jnp.exp(-cum)).astype(BF16)
        kte = (kk * jnp.exp(tot - cum)).astype(BF16)
        vb = v_ref[sl, :].astype(BF16)
        attn = lax.dot_general(qd, ki, _NT, preferred_element_type=F32)
        attn = jnp.where(keep, attn, 0.0).astype(BF16)
        o = jnp.dot(attn, vb, preferred_element_type=F32)
        o_scr[sl, :] = o + lax.dot_general(qd, st_t.astype(BF16), _NT, preferred_element_type=F32)
        upd = lax.dot_general(vb, kte, _TN, preferred_element_type=F32)
        return st_t * jnp.exp(tot) + upd

    laf_scr[...] = log_gate(wgf_ref, bgf_ref)
    lab_scr[...] = log_gate(wgb_ref, bgb_ref)

    def body(i, states):
        return (chunk(True, i, states[0], laf_scr, of_scr),
                chunk(False, n - 1 - i, states[1], lab_scr, o_ref))

    zero = jnp.zeros((GLA_DV, GLA_DK), F32)
    lax.fori_loop(0, n, body, (zero, zero), unroll=2)
    y = _rms(of_scr[...] + o_ref[...], g_ref[0])
    o_ref[...] = y * _silu(r_ref[...])


def _gla(proj, gz, wgf, bgf, wgb, bgb, g_out, B, S):
    T = proj.shape[0]
    H = GLA_HEADS
    return pl.pallas_call(
        _gla_kernel,
        grid=(B, H),
        in_specs=[pl.BlockSpec((S, GLA_DK), lambda b, h: (b, _COL_GQ + h)),
                  pl.BlockSpec((S, GLA_DK), lambda b, h: (b, _COL_GK + h)),
                  pl.BlockSpec((S, GLA_DV), lambda b, h: (b, _COL_GV // 2 + h)),
                  pl.BlockSpec((S, GLA_DV), lambda b, h: (b, _COL_GR // 2 + h)),
                  pl.BlockSpec((S, LANE), lambda b, h: (b, 0)),
                  pl.BlockSpec((1, LANE, GLA_DK), lambda b, h: (h, 0, 0)),
                  pl.BlockSpec((1, 1, GLA_DK), lambda b, h: (h, 0, 0)),
                  pl.BlockSpec((1, LANE, GLA_DK), lambda b, h: (h, 0, 0)),
                  pl.BlockSpec((1, 1, GLA_DK), lambda b, h: (h, 0, 0)),
                  pl.BlockSpec((1, 1, GLA_DV), lambda b, h: (h, 0, 0))],
        out_specs=pl.BlockSpec((S, GLA_DV), lambda b, h: (b, h)),
        out_shape=jax.ShapeDtypeStruct((T, GLA_HEADS * GLA_DV), F32),
        scratch_shapes=[pltpu.VMEM((S, GLA_DK), F32), pltpu.VMEM((S, GLA_DK), F32),
                        pltpu.VMEM((S, GLA_DV), F32)],
        compiler_params=_params(("parallel", "parallel"), 56),
        name="gla",
    )(proj, proj, proj, proj, gz, wgf, bgf, wgb, bgb, g_out)


_DIL_QB = 256
_DIL_REACH = max(w // 2 for w, _ in DIL_PATTERNS) // _DIL_QB


def _dil_bias():
    qb = _DIL_QB
    d = np.arange(-_DIL_REACH, _DIL_REACH + 1)[:, None, None] * qb
    delta = d + np.arange(qb)[None, None, :] - np.arange(qb)[None, :, None]
    mult = np.zeros(delta.shape, np.float64)
    for window, dilation in DIL_PATTERNS:
        half = window // (2 * dilation)
        mult += ((delta % dilation) == 0) & (np.abs(delta) <= half * dilation)
    with np.errstate(divide="ignore"):
        bias = np.where(mult > 0, np.log(np.maximum(mult, 1.0)), NEG_INF)
    return jnp.asarray(bias, F32)


def _dil_kernel(q_ref, k_ref, v_ref, cs_ref, sn_ref, bias_ref, o_ref, kr_scr, vb_scr):
    qi = pl.program_id(2)
    nq = pl.num_programs(2)
    QB = _DIL_QB
    half = DIL_HD // 2

    @pl.when(qi == 0)
    def _():
        k = k_ref[...]
        kr_scr[...] = (k * cs_ref[...] + pltpu.roll(k, half, 1) * sn_ref[...]).astype(BF16)
        vb_scr[...] = v_ref[...].astype(BF16)

    r0 = pl.multiple_of(qi * QB, QB)
    q = q_ref[...]
    qr = q * cs_ref[pl.ds(r0, QB), :] + pltpu.roll(q, half, 1) * sn_ref[pl.ds(r0, QB), :]
    qb = (qr * (DIL_HD ** -0.5)).astype(BF16)

    m = jnp.full((QB, 1), NEG_INF, F32)
    l = jnp.zeros((QB, 1), F32)
    acc = jnp.zeros((QB, DIL_HD), F32)
    order = [0] + [s * d for d in range(1, _DIL_REACH + 1) for s in (-1, 1)]
    for d in order:
        kb = qi + d
        valid = jnp.logical_and(kb >= 0, kb < nq)
        k0 = pl.multiple_of(jnp.clip(kb, 0, nq - 1) * QB, QB)
        s = lax.dot_general(qb, kr_scr[pl.ds(k0, QB), :], _NT, preferred_element_type=F32)
        s = jnp.where(valid, s + bias_ref[d + _DIL_REACH], NEG_INF)
        m_new = jnp.maximum(m, jnp.max(s, axis=-1, keepdims=True))
        alpha = jnp.exp(m - m_new)
        p = jnp.exp(s - m_new)
        l = alpha * l + jnp.sum(p, axis=-1, keepdims=True)
        acc = alpha * acc + jnp.dot(p.astype(BF16), vb_scr[pl.ds(k0, QB), :],
                                    preferred_element_type=F32)
        m = m_new
    o_ref[...] = acc / l


def _dil(proj, cs, sn, B, S):
    T = proj.shape[0]
    QB = _DIL_QB
    nq = S // QB
    nb = 2 * _DIL_REACH + 1
    return pl.pallas_call(
        _dil_kernel,
        grid=(B, DIL_HEADS, nq),
        in_specs=[pl.BlockSpec((QB, DIL_HD), lambda b, h, i: (b * nq + i, _COL_DQ + h)),
                  pl.BlockSpec((S, DIL_HD), lambda b, h, i: (b, _COL_DK + h)),
                  pl.BlockSpec((S, DIL_HD), lambda b, h, i: (b, _COL_DV + h)),
                  pl.BlockSpec((S, DIL_HD), lambda b, h, i: (b, 0)),
                  pl.BlockSpec((S, DIL_HD), lambda b, h, i: (b, 0)),
                  pl.BlockSpec((nb, QB, QB), lambda b, h, i: (0, 0, 0))],
        out_specs=pl.BlockSpec((QB, DIL_HD), lambda b, h, i: (b * nq + i, h)),
        out_shape=jax.ShapeDtypeStruct((T, DIL_HEADS * DIL_HD), F32),
        scratch_shapes=[pltpu.VMEM((S, DIL_HD), BF16), pltpu.VMEM((S, DIL_HD), BF16)],
        compiler_params=_params(("parallel", "parallel", "arbitrary"), 48),
        name="dil",
    )(proj, proj, proj, cs, sn, _dil_bias())


def _outproj_kernel(og_ref, od_ref, w_ref, x_ref, ga_ref, o_ref):
    kg = og_ref.shape[1]
    mixed = jnp.dot(og_ref[...].astype(BF16), w_ref[:kg, :], preferred_element_type=F32)
    mixed = mixed + jnp.dot(od_ref[...].astype(BF16), w_ref[kg:, :], preferred_element_type=F32)
    o_ref[...] = x_ref[...] + ga_ref[0] * mixed


def _outproj(o_gla, o_dil, w_out, x2, mod3, S):
    T, D = x2.shape
    tm = 512
    per_b = S // tm
    kg, kd = o_gla.shape[1], o_dil.shape[1]
    return pl.pallas_call(
        _outproj_kernel,
        grid=(T // tm,),
        in_specs=[pl.BlockSpec((tm, kg), lambda i: (i, 0)),
                  pl.BlockSpec((tm, kd), lambda i: (i, 0)),
                  pl.BlockSpec((kg + kd, D), lambda i: (0, 0)),
                  pl.BlockSpec((tm, D), lambda i: (i, 0)),
                  pl.BlockSpec((1, 1, D), lambda i: ((i // per_b) * 6 + 2, 0, 0))],
        out_specs=pl.BlockSpec((tm, D), lambda i: (i, 0)),
        out_shape=jax.ShapeDtypeStruct((T, D), F32),
        compiler_params=_params(("parallel",), 48),
        name="outproj",
    )(o_gla, o_dil, w_out, x2, mod3)


def _pq_kernel(x_ref, g_ref, sc_ref, sh_ref, w_ref, o_ref, h_ref):
    h = _rms(x_ref[...], g_ref[...]) * (1.0 + sc_ref[0]) + sh_ref[0]
    h_ref[...] = h
    o_ref[...] = jnp.dot(h.astype(BF16), w_ref[...], preferred_element_type=F32)


def _pq(x1, g, mod3, wq, S):
    T, D = x1.shape
    N = wq.shape[1]
    tm = 512
    per_b = S // tm
    return pl.pallas_call(
        _pq_kernel,
        grid=(T // tm,),
        in_specs=[pl.BlockSpec((tm, D), lambda i: (i, 0)),
                  pl.BlockSpec((1, D), lambda i: (0, 0)),
                  pl.BlockSpec((1, 1, D), lambda i: ((i // per_b) * 6 + 4, 0, 0)),
                  pl.BlockSpec((1, 1, D), lambda i: ((i // per_b) * 6 + 3, 0, 0)),
                  pl.BlockSpec((D, N), lambda i: (0, 0))],
        out_specs=[pl.BlockSpec((tm, N), lambda i: (i, 0)),
                   pl.BlockSpec((tm, D), lambda i: (i, 0))],
        out_shape=[jax.ShapeDtypeStruct((T, N), F32), jax.ShapeDtypeStruct((T, D), F32)],
        compiler_params=_params(("parallel",), 56),
        name="pq",
    )(x1, g, mod3, mod3, wq)


def _top_rows(s, k, payload=None):
    n_rows = s.shape[0]
    rid = lax.broadcasted_iota(jnp.int32, s.shape, 0).astype(F32)
    vals, picks = [], []
    for _ in range(k):
        m = jnp.max(s, axis=0, keepdims=True)
        pos = jnp.min(jnp.where(s == m, rid, float(n_rows)), axis=0, keepdims=True)
        hit = rid == pos
        vals.append(m)
        if payload is None:
            picks.append(pos)
        else:
            picks.append(jnp.sum(jnp.where(hit, payload, 0), axis=0, keepdims=True))
        s = jnp.where(hit, -jnp.inf, s)
    picks = jnp.concatenate(picks, axis=0)
    return jnp.concatenate(vals, axis=0), picks.astype(jnp.int32)


def _staircase(a, b, combine, fill):
    K = a.shape[0]
    half = K // 2
    jrow = lax.broadcasted_iota(jnp.int32, (half, a.shape[1]), 0)
    pieces = [combine(a[0:1], b)]
    for i in range(1, half):
        piece = combine(a[i:i + 1], b[0:half])
        width = K // (i + 1)
        pieces.append(piece if width >= half else jnp.where(jrow < width, piece, fill))
    pieces.append(combine(a[half:K], b[0:1]))
    return jnp.concatenate(pieces, axis=0)


def _topk_kernel(q_ref, keys_ref, idx_ref, gate_ref):
    K = PEER_TOPK
    for h in range(PEER_HEADS):
        tops = []
        for half in range(2):
            c0 = (h * 2 + half) * PEER_HALF
            qh = q_ref[:, c0:c0 + PEER_HALF].astype(BF16)
            sc = lax.dot_general(keys_ref[h, half], qh, _NT, preferred_element_type=F32)
            tops.append(_top_rows(sc, K))
        (s0, i0), (s1, i1) = tops
        cand_s = _staircase(s0, s1, lambda a, b: a + b, -jnp.inf)
        cand_i = _staircase(i0, i1, lambda a, b: a * PEER_NKEYS + b, 0)
        best, idx = _top_rows(cand_s, K, payload=cand_i)
        e = jnp.exp(best - best[0:1])
        gate = e / jnp.sum(e, axis=0, keepdims=True)
        idx_ref[h * K:(h + 1) * K, :] = idx
        gate_ref[h * K:(h + 1) * K, :] = gate


def _topk(qp, keys_bf):
    T, N = qp.shape
    tt = 256
    HK = PEER_HEADS * PEER_TOPK
    return pl.pallas_call(
        _topk_kernel,
        grid=(T // tt,),
        in_specs=[pl.BlockSpec((tt, N), lambda i: (i, 0)),
                  pl.BlockSpec(keys_bf.shape, lambda i: (0, 0, 0, 0))],
        out_specs=[pl.BlockSpec((HK, tt), lambda i: (0, i)),
                   pl.BlockSpec((HK, tt), lambda i: (0, i))],
        out_shape=[jax.ShapeDtypeStruct((HK, T), jnp.int32),
                   jax.ShapeDtypeStruct((HK, T), F32)],
        compiler_params=_params(("parallel",), 32),
        name="topk",
    )(qp, keys_bf)


_PEER_TB = 128
_PEER_SUB = 8
_SUBLANES = 8
_PEER_CHUNKS = 16
_PEER_SC_U_CHUNKS = 1
_SC_LANES = 16
_SC_ROWS = 16


def _peer_u_kernel(idx_hbm, gate_ref, x1_ref, gn_ref, sc_ref, sh_ref, u_hbm, w_ref,
                   idx_smem, ub0, ub1, h_scr, sem_i, sem_u, *, step0):
    HK = PEER_HEADS * PEER_TOPK
    TB, SUB = _PEER_TB, _PEER_SUB
    R = SUB * HK
    N = TB * HK
    nsub = TB // SUB
    D = x1_ref.shape[1]
    nchunk = D // LANE
    tiles = HK // _SUBLANES
    i = pl.program_id(0)
    n = pl.num_programs(0)
    cur = lax.rem(i, 2) * N
    nxt = N - cur
    more = i + 1 < n
    ubufs = (ub0, ub1)

    def idx_copy(step, base):
        return pltpu.make_async_copy(idx_hbm.at[pl.ds((step0 + step) * N, N)],
                                     idx_smem.at[pl.ds(base, N)], sem_i)

    def issue_token(base, t, slot):
        for k in range(HK):
            e = idx_smem[base + t * HK + k]
            rt, s = t * tiles + k // _SUBLANES, k % _SUBLANES
            pltpu.make_async_copy(u_hbm.at[e], ubufs[slot].at[rt, :, s, :],
                                  sem_u.at[slot]).start(priority=k % 2)

    def wait(slot):
        pltpu.make_async_copy(ubufs[slot], ubufs[slot], sem_u.at[slot]).wait()

    @pl.when(i == 0)
    def _():
        first = idx_copy(0, 0)
        first.start()
        first.wait()
        for t in range(SUB):
            issue_token(0, t, 0)

    @pl.when(more)
    def _():
        idx_copy(i + 1, nxt).start()

    h_scr[...] = _rms(x1_ref[...], gn_ref[...]) * (1.0 + sc_ref[0]) + sh_ref[0]
    lane = lax.broadcasted_iota(jnp.int32, (HK, TB), 1)

    def compute_token(j, t, slot, wacc):
        ub = ubufs[slot]
        tok = j * SUB + t
        xt = h_scr[pl.ds(tok, 1), :]
        rows = slice(t * tiles, (t + 1) * tiles)
        part = ub[rows, 0].reshape(HK, LANE) * xt[:, 0:LANE]
        for c in range(1, nchunk):
            part = part + ub[rows, c].reshape(HK, LANE) * xt[:, c * LANE:(c + 1) * LANE]
        a = jnp.sum(part, axis=1, keepdims=True)
        hit = lane == tok
        g = jnp.sum(jnp.where(hit, gate_ref[...], 0.0), axis=1, keepdims=True)
        wgt = g * (0.5 * a * (1.0 + lax.erf(a * (2.0 ** -0.5))))
        return jnp.where(hit, wgt, wacc)

    def half(j, slot, next_base, wacc):
        wait(slot)
        for t in range(SUB):
            issue_token(next_base, t, 1 - slot)
            wacc = compute_token(j, t, slot, wacc)
        return wacc

    def pair(jj, wacc):
        j0 = 2 * jj
        wacc = half(j0, 0, cur + (j0 + 1) * R, wacc)
        last = jj == nsub // 2 - 1

        @pl.when(jnp.logical_and(last, more))
        def _():
            idx_copy(i + 1, nxt).wait()

        after = jnp.where(more, nxt, cur)
        return half(j0 + 1, 1, jnp.where(last, after, cur + (j0 + 2) * R), wacc)

    wacc = lax.fori_loop(0, nsub // 2, pair, jnp.zeros((HK, TB), F32))

    @pl.when(jnp.logical_not(more))
    def _():
        wait(0)

    w_ref[:, :HK] = jnp.zeros((TB, HK), F32)
    w_ref[:, HK:] = wacc.T


def _peer_u(idx_flat, gate_t, x1, g_norm, mod3, u3, S, step0, nsteps):
    T, D = x1.shape
    HK = PEER_HEADS * PEER_TOPK
    TB, SUB = _PEER_TB, _PEER_SUB
    per_b = S // TB
    modrow = lambda k: (lambda i: (((step0 + i) // per_b) * 6 + k, 0, 0))
    gbuf = pltpu.VMEM((SUB * HK // _SUBLANES, D // LANE, _SUBLANES, LANE), F32)
    return pl.pallas_call(
        functools.partial(_peer_u_kernel, step0=step0),
        grid=(nsteps,),
        in_specs=[pl.BlockSpec(memory_space=pl.ANY),
                  pl.BlockSpec((HK, TB), lambda i: (0, step0 + i)),
                  pl.BlockSpec((TB, D), lambda i: (step0 + i, 0)),
                  pl.BlockSpec((1, D), lambda i: (0, 0)),
                  pl.BlockSpec((1, 1, D), modrow(4)),
                  pl.BlockSpec((1, 1, D), modrow(3)),
                  pl.BlockSpec(memory_space=pl.ANY)],
        out_specs=pl.BlockSpec((TB, 2 * HK), lambda i: (i, 0)),
        out_shape=jax.ShapeDtypeStruct((nsteps * TB, 2 * HK), F32),
        scratch_shapes=[pltpu.SMEM((2 * TB * HK,), jnp.int32),
                        gbuf, gbuf,
                        pltpu.VMEM((TB, D), F32),
                        pltpu.SemaphoreType.DMA,
                        pltpu.SemaphoreType.DMA((2,))],
        compiler_params=_params(("arbitrary",), 40),
        name="peer_u",
    )(idx_flat, gate_t, x1, g_norm, mod3, mod3, u3)


def _sc_peer_u(u_tab, idx_flat, h, tok_base, ntok):
    E, nblk, _ = u_tab.shape
    D = nblk * LANE
    HK = PEER_HEADS * PEER_TOPK
    info = plsc.get_sparse_core_info()
    nw = info.num_cores * info.num_subcores
    tpw = ntok // nw
    CH = _SC_ROWS
    nch = HK // CH
    nsl = D // _SC_LANES
    GRP = 8
    mesh = plsc.VectorSubcoreMesh(core_axis_name="c", subcore_axis_name="s")

    @functools.partial(
        pl.kernel, mesh=mesh, out_type=jax.ShapeDtypeStruct((ntok, HK), F32),
        scratch_types=[pltpu.VMEM((tpw * HK,), jnp.int32), pltpu.VMEM((D,), F32), pltpu.VMEM((HK,), F32),
                       pltpu.VMEM((CH, nblk, LANE), F32), pltpu.VMEM((CH, nblk, LANE), F32),
                       pltpu.SemaphoreType.DMA, pltpu.SemaphoreType.DMA],
        compiler_params=pltpu.CompilerParams(needs_layout_passes=False),
        name="sc_peer_u",
    )
    def k(tab_hbm, idx_hbm, h_hbm, a_hbm, idx_v, x_v, a_v, buf0, buf1, g0, g1):
        wid = lax.axis_index("s") * info.num_cores + lax.axis_index("c")
        tok0 = tok_base + wid * tpw
        pltpu.sync_copy(idx_hbm.at[pl.ds(tok0 * HK, tpw * HK)], idx_v)
        bufs, gs = (buf0, buf1), (g0, g1)
        lane = lax.broadcasted_iota(jnp.int32, (_SC_LANES,), 0)

        def gather(g, b):
            return pltpu.make_async_copy(tab_hbm.at[idx_v.at[pl.ds(g * CH, CH)]], bufs[b], gs[b])

        gather(0, 0).start()

        @pl.loop(0, tpw)
        def _(t):
            pltpu.sync_copy(h_hbm.at[tok0 + t], x_v)
            for ch in range(nch):
                b = ch % 2
                g = t * nch + ch
                gather(g, b).wait()

                @pl.when(g + 1 < tpw * nch)
                def _():
                    gather(g + 1, 1 - b).start()

                a_vec = jnp.zeros((_SC_LANES,), F32)
                for grp in range(CH // GRP):
                    def body(c, accs):
                        blk = c // (LANE // _SC_LANES)
                        lo = pl.multiple_of(c * _SC_LANES - blk * LANE, _SC_LANES)
                        xv = x_v[pl.ds(pl.multiple_of(c * _SC_LANES, _SC_LANES), _SC_LANES)]
                        return tuple(accs[r] + bufs[b][grp * GRP + r, blk, pl.ds(lo, _SC_LANES)] * xv
                                     for r in range(GRP))

                    accs = lax.fori_loop(0, nsl, body,
                                         tuple(jnp.zeros((_SC_LANES,), F32) for _ in range(GRP)))
                    for r in range(GRP):
                        a_vec = jnp.where(lane == grp * GRP + r, jnp.sum(accs[r]), a_vec)
                a_v[pl.ds(ch * CH, CH)] = a_vec
            pltpu.sync_copy(a_v, a_hbm.at[tok0 - tok_base + t])

    return k(u_tab, idx_flat, h)


def _peer_gate_kernel(a_ref, gate_ref, w_ref):
    HK = a_ref.shape[1]
    a = a_ref[...]
    w_ref[:, :HK] = jnp.zeros(a.shape, F32)
    w_ref[:, HK:] = gate_ref[...].T * (0.5 * a * (1.0 + lax.erf(a * (2.0 ** -0.5))))


def _peer_gate(a, gate_t, step0):
    Tc, HK = a.shape
    TB = _PEER_TB
    return pl.pallas_call(
        _peer_gate_kernel,
        grid=(Tc // TB,),
        in_specs=[pl.BlockSpec((TB, HK), lambda i: (i, 0)),
                  pl.BlockSpec((HK, TB), lambda i: (0, step0 + i))],
        out_specs=pl.BlockSpec((TB, 2 * HK), lambda i: (i, 0)),
        out_shape=jax.ShapeDtypeStruct((Tc, 2 * HK), F32),
        compiler_params=_params(("parallel",), 32),
        name="peer_gate",
    )(a, gate_t)


def _sc_peer_v(v_tab, idx_flat, wgt, tok_base):
    E, nblk, _ = v_tab.shape
    D = nblk * LANE
    Tc, HK = wgt.shape[0], wgt.shape[1] // 2
    info = plsc.get_sparse_core_info()
    nw = info.num_cores * info.num_subcores
    tpw = Tc // nw
    CH = _SC_ROWS
    nch = HK // CH
    nsl = D // _SC_LANES
    mesh = plsc.VectorSubcoreMesh(core_axis_name="c", subcore_axis_name="s")

    @functools.partial(
        pl.kernel, mesh=mesh, out_type=jax.ShapeDtypeStruct((Tc, D), F32),
        scratch_types=[pltpu.VMEM((tpw * HK,), jnp.int32), pltpu.VMEM((2 * HK,), F32),
                       pltpu.VMEM((D,), F32),
                       pltpu.VMEM((CH, nblk, LANE), F32), pltpu.VMEM((CH, nblk, LANE), F32),
                       pltpu.SemaphoreType.DMA, pltpu.SemaphoreType.DMA],
        compiler_params=pltpu.CompilerParams(needs_layout_passes=False),
        name="sc_peer_v",
    )
    def k(tab_hbm, idx_hbm, w_hbm, o_hbm, idx_v, w_v, o_v, buf0, buf1, g0, g1):
        wid = lax.axis_index("s") * info.num_cores + lax.axis_index("c")
        tok0 = wid * tpw
        pltpu.sync_copy(idx_hbm.at[pl.ds((tok_base + tok0) * HK, tpw * HK)], idx_v)
        bufs, gs = (buf0, buf1), (g0, g1)

        def gather(g, b):
            return pltpu.make_async_copy(tab_hbm.at[idx_v.at[pl.ds(g * CH, CH)]], bufs[b], gs[b])

        gather(0, 0).start()

        @pl.loop(0, tpw)
        def _(t):
            pltpu.sync_copy(w_hbm.at[tok0 + t], w_v)

            @pl.loop(0, nsl)
            def _(c):
                o_v[pl.ds(pl.multiple_of(c * _SC_LANES, _SC_LANES), _SC_LANES)] = jnp.zeros((_SC_LANES,), F32)

            for ch in range(nch):
                b = ch % 2
                g = t * nch + ch
                gather(g, b).wait()

                @pl.when(g + 1 < tpw * nch)
                def _():
                    gather(g + 1, 1 - b).start()

                ws = [plsc.load_gather(w_v, [jnp.full((_SC_LANES,), HK + ch * CH + r, jnp.int32)])
                      for r in range(CH)]

                @pl.loop(0, nsl, step=2)
                def _(c):
                    for half in range(2):
                        off = pl.multiple_of((c + half) * _SC_LANES, _SC_LANES)
                        blk = (c + half) // (LANE // _SC_LANES)
                        lo = pl.multiple_of(off - blk * LANE, _SC_LANES)
                        parts = [ws[r] * bufs[b][r, blk, pl.ds(lo, _SC_LANES)] for r in range(CH)]
                        while len(parts) > 1:
                            parts = [parts[p] + parts[p + 1] for p in range(0, len(parts), 2)]
                        o_v[pl.ds(off, _SC_LANES)] = o_v[pl.ds(off, _SC_LANES)] + parts[0]

            pltpu.sync_copy(o_v, o_hbm.at[tok0 + t])

    return k(v_tab, idx_flat, wgt)


def _peer_fin_kernel(x_ref, p_ref, ga_ref, gf_ref, o_ref):
    o_ref[...] = _rms(x_ref[...] + ga_ref[0] * p_ref[...], gf_ref[...])


def _peer_fin(x1, po, mod3, g_final, S):
    T, D = x1.shape
    tm = 512
    per_b = S // tm
    return pl.pallas_call(
        _peer_fin_kernel,
        grid=(T // tm,),
        in_specs=[pl.BlockSpec((tm, D), lambda i: (i, 0)),
                  pl.BlockSpec((tm, D), lambda i: (i, 0)),
                  pl.BlockSpec((1, 1, D), lambda i: ((i // per_b) * 6 + 5, 0, 0)),
                  pl.BlockSpec((1, D), lambda i: (0, 0))],
        out_specs=pl.BlockSpec((tm, D), lambda i: (i, 0)),
        out_shape=jax.ShapeDtypeStruct((T, D), F32),
        compiler_params=_params(("parallel",), 40),
        name="peer_fin",
    )(x1, po, mod3, g_final)


def _pad_gate(w, lo):
    rank = w.shape[0]
    wh = w.reshape(rank, GLA_HEADS, GLA_DK).transpose(1, 0, 2)
    return jnp.zeros((GLA_HEADS, LANE, GLA_DK), F32).at[:, lo:lo + rank, :].set(wh)


def kernel(x, c, positions, w_ada, b_ada, g_norm_mix, w_in, w_gate_f, b_gate_f, w_gate_b, b_gate_b,
           g_gla_out, w_out, g_norm_ffn, w_peer_q, peer_sub_keys, peer_u, peer_v, g_final):
    B, S, D = x.shape
    T = B * S
    depth = w_ada.shape[0]
    assert depth == 1, "the final norm is fused into the last PEER call; one layer only"
    xt = x.reshape(T, D)
    cs, sn = _rope_tables(positions.reshape(T, 1))
    gz0 = 2 * GLA_HEADS * GLA_DK + 2 * GLA_HEADS * GLA_DV
    gz1 = gz0 + 2 * GLA_GATE_RANK
    for l in range(depth):
        mod3 = _ada(c, w_ada[l], b_ada[l]).reshape(B * 6, 1, D)
        w_main = jnp.concatenate([w_in[l][:, :gz0], w_in[l][:, gz1:]], axis=1).astype(BF16)
        w_z = jnp.pad(w_in[l][:, gz0:gz1], ((0, 0), (0, LANE - (gz1 - gz0)))).astype(BF16)
        proj, gz = _inproj(xt, g_norm_mix[l].reshape(1, D), mod3, w_main, w_z, S)
        o_gla = _gla(proj, gz,
                     _pad_gate(w_gate_f[l], 0), b_gate_f[l].reshape(GLA_HEADS, 1, GLA_DK),
                     _pad_gate(w_gate_b[l], GLA_GATE_RANK), b_gate_b[l].reshape(GLA_HEADS, 1, GLA_DK),
                     g_gla_out[l].reshape(GLA_HEADS, 1, GLA_DV), B, S)
        o_dil = _dil(proj, cs, sn, B, S)
        x1 = _outproj(o_gla, o_dil, w_out[l].astype(BF16), xt, mod3, S)
        qp, h2 = _pq(x1, g_norm_ffn[l].reshape(1, D), mod3, w_peer_q[l].astype(BF16), S)
        idx_t, gate_t = _topk(qp, peer_sub_keys[l].astype(BF16))
        idx_flat = idx_t.T.reshape(-1)
        E = peer_u.shape[1]
        u3 = peer_u[l].reshape(E, D // LANE, LANE)
        v3 = peer_v[l].reshape(E, D // LANE, LANE)
        steps = T // _PEER_TB // _PEER_CHUNKS
        tc_chunks = _PEER_CHUNKS - _PEER_SC_U_CHUNKS
        chunk_tok = steps * _PEER_TB
        a_sc = _sc_peer_u(u3, idx_flat, h2, tc_chunks * chunk_tok, _PEER_SC_U_CHUNKS * chunk_tok)
        outs = []
        for ck in range(_PEER_CHUNKS):
            if ck < tc_chunks:
                wgt = _peer_u(idx_flat, gate_t, x1, g_norm_ffn[l].reshape(1, D), mod3, u3, S,
                              ck * steps, steps)
            else:
                lo = (ck - tc_chunks) * chunk_tok
                wgt = _peer_gate(a_sc[lo:lo + chunk_tok], gate_t, ck * steps)
            outs.append(_sc_peer_v(v3, idx_flat, wgt, ck * chunk_tok))
        xt = _peer_fin(x1, jnp.concatenate(outs, axis=0), mod3, g_final.reshape(1, D), S)
    return xt.reshape(B, S, D)
```

```python
import functools

import jax
import jax.numpy as jnp
from jax import lax
from jax.experimental import pallas as pl
from jax.experimental.pallas import tpu as pltpu
from jax.experimental.pallas import tpu_sc as plsc

F32 = jnp.float32
BF16 = jnp.bfloat16
HIGHEST = lax.Precision.HIGHEST

NORM_EPS = 1e-6
GLA_HEADS = 4
GLA_DK = 128
GLA_DV = 256
GLA_GATE_RANK = 16
GLA_TAU = 16.0
GLA_CHUNK = 64
DIL_HD = 128
DIL_HEADS = 8
DIL_PATTERNS = ((128, 1), (512, 4), (2048, 16))
ROPE_THETA = 10000.0
NEG_INF = -1e30
PEER_HEADS = 8
PEER_NKEYS = 128
PEER_TOPK = 16
PEER_HALF = 128

LANE = 128
MIB = 1024 * 1024

_COL_GQ, _COL_GK, _COL_GV, _COL_GR, _COL_DQ, _COL_DK, _COL_DV = 0, 4, 8, 16, 24, 32, 40
_PROJ_W = 48 * LANE

_NT = (((1,), (1,)), ((), ()))
_TN = (((0,), (0,)), ((), ()))


def _params(sem, vmem_mib):
    return pltpu.CompilerParams(dimension_semantics=sem, vmem_limit_bytes=vmem_mib * MIB)


def _rms(x, g):
    return x * lax.rsqrt(jnp.mean(x * x, axis=-1, keepdims=True) + NORM_EPS) * g


def _silu(x):
    return x / (1.0 + jnp.exp(-x))


def _ada_kernel(c_ref, w_ref, b_ref, o_ref):
    s = _silu(c_ref[...]).astype(BF16)
    o_ref[...] = jnp.dot(s, w_ref[...].astype(BF16), preferred_element_type=F32) + b_ref[...]


def _ada(c, w, b):
    B, D = c.shape
    N = w.shape[1]
    tn = 1024
    cp = jnp.zeros((8, D), F32).at[:B].set(c)
    out = pl.pallas_call(
        _ada_kernel,
        grid=(N // tn,),
        in_specs=[pl.BlockSpec((8, D), lambda j: (0, 0)),
                  pl.BlockSpec((D, tn), lambda j: (0, j)),
                  pl.BlockSpec((1, tn), lambda j: (0, j))],
        out_specs=pl.BlockSpec((8, tn), lambda j: (0, j)),
        out_shape=jax.ShapeDtypeStruct((8, N), F32),
        compiler_params=_params(("parallel",), 40),
        name="ada",
    )(cp, w, b.reshape(1, N))
    return out[:B]


def _inproj_kernel(x_ref, g_ref, sc_ref, sh_ref, w_ref, wz_ref, o_ref, z_ref, h_scr):
    @pl.when(pl.program_id(1) == 0)
    def _():
        h = _rms(x_ref[...], g_ref[...]) * (1.0 + sc_ref[0]) + sh_ref[0]
        hb = h.astype(BF16)
        h_scr[...] = hb
        z_ref[...] = jnp.dot(hb, wz_ref[...], preferred_element_type=F32)

    o_ref[...] = jnp.dot(h_scr[...], w_ref[...], preferred_element_type=F32)


def _inproj(x2, g, mod3, w_main, w_z, S):
    T, D = x2.shape
    tm, tn = 1024, 768
    per_b = S // tm
    return pl.pallas_call(
        _inproj_kernel,
        grid=(T // tm, _PROJ_W // tn),
        in_specs=[pl.BlockSpec((tm, D), lambda i, j: (i, 0)),
                  pl.BlockSpec((1, D), lambda i, j: (0, 0)),
                  pl.BlockSpec((1, 1, D), lambda i, j: ((i // per_b) * 6 + 1, 0, 0)),
                  pl.BlockSpec((1, 1, D), lambda i, j: ((i // per_b) * 6 + 0, 0, 0)),
                  pl.BlockSpec((D, tn), lambda i, j: (0, j)),
                  pl.BlockSpec((D, LANE), lambda i, j: (0, 0))],
        out_specs=[pl.BlockSpec((tm, tn), lambda i, j: (i, j)),
                   pl.BlockSpec((tm, LANE), lambda i, j: (i, 0))],
        out_shape=[jax.ShapeDtypeStruct((T, _PROJ_W), F32),
                   jax.ShapeDtypeStruct((T, LANE), F32)],
        scratch_shapes=[pltpu.VMEM((tm, D), BF16)],
        compiler_params=_params(("parallel", "arbitrary"), 48),
        name="inproj",
    )(x2, g, mod3, mod3, w_main, w_z)


def _rope_kernel(pos_ref, f_ref, sg_ref, cs_ref, sn_ref):
    ang = pos_ref[...].astype(F32) * f_ref[...]
    cs_ref[...] = jnp.cos(ang)
    sn_ref[...] = jnp.sin(ang) * sg_ref[...]


def _rope_tables(pos_col):
    T = pos_col.shape[0]
    half = DIL_HD // 2
    inv = jnp.power(ROPE_THETA, -jnp.arange(half, dtype=F32) * 2.0 / DIL_HD)
    freq = jnp.concatenate([inv, inv]).reshape(1, DIL_HD)
    sign = jnp.concatenate([-jnp.ones((half,), F32), jnp.ones((half,), F32)]).reshape(1, DIL_HD)
    tm = 1024
    return pl.pallas_call(
        _rope_kernel,
        grid=(T // tm,),
        in_specs=[pl.BlockSpec((tm, 1), lambda i: (i, 0)),
                  pl.BlockSpec((1, DIL_HD), lambda i: (0, 0)),
                  pl.BlockSpec((1, DIL_HD), lambda i: (0, 0))],
        out_specs=[pl.BlockSpec((tm, DIL_HD), lambda i: (i, 0)),
                   pl.BlockSpec((tm, DIL_HD), lambda i: (i, 0))],
        out_shape=[jax.ShapeDtypeStruct((T, DIL_HD), F32)] * 2,
        compiler_params=_params(("parallel",), 32),
        name="rope",
    )(pos_col, freq, sign)


def _gla_kernel(q_ref, k_ref, v_ref, r_ref, z_ref, wgf_ref, bgf_ref, wgb_ref, bgb_ref, g_ref,
                o_ref, laf_scr, lab_scr, of_scr):
    S = q_ref.shape[0]
    C = GLA_CHUNK
    n = S // C
    scale = GLA_DK ** -0.5
    row = lax.broadcasted_iota(jnp.int32, (C, C), 0)
    col = lax.broadcasted_iota(jnp.int32, (C, C), 1)

    def log_gate(w_ref, b_ref):
        zz = jnp.dot(z_ref[...], w_ref[0], precision=HIGHEST, preferred_element_type=F32) + b_ref[0]
        return (jnp.minimum(zz, 0.0) - jnp.log(1.0 + jnp.exp(-jnp.abs(zz)))) * (1.0 / GLA_TAU)

    def chunk(fwd, c, st_t, la_scr, o_scr):
        keep = (col <= row) if fwd else (col >= row)
        sl = pl.ds(pl.multiple_of(c * C, C), C)
        cum = jnp.dot(keep.astype(F32), la_scr[sl, :], precision=HIGHEST, preferred_element_type=F32)
        tot = cum[C - 1:C, :] if fwd else cum[0:1, :]
        kk = k_ref[sl, :]
        qd = (q_ref[sl, :] * scale * jnp.exp(cum)).astype(BF16)
        ki = (kk * jnp.exp(-cum)).astype(BF16)
        kte = (kk * jnp.exp(tot - cum)).astype(BF16)
        vb = v_ref[sl, :].astype(BF16)
        attn = lax.dot_general(qd, ki, _NT, preferred_element_type=F32)
        attn = jnp.where(keep, attn, 0.0).astype(BF16)
        o = jnp.dot(attn, vb, preferred_element_type=F32)
        o_scr[sl, :] = o + lax.dot_general(qd, st_t.astype(BF16), _NT, preferred_element_type=F32)
        upd = lax.dot_general(vb, kte, _TN, preferred_element_type=F32)
        return st_t * jnp.exp(tot) + upd

    laf_scr[...] = log_gate(wgf_ref, bgf_ref)
    lab_scr[...] = log_gate(wgb_ref, bgb_ref)

    def body(i, states):
        return (chunk(True, i, states[0], laf_scr, of_scr),
                chunk(False, n - 1 - i, states[1], lab_scr, o_ref))

    zero = jnp.zeros((GLA_DV, GLA_DK), F32)
    lax.fori_loop(0, n, body, (zero, zero), unroll=2)
    y = _rms(of_scr[...] + o_ref[...], g_ref[0])
    o_ref[...] = y * _silu(r_ref[...])


def _gla(proj, gz, wgf, bgf, wgb, bgb, g_out, B, S):
    T = proj.shape[0]
    H = GLA_HEADS
    return pl.pallas_call(
        _gla_kernel,
        grid=(B, H),
        in_specs=[pl.BlockSpec((S, GLA_DK), lambda b, h: (b, _COL_GQ + h)),
                  pl.BlockSpec((S, GLA_DK), lambda b, h: (b, _COL_GK + h)),
                  pl.BlockSpec((S, GLA_DV), lambda b, h: (b, _COL_GV // 2 + h)),
                  pl.BlockSpec((S, GLA_DV), lambda b, h: (b, _COL_GR // 2 + h)),
                  pl.BlockSpec((S, LANE), lambda b, h: (b, 0)),
                  pl.BlockSpec((1, LANE, GLA_DK), lambda b, h: (h, 0, 0)),
                  pl.BlockSpec((1, 1, GLA_DK), lambda b, h: (h, 0, 0)),
                  pl.BlockSpec((1, LANE, GLA_DK), lambda b, h: (h, 0, 0)),
                  pl.BlockSpec((1, 1, GLA_DK), lambda b, h: (h, 0, 0)),
                  pl.BlockSpec((1, 1, GLA_DV), lambda b, h: (h, 0, 0))],
        out_specs=pl.BlockSpec((S, GLA_DV), lambda b, h: (b, h)),
        out_shape=jax.ShapeDtypeStruct((T, GLA_HEADS * GLA_DV), F32),
        scratch_shapes=[pltpu.VMEM((S, GLA_DK), F32), pltpu.VMEM((S, GLA_DK), F32),
                        pltpu.VMEM((S, GLA_DV), F32)],
        compiler_params=_params(("parallel", "parallel"), 56),
        name="gla",
    )(proj, proj, proj, proj, gz, wgf, bgf, wgb, bgb, g_out)


_DIL_QB = 128
_DIL_HALF = 64
_DIL_KW = _DIL_QB + 2 * _DIL_HALF


def _dil_kernel(q_ref, k_ref, v_ref, cs_ref, sn_ref, o_ref, qf_scr, kf_scr, qb_scr, kb_scr, vb_scr, lse_scr):
    S = q_ref.shape[0]
    QB, KW, HALF = _DIL_QB, _DIL_KW, _DIL_HALF
    rot = DIL_HD // 2
    cs, sn = cs_ref[...], sn_ref[...]
    q = q_ref[...]
    qf_scr[...] = (q * cs + pltpu.roll(q, rot, 1) * sn) * (DIL_HD ** -0.5)
    k = k_ref[...]
    kf_scr[...] = k * cs + pltpu.roll(k, rot, 1) * sn
    rel = (lax.broadcasted_iota(jnp.int32, (QB, KW), 1) - lax.broadcasted_iota(jnp.int32, (QB, KW), 0))

    for pi, (window, d) in enumerate(DIL_PATTERNS):
        assert window // (2 * d) == HALF
        M = S // d
        per = M // QB
        for r in range(d):
            rows = pl.ds(r * M, M)
            src = pl.ds(r, M, stride=d) if d > 1 else pl.ds(0, M)
            qb_scr[rows, :] = qf_scr[src, :].astype(BF16)
            kb_scr[rows, :] = kf_scr[src, :].astype(BF16)
            vb_scr[rows, :] = v_ref[src, :].astype(BF16)

        for r in range(d):
            def block(mb, carry, r=r, first=(pi == 0)):
                m0 = mb * QB
                start = jnp.clip(m0 - HALF, 0, M - KW)
                qrow = pl.multiple_of(r * M + m0, QB)
                krow = pl.multiple_of(r * M + start, HALF)
                s = lax.dot_general(qb_scr[pl.ds(qrow, QB), :], kb_scr[pl.ds(krow, KW), :], _NT,
                                    preferred_element_type=F32)
                s = jnp.where(jnp.abs(rel + (start - m0)) <= HALF, s, NEG_INF)
                m = jnp.max(s, axis=-1, keepdims=True)
                p = jnp.exp(s - m)
                l = jnp.sum(p, axis=-1, keepdims=True)
                o = jnp.dot(p.astype(BF16), vb_scr[pl.ds(krow, KW), :], preferred_element_type=F32) / l
                ls = jnp.broadcast_to(m + jnp.log(l), (QB, DIL_HD))
                base = pl.multiple_of(m0 * d, QB)
                nat = pl.ds(base + r, QB, stride=d) if d > 1 else pl.ds(base, QB)
                if first:
                    o_ref[nat, :] = o
                    lse_scr[nat, :] = ls
                else:
                    o0, l0 = o_ref[nat, :], lse_scr[nat, :]
                    mx = jnp.maximum(l0, ls)
                    w0, w1 = jnp.exp(l0 - mx), jnp.exp(ls - mx)
                    den = w0 + w1
                    o_ref[nat, :] = (w0 * o0 + w1 * o) / den
                    lse_scr[nat, :] = mx + jnp.log(den)
                return carry

            lax.fori_loop(0, per, block, 0)


def _dil(proj, cs, sn, B, S):
    T = proj.shape[0]
    blk = lambda col: pl.BlockSpec((S, DIL_HD), lambda b, h: (b, col + h))
    return pl.pallas_call(
        _dil_kernel,
        grid=(B, DIL_HEADS),
        in_specs=[blk(_COL_DQ), blk(_COL_DK), blk(_COL_DV),
                  pl.BlockSpec((S, DIL_HD), lambda b, h: (b, 0)),
                  pl.BlockSpec((S, DIL_HD), lambda b, h: (b, 0))],
        out_specs=pl.BlockSpec((S, DIL_HD), lambda b, h: (b, h)),
        out_shape=jax.ShapeDtypeStruct((T, DIL_HEADS * DIL_HD), F32),
        scratch_shapes=[pltpu.VMEM((S, DIL_HD), F32), pltpu.VMEM((S, DIL_HD), F32),
                        pltpu.VMEM((S, DIL_HD), BF16), pltpu.VMEM((S, DIL_HD), BF16),
                        pltpu.VMEM((S, DIL_HD), BF16), pltpu.VMEM((S, DIL_HD), F32)],
        compiler_params=_params(("parallel", "parallel"), 48),
        name="dil",
    )(proj, proj, proj, cs, sn)


def _outproj_kernel(og_ref, od_ref, w_ref, x_ref, ga_ref, o_ref):
    kg = og_ref.shape[1]
    mixed = jnp.dot(og_ref[...].astype(BF16), w_ref[:kg, :], preferred_element_type=F32)
    mixed = mixed + jnp.dot(od_ref[...].astype(BF16), w_ref[kg:, :], preferred_element_type=F32)
    o_ref[...] = x_ref[...] + ga_ref[0] * mixed


def _outproj(o_gla, o_dil, w_out, x2, mod3, S):
    T, D = x2.shape
    tm = 512
    per_b = S // tm
    kg, kd = o_gla.shape[1], o_dil.shape[1]
    return pl.pallas_call(
        _outproj_kernel,
        grid=(T // tm,),
        in_specs=[pl.BlockSpec((tm, kg), lambda i: (i, 0)),
                  pl.BlockSpec((tm, kd), lambda i: (i, 0)),
                  pl.BlockSpec((kg + kd, D), lambda i: (0, 0)),
                  pl.BlockSpec((tm, D), lambda i: (i, 0)),
                  pl.BlockSpec((1, 1, D), lambda i: ((i // per_b) * 6 + 2, 0, 0))],
        out_specs=pl.BlockSpec((tm, D), lambda i: (i, 0)),
        out_shape=jax.ShapeDtypeStruct((T, D), F32),
        compiler_params=_params(("parallel",), 48),
        name="outproj",
    )(o_gla, o_dil, w_out, x2, mod3)


def _pq_kernel(x_ref, g_ref, sc_ref, sh_ref, w_ref, o_ref):
    h = _rms(x_ref[...], g_ref[...]) * (1.0 + sc_ref[0]) + sh_ref[0]
    o_ref[...] = jnp.dot(h.astype(BF16), w_ref[...], preferred_element_type=F32)


def _pq(x1, g, mod3, wq, S):
    T, D = x1.shape
    N = wq.shape[1]
    tm = 512
    per_b = S // tm
    return pl.pallas_call(
        _pq_kernel,
        grid=(T // tm,),
        in_specs=[pl.BlockSpec((tm, D), lambda i: (i, 0)),
                  pl.BlockSpec((1, D), lambda i: (0, 0)),
                  pl.BlockSpec((1, 1, D), lambda i: ((i // per_b) * 6 + 4, 0, 0)),
                  pl.BlockSpec((1, 1, D), lambda i: ((i // per_b) * 6 + 3, 0, 0)),
                  pl.BlockSpec((D, N), lambda i: (0, 0))],
        out_specs=pl.BlockSpec((tm, N), lambda i: (i, 0)),
        out_shape=jax.ShapeDtypeStruct((T, N), F32),
        compiler_params=_params(("parallel",), 48),
        name="pq",
    )(x1, g, mod3, mod3, wq)


def _top_rows(s, k, payload=None):
    n_rows = s.shape[0]
    rid = lax.broadcasted_iota(jnp.int32, s.shape, 0).astype(F32)
    vals, picks = [], []
    for _ in range(k):
        m = jnp.max(s, axis=0, keepdims=True)
        pos = jnp.min(jnp.where(s == m, rid, float(n_rows)), axis=0, keepdims=True)
        hit = rid == pos
        vals.append(m)
        if payload is None:
            picks.append(pos)
        else:
            picks.append(jnp.sum(jnp.where(hit, payload, 0), axis=0, keepdims=True))
        s = jnp.where(hit, -jnp.inf, s)
    picks = jnp.concatenate(picks, axis=0)
    return jnp.concatenate(vals, axis=0), picks.astype(jnp.int32)


def _staircase(a, b, combine, fill):
    K = a.shape[0]
    half = K // 2
    jrow = lax.broadcasted_iota(jnp.int32, (half, a.shape[1]), 0)
    pieces = [combine(a[0:1], b)]
    for i in range(1, half):
        piece = combine(a[i:i + 1], b[0:half])
        width = K // (i + 1)
        pieces.append(piece if width >= half else jnp.where(jrow < width, piece, fill))
    pieces.append(combine(a[half:K], b[0:1]))
    return jnp.concatenate(pieces, axis=0)


def _topk_kernel(q_ref, keys_ref, idx_ref, gate_ref):
    K = PEER_TOPK
    for h in range(PEER_HEADS):
        tops = []
        for half in range(2):
            c0 = (h * 2 + half) * PEER_HALF
            qh = q_ref[:, c0:c0 + PEER_HALF].astype(BF16)
            sc = lax.dot_general(keys_ref[h, half], qh, _NT, preferred_element_type=F32)
            tops.append(_top_rows(sc, K))
        (s0, i0), (s1, i1) = tops
        cand_s = _staircase(s0, s1, lambda a, b: a + b, -jnp.inf)
        cand_i = _staircase(i0, i1, lambda a, b: a * PEER_NKEYS + b, 0)
        best, idx = _top_rows(cand_s, K, payload=cand_i)
        e = jnp.exp(best - best[0:1])
        gate = e / jnp.sum(e, axis=0, keepdims=True)
        idx_ref[h * K:(h + 1) * K, :] = idx
        gate_ref[h * K:(h + 1) * K, :] = gate


def _topk(qp, keys_bf):
    T, N = qp.shape
    tt = 256
    HK = PEER_HEADS * PEER_TOPK
    return pl.pallas_call(
        _topk_kernel,
        grid=(T // tt,),
        in_specs=[pl.BlockSpec((tt, N), lambda i: (i, 0)),
                  pl.BlockSpec(keys_bf.shape, lambda i: (0, 0, 0, 0))],
        out_specs=[pl.BlockSpec((HK, tt), lambda i: (0, i)),
                   pl.BlockSpec((HK, tt), lambda i: (0, i))],
        out_shape=[jax.ShapeDtypeStruct((HK, T), jnp.int32),
                   jax.ShapeDtypeStruct((HK, T), F32)],
        compiler_params=_params(("parallel",), 32),
        name="topk",
    )(qp, keys_bf)


_PEER_TB = 128
_PEER_SUB = 8
_SUBLANES = 8
_PEER_CHUNKS = 16
_SC_LANES = 16
_SC_ROWS = 16


def _peer_u_kernel(idx_hbm, gate_ref, x1_ref, gn_ref, sc_ref, sh_ref, u_hbm, w_ref,
                   idx_smem, ub0, ub1, h_scr, sem_i, sem_u, *, step0):
    HK = PEER_HEADS * PEER_TOPK
    TB, SUB = _PEER_TB, _PEER_SUB
    R = SUB * HK
    N = TB * HK
    nsub = TB // SUB
    D = x1_ref.shape[1]
    nchunk = D // LANE
    tiles = HK // _SUBLANES
    i = pl.program_id(0)
    n = pl.num_programs(0)
    cur = lax.rem(i, 2) * N
    nxt = N - cur
    more = i + 1 < n
    ubufs = (ub0, ub1)

    def idx_copy(step, base):
        return pltpu.make_async_copy(idx_hbm.at[pl.ds((step0 + step) * N, N)],
                                     idx_smem.at[pl.ds(base, N)], sem_i)

    def issue_token(base, t, slot):
        for k in range(HK):
            e = idx_smem[base + t * HK + k]
            rt, s = t * tiles + k // _SUBLANES, k % _SUBLANES
            pltpu.make_async_copy(u_hbm.at[e], ubufs[slot].at[rt, :, s, :],
                                  sem_u.at[slot]).start(priority=k % 2)

    def wait(slot):
        pltpu.make_async_copy(ubufs[slot], ubufs[slot], sem_u.at[slot]).wait()

    @pl.when(i == 0)
    def _():
        first = idx_copy(0, 0)
        first.start()
        first.wait()
        for t in range(SUB):
            issue_token(0, t, 0)

    @pl.when(more)
    def _():
        idx_copy(i + 1, nxt).start()

    h_scr[...] = _rms(x1_ref[...], gn_ref[...]) * (1.0 + sc_ref[0]) + sh_ref[0]
    lane = lax.broadcasted_iota(jnp.int32, (HK, TB), 1)

    def compute_token(j, t, slot, wacc):
        ub = ubufs[slot]
        tok = j * SUB + t
        xt = h_scr[pl.ds(tok, 1), :]
        rows = slice(t * tiles, (t + 1) * tiles)
        part = ub[rows, 0].reshape(HK, LANE) * xt[:, 0:LANE]
        for c in range(1, nchunk):
            part = part + ub[rows, c].reshape(HK, LANE) * xt[:, c * LANE:(c + 1) * LANE]
        a = jnp.sum(part, axis=1, keepdims=True)
        hit = lane == tok
        g = jnp.sum(jnp.where(hit, gate_ref[...], 0.0), axis=1, keepdims=True)
        wgt = g * (0.5 * a * (1.0 + lax.erf(a * (2.0 ** -0.5))))
        return jnp.where(hit, wgt, wacc)

    def half(j, slot, next_base, wacc):
        wait(slot)
        for t in range(SUB):
            issue_token(next_base, t, 1 - slot)
            wacc = compute_token(j, t, slot, wacc)
        return wacc

    def pair(jj, wacc):
        j0 = 2 * jj
        wacc = half(j0, 0, cur + (j0 + 1) * R, wacc)
        last = jj == nsub // 2 - 1

        @pl.when(jnp.logical_and(last, more))
        def _():
            idx_copy(i + 1, nxt).wait()

        after = jnp.where(more, nxt, cur)
        return half(j0 + 1, 1, jnp.where(last, after, cur + (j0 + 2) * R), wacc)

    wacc = lax.fori_loop(0, nsub // 2, pair, jnp.zeros((HK, TB), F32))

    @pl.when(jnp.logical_not(more))
    def _():
        wait(0)

    w_ref[:, :HK] = jnp.zeros((TB, HK), F32)
    w_ref[:, HK:] = wacc.T


def _peer_u(idx_flat, gate_t, x1, g_norm, mod3, u3, S, step0, nsteps):
    T, D = x1.shape
    HK = PEER_HEADS * PEER_TOPK
    TB, SUB = _PEER_TB, _PEER_SUB
    per_b = S // TB
    modrow = lambda k: (lambda i: (((step0 + i) // per_b) * 6 + k, 0, 0))
    gbuf = pltpu.VMEM((SUB * HK // _SUBLANES, D // LANE, _SUBLANES, LANE), F32)
    return pl.pallas_call(
        functools.partial(_peer_u_kernel, step0=step0),
        grid=(nsteps,),
        in_specs=[pl.BlockSpec(memory_space=pl.ANY),
                  pl.BlockSpec((HK, TB), lambda i: (0, step0 + i)),
                  pl.BlockSpec((TB, D), lambda i: (step0 + i, 0)),
                  pl.BlockSpec((1, D), lambda i: (0, 0)),
                  pl.BlockSpec((1, 1, D), modrow(4)),
                  pl.BlockSpec((1, 1, D), modrow(3)),
                  pl.BlockSpec(memory_space=pl.ANY)],
        out_specs=pl.BlockSpec((TB, 2 * HK), lambda i: (i, 0)),
        out_shape=jax.ShapeDtypeStruct((nsteps * TB, 2 * HK), F32),
        scratch_shapes=[pltpu.SMEM((2 * TB * HK,), jnp.int32),
                        gbuf, gbuf,
                        pltpu.VMEM((TB, D), F32),
                        pltpu.SemaphoreType.DMA,
                        pltpu.SemaphoreType.DMA((2,))],
        compiler_params=_params(("arbitrary",), 40),
        name="peer_u",
    )(idx_flat, gate_t, x1, g_norm, mod3, mod3, u3)


def _sc_peer_v(v_tab, idx_flat, wgt, tok_base):
    E, nblk, _ = v_tab.shape
    D = nblk * LANE
    Tc, HK = wgt.shape[0], wgt.shape[1] // 2
    info = plsc.get_sparse_core_info()
    nw = info.num_cores * info.num_subcores
    tpw = Tc // nw
    CH = _SC_ROWS
    nch = HK // CH
    nsl = D // _SC_LANES
    mesh = plsc.VectorSubcoreMesh(core_axis_name="c", subcore_axis_name="s")

    @functools.partial(
        pl.kernel, mesh=mesh, out_type=jax.ShapeDtypeStruct((Tc, D), F32),
        scratch_types=[pltpu.VMEM((tpw * HK,), jnp.int32), pltpu.VMEM((2 * HK,), F32),
                       pltpu.VMEM((D,), F32),
                       pltpu.VMEM((CH, nblk, LANE), F32), pltpu.VMEM((CH, nblk, LANE), F32),
                       pltpu.SemaphoreType.DMA, pltpu.SemaphoreType.DMA],
        compiler_params=pltpu.CompilerParams(needs_layout_passes=False),
        name="sc_peer_v",
    )
    def k(tab_hbm, idx_hbm, w_hbm, o_hbm, idx_v, w_v, o_v, buf0, buf1, g0, g1):
        wid = lax.axis_index("s") * info.num_cores + lax.axis_index("c")
        tok0 = wid * tpw
        pltpu.sync_copy(idx_hbm.at[pl.ds((tok_base + tok0) * HK, tpw * HK)], idx_v)
        bufs, gs = (buf0, buf1), (g0, g1)

        def gather(g, b):
            return pltpu.make_async_copy(tab_hbm.at[idx_v.at[pl.ds(g * CH, CH)]], bufs[b], gs[b])

        gather(0, 0).start()

        @pl.loop(0, tpw)
        def _(t):
            pltpu.sync_copy(w_hbm.at[tok0 + t], w_v)

            @pl.loop(0, nsl)
            def _(c):
                o_v[pl.ds(pl.multiple_of(c * _SC_LANES, _SC_LANES), _SC_LANES)] = jnp.zeros((_SC_LANES,), F32)

            for ch in range(nch):
                b = ch % 2
                g = t * nch + ch
                gather(g, b).wait()

                @pl.when(g + 1 < tpw * nch)
                def _():
                    gather(g + 1, 1 - b).start()

                ws = [plsc.load_gather(w_v, [jnp.full((_SC_LANES,), HK + ch * CH + r, jnp.int32)])
                      for r in range(CH)]

                @pl.loop(0, nsl, step=2)
                def _(c):
                    for half in range(2):
                        off = pl.multiple_of((c + half) * _SC_LANES, _SC_LANES)
                        blk = (c + half) // (LANE // _SC_LANES)
                        lo = pl.multiple_of(off - blk * LANE, _SC_LANES)
                        parts = [ws[r] * bufs[b][r, blk, pl.ds(lo, _SC_LANES)] for r in range(CH)]
                        while len(parts) > 1:
                            parts = [parts[p] + parts[p + 1] for p in range(0, len(parts), 2)]
                        o_v[pl.ds(off, _SC_LANES)] = o_v[pl.ds(off, _SC_LANES)] + parts[0]

            pltpu.sync_copy(o_v, o_hbm.at[tok0 + t])

    return k(v_tab, idx_flat, wgt)


def _peer_fin_kernel(x_ref, p_ref, ga_ref, gf_ref, o_ref):
    o_ref[...] = _rms(x_ref[...] + ga_ref[0] * p_ref[...], gf_ref[...])


def _peer_fin(x1, po, mod3, g_final, S):
    T, D = x1.shape
    tm = 512
    per_b = S // tm
    return pl.pallas_call(
        _peer_fin_kernel,
        grid=(T // tm,),
        in_specs=[pl.BlockSpec((tm, D), lambda i: (i, 0)),
                  pl.BlockSpec((tm, D), lambda i: (i, 0)),
                  pl.BlockSpec((1, 1, D), lambda i: ((i // per_b) * 6 + 5, 0, 0)),
                  pl.BlockSpec((1, D), lambda i: (0, 0))],
        out_specs=pl.BlockSpec((tm, D), lambda i: (i, 0)),
        out_shape=jax.ShapeDtypeStruct((T, D), F32),
        compiler_params=_params(("parallel",), 40),
        name="peer_fin",
    )(x1, po, mod3, g_final)


def _pad_gate(w, lo):
    rank = w.shape[0]
    wh = w.reshape(rank, GLA_HEADS, GLA_DK).transpose(1, 0, 2)
    return jnp.zeros((GLA_HEADS, LANE, GLA_DK), F32).at[:, lo:lo + rank, :].set(wh)


def kernel(x, c, positions, w_ada, b_ada, g_norm_mix, w_in, w_gate_f, b_gate_f, w_gate_b, b_gate_b,
           g_gla_out, w_out, g_norm_ffn, w_peer_q, peer_sub_keys, peer_u, peer_v, g_final):
    B, S, D = x.shape
    T = B * S
    depth = w_ada.shape[0]
    assert depth == 1, "the final norm is fused into the last PEER call; one layer only"
    xt = x.reshape(T, D)
    cs, sn = _rope_tables(positions.reshape(T, 1))
    gz0 = 2 * GLA_HEADS * GLA_DK + 2 * GLA_HEADS * GLA_DV
    gz1 = gz0 + 2 * GLA_GATE_RANK
    for l in range(depth):
        mod3 = _ada(c, w_ada[l], b_ada[l]).reshape(B * 6, 1, D)
        w_main = jnp.concatenate([w_in[l][:, :gz0], w_in[l][:, gz1:]], axis=1).astype(BF16)
        w_z = jnp.pad(w_in[l][:, gz0:gz1], ((0, 0), (0, LANE - (gz1 - gz0)))).astype(BF16)
        proj, gz = _inproj(xt, g_norm_mix[l].reshape(1, D), mod3, w_main, w_z, S)
        o_gla = _gla(proj, gz,
                     _pad_gate(w_gate_f[l], 0), b_gate_f[l].reshape(GLA_HEADS, 1, GLA_DK),
                     _pad_gate(w_gate_b[l], GLA_GATE_RANK), b_gate_b[l].reshape(GLA_HEADS, 1, GLA_DK),
                     g_gla_out[l].reshape(GLA_HEADS, 1, GLA_DV), B, S)
        o_dil = _dil(proj, cs, sn, B, S)
        x1 = _outproj(o_gla, o_dil, w_out[l].astype(BF16), xt, mod3, S)
        qp = _pq(x1, g_norm_ffn[l].reshape(1, D), mod3, w_peer_q[l].astype(BF16), S)
        idx_t, gate_t = _topk(qp, peer_sub_keys[l].astype(BF16))
        idx_flat = idx_t.T.reshape(-1)
        E = peer_u.shape[1]
        u3 = peer_u[l].reshape(E, D // LANE, LANE)
        v3 = peer_v[l].reshape(E, D // LANE, LANE)
        steps = T // _PEER_TB // _PEER_CHUNKS
        outs = []
        for ck in range(_PEER_CHUNKS):
            wgt = _peer_u(idx_flat, gate_t, x1, g_norm_ffn[l].reshape(1, D), mod3, u3, S, ck * steps, steps)
            outs.append(_sc_peer_v(v3, idx_flat, wgt, ck * steps * _PEER_TB))
        xt = _peer_fin(x1, jnp.concatenate(outs, axis=0), mod3, g_final.reshape(1, D), S)
    return xt.reshape(B, S, D)
```

```python
import functools

import jax
import jax.numpy as jnp
from jax import lax
from jax.experimental import pallas as pl
from jax.experimental.pallas import tpu as pltpu
from jax.experimental.pallas import tpu_sc as plsc

F32 = jnp.float32
BF16 = jnp.bfloat16
HIGHEST = lax.Precision.HIGHEST

NORM_EPS = 1e-6
GLA_HEADS = 4
GLA_DK = 128
GLA_DV = 256
GLA_GATE_RANK = 16
GLA_TAU = 16.0
GLA_CHUNK = 64
DIL_HD = 128
DIL_HEADS = 8
DIL_PATTERNS = ((128, 1), (512, 4), (2048, 16))
ROPE_THETA = 10000.0
NEG_INF = -1e30
PEER_HEADS = 8
PEER_NKEYS = 128
PEER_TOPK = 16
PEER_HALF = 128

LANE = 128
MIB = 1024 * 1024

_COL_GQ, _COL_GK, _COL_GV, _COL_GR, _COL_DQ, _COL_DK, _COL_DV = 0, 4, 8, 16, 24, 32, 40
_PROJ_W = 48 * LANE

_NT = (((1,), (1,)), ((), ()))
_TN = (((0,), (0,)), ((), ()))


def _params(sem, vmem_mib):
    return pltpu.CompilerParams(dimension_semantics=sem, vmem_limit_bytes=vmem_mib * MIB)


def _rms(x, g):
    return x * lax.rsqrt(jnp.mean(x * x, axis=-1, keepdims=True) + NORM_EPS) * g


def _silu(x):
    return x / (1.0 + jnp.exp(-x))


def _ada_kernel(c_ref, w_ref, b_ref, o_ref):
    s = _silu(c_ref[...]).astype(BF16)
    o_ref[...] = jnp.dot(s, w_ref[...].astype(BF16), preferred_element_type=F32) + b_ref[...]


def _ada(c, w, b):
    B, D = c.shape
    N = w.shape[1]
    tn = 1024
    cp = jnp.zeros((8, D), F32).at[:B].set(c)
    out = pl.pallas_call(
        _ada_kernel,
        grid=(N // tn,),
        in_specs=[pl.BlockSpec((8, D), lambda j: (0, 0)),
                  pl.BlockSpec((D, tn), lambda j: (0, j)),
                  pl.BlockSpec((1, tn), lambda j: (0, j))],
        out_specs=pl.BlockSpec((8, tn), lambda j: (0, j)),
        out_shape=jax.ShapeDtypeStruct((8, N), F32),
        compiler_params=_params(("parallel",), 40),
        name="ada",
    )(cp, w, b.reshape(1, N))
    return out[:B]


def _inproj_kernel(x_ref, g_ref, sc_ref, sh_ref, w_ref, wz_ref, o_ref, z_ref, h_scr):
    @pl.when(pl.program_id(1) == 0)
    def _():
        h = _rms(x_ref[...], g_ref[...]) * (1.0 + sc_ref[0]) + sh_ref[0]
        hb = h.astype(BF16)
        h_scr[...] = hb
        z_ref[...] = jnp.dot(hb, wz_ref[...], preferred_element_type=F32)

    o_ref[...] = jnp.dot(h_scr[...], w_ref[...], preferred_element_type=F32)


def _inproj(x2, g, mod3, w_main, w_z, S):
    T, D = x2.shape
    tm, tn = 1024, 768
    per_b = S // tm
    return pl.pallas_call(
        _inproj_kernel,
        grid=(T // tm, _PROJ_W // tn),
        in_specs=[pl.BlockSpec((tm, D), lambda i, j: (i, 0)),
                  pl.BlockSpec((1, D), lambda i, j: (0, 0)),
                  pl.BlockSpec((1, 1, D), lambda i, j: ((i // per_b) * 6 + 1, 0, 0)),
                  pl.BlockSpec((1, 1, D), lambda i, j: ((i // per_b) * 6 + 0, 0, 0)),
                  pl.BlockSpec((D, tn), lambda i, j: (0, j)),
                  pl.BlockSpec((D, LANE), lambda i, j: (0, 0))],
        out_specs=[pl.BlockSpec((tm, tn), lambda i, j: (i, j)),
                   pl.BlockSpec((tm, LANE), lambda i, j: (i, 0))],
        out_shape=[jax.ShapeDtypeStruct((T, _PROJ_W), F32),
                   jax.ShapeDtypeStruct((T, LANE), F32)],
        scratch_shapes=[pltpu.VMEM((tm, D), BF16)],
        compiler_params=_params(("parallel", "arbitrary"), 48),
        name="inproj",
    )(x2, g, mod3, mod3, w_main, w_z)


def _rope_kernel(pos_ref, f_ref, sg_ref, cs_ref, sn_ref):
    ang = pos_ref[...].astype(F32) * f_ref[...]
    cs_ref[...] = jnp.cos(ang)
    sn_ref[...] = jnp.sin(ang) * sg_ref[...]


def _rope_tables(pos_col):
    T = pos_col.shape[0]
    half = DIL_HD // 2
    inv = jnp.power(ROPE_THETA, -jnp.arange(half, dtype=F32) * 2.0 / DIL_HD)
    freq = jnp.concatenate([inv, inv]).reshape(1, DIL_HD)
    sign = jnp.concatenate([-jnp.ones((half,), F32), jnp.ones((half,), F32)]).reshape(1, DIL_HD)
    tm = 1024
    return pl.pallas_call(
        _rope_kernel,
        grid=(T // tm,),
        in_specs=[pl.BlockSpec((tm, 1), lambda i: (i, 0)),
                  pl.BlockSpec((1, DIL_HD), lambda i: (0, 0)),
                  pl.BlockSpec((1, DIL_HD), lambda i: (0, 0))],
        out_specs=[pl.BlockSpec((tm, DIL_HD), lambda i: (i, 0)),
                   pl.BlockSpec((tm, DIL_HD), lambda i: (i, 0))],
        out_shape=[jax.ShapeDtypeStruct((T, DIL_HD), F32)] * 2,
        compiler_params=_params(("parallel",), 32),
        name="rope",
    )(pos_col, freq, sign)


def _gla_kernel(q_ref, k_ref, v_ref, r_ref, z_ref, wgf_ref, bgf_ref, wgb_ref, bgb_ref, g_ref,
                o_ref, laf_scr, lab_scr, of_scr):
    S = q_ref.shape[0]
    C = GLA_CHUNK
    n = S // C
    scale = GLA_DK ** -0.5
    row = lax.broadcasted_iota(jnp.int32, (C, C), 0)
    col = lax.broadcasted_iota(jnp.int32, (C, C), 1)

    def log_gate(w_ref, b_ref):
        zz = jnp.dot(z_ref[...], w_ref[0], precision=HIGHEST, preferred_element_type=F32) + b_ref[0]
        return (jnp.minimum(zz, 0.0) - jnp.log(1.0 + jnp.exp(-jnp.abs(zz)))) * (1.0 / GLA_TAU)

    def chunk(fwd, c, st_t, la_scr, o_scr):
        keep = (col <= row) if fwd else (col >= row)
        sl = pl.ds(pl.multiple_of(c * C, C), C)
        cum = jnp.dot(keep.astype(F32), la_scr[sl, :], precision=HIGHEST, preferred_element_type=F32)
        tot = cum[C - 1:C, :] if fwd else cum[0:1, :]
        kk = k_ref[sl, :]
        qd = (q_ref[sl, :] * scale * jnp.exp(cum)).astype(BF16)
        ki = (kk * jnp.exp(-cum)).astype(BF16)
        kte = (kk * jnp.exp(tot - cum)).astype(BF16)
        vb = v_ref[sl, :].astype(BF16)
        attn = lax.dot_general(qd, ki, _NT, preferred_element_type=F32)
        attn = jnp.where(keep, attn, 0.0).astype(BF16)
        o = jnp.dot(attn, vb, preferred_element_type=F32)
        o_scr[sl, :] = o + lax.dot_general(qd, st_t.astype(BF16), _NT, preferred_element_type=F32)
        upd = lax.dot_general(vb, kte, _TN, preferred_element_type=F32)
        return st_t * jnp.exp(tot) + upd

    laf_scr[...] = log_gate(wgf_ref, bgf_ref)
    lab_scr[...] = log_gate(wgb_ref, bgb_ref)

    def body(i, states):
        return (chunk(True, i, states[0], laf_scr, of_scr),
                chunk(False, n - 1 - i, states[1], lab_scr, o_ref))

    zero = jnp.zeros((GLA_DV, GLA_DK), F32)
    lax.fori_loop(0, n, body, (zero, zero), unroll=4)
    y = _rms(of_scr[...] + o_ref[...], g_ref[0])
    o_ref[...] = y * _silu(r_ref[...])


def _gla(proj, gz, wgf, bgf, wgb, bgb, g_out, B, S):
    T = proj.shape[0]
    H = GLA_HEADS
    return pl.pallas_call(
        _gla_kernel,
        grid=(B, H),
        in_specs=[pl.BlockSpec((S, GLA_DK), lambda b, h: (b, _COL_GQ + h)),
                  pl.BlockSpec((S, GLA_DK), lambda b, h: (b, _COL_GK + h)),
                  pl.BlockSpec((S, GLA_DV), lambda b, h: (b, _COL_GV // 2 + h)),
                  pl.BlockSpec((S, GLA_DV), lambda b, h: (b, _COL_GR // 2 + h)),
                  pl.BlockSpec((S, LANE), lambda b, h: (b, 0)),
                  pl.BlockSpec((1, LANE, GLA_DK), lambda b, h: (h, 0, 0)),
                  pl.BlockSpec((1, 1, GLA_DK), lambda b, h: (h, 0, 0)),
                  pl.BlockSpec((1, LANE, GLA_DK), lambda b, h: (h, 0, 0)),
                  pl.BlockSpec((1, 1, GLA_DK), lambda b, h: (h, 0, 0)),
                  pl.BlockSpec((1, 1, GLA_DV), lambda b, h: (h, 0, 0))],
        out_specs=pl.BlockSpec((S, GLA_DV), lambda b, h: (b, h)),
        out_shape=jax.ShapeDtypeStruct((T, GLA_HEADS * GLA_DV), F32),
        scratch_shapes=[pltpu.VMEM((S, GLA_DK), F32), pltpu.VMEM((S, GLA_DK), F32),
                        pltpu.VMEM((S, GLA_DV), F32)],
        compiler_params=_params(("parallel", "parallel"), 56),
        name="gla",
    )(proj, proj, proj, proj, gz, wgf, bgf, wgb, bgb, g_out)


_DIL_QB = 128
_DIL_HALF = 64
_DIL_KW = _DIL_QB + 2 * _DIL_HALF


def _dil_kernel(q_ref, k_ref, v_ref, cs_ref, sn_ref, o_ref, qf_scr, kf_scr, qb_scr, kb_scr, vb_scr, lse_scr):
    S = q_ref.shape[0]
    QB, KW, HALF = _DIL_QB, _DIL_KW, _DIL_HALF
    rot = DIL_HD // 2
    cs, sn = cs_ref[...], sn_ref[...]
    q = q_ref[...]
    qf_scr[...] = (q * cs + pltpu.roll(q, rot, 1) * sn) * (DIL_HD ** -0.5)
    k = k_ref[...]
    kf_scr[...] = k * cs + pltpu.roll(k, rot, 1) * sn
    rel = (lax.broadcasted_iota(jnp.int32, (QB, KW), 1) - lax.broadcasted_iota(jnp.int32, (QB, KW), 0))

    for pi, (window, d) in enumerate(DIL_PATTERNS):
        assert window // (2 * d) == HALF
        M = S // d
        per = M // QB
        for r in range(d):
            rows = pl.ds(r * M, M)
            src = pl.ds(r, M, stride=d) if d > 1 else pl.ds(0, M)
            qb_scr[rows, :] = qf_scr[src, :].astype(BF16)
            kb_scr[rows, :] = kf_scr[src, :].astype(BF16)
            vb_scr[rows, :] = v_ref[src, :].astype(BF16)

        for r in range(d):
            def block(mb, carry, r=r, first=(pi == 0)):
                m0 = mb * QB
                start = jnp.clip(m0 - HALF, 0, M - KW)
                qrow = pl.multiple_of(r * M + m0, QB)
                krow = pl.multiple_of(r * M + start, HALF)
                s = lax.dot_general(qb_scr[pl.ds(qrow, QB), :], kb_scr[pl.ds(krow, KW), :], _NT,
                                    preferred_element_type=F32)
                s = jnp.where(jnp.abs(rel + (start - m0)) <= HALF, s, NEG_INF)
                m = jnp.max(s, axis=-1, keepdims=True)
                p = jnp.exp(s - m)
                l = jnp.sum(p, axis=-1, keepdims=True)
                o = jnp.dot(p.astype(BF16), vb_scr[pl.ds(krow, KW), :], preferred_element_type=F32) / l
                ls = jnp.broadcast_to(m + jnp.log(l), (QB, DIL_HD))
                base = pl.multiple_of(m0 * d, QB)
                nat = pl.ds(base + r, QB, stride=d) if d > 1 else pl.ds(base, QB)
                if first:
                    o_ref[nat, :] = o
                    lse_scr[nat, :] = ls
                else:
                    o0, l0 = o_ref[nat, :], lse_scr[nat, :]
                    mx = jnp.maximum(l0, ls)
                    w0, w1 = jnp.exp(l0 - mx), jnp.exp(ls - mx)
                    den = w0 + w1
                    o_ref[nat, :] = (w0 * o0 + w1 * o) / den
                    lse_scr[nat, :] = mx + jnp.log(den)
                return carry

            lax.fori_loop(0, per, block, 0, unroll=min(per, 4))


def _dil(proj, cs, sn, B, S):
    T = proj.shape[0]
    blk = lambda col: pl.BlockSpec((S, DIL_HD), lambda b, h: (b, col + h))
    return pl.pallas_call(
        _dil_kernel,
        grid=(B, DIL_HEADS),
        in_specs=[blk(_COL_DQ), blk(_COL_DK), blk(_COL_DV),
                  pl.BlockSpec((S, DIL_HD), lambda b, h: (b, 0)),
                  pl.BlockSpec((S, DIL_HD), lambda b, h: (b, 0))],
        out_specs=pl.BlockSpec((S, DIL_HD), lambda b, h: (b, h)),
        out_shape=jax.ShapeDtypeStruct((T, DIL_HEADS * DIL_HD), F32),
        scratch_shapes=[pltpu.VMEM((S, DIL_HD), F32), pltpu.VMEM((S, DIL_HD), F32),
                        pltpu.VMEM((S, DIL_HD), BF16), pltpu.VMEM((S, DIL_HD), BF16),
                        pltpu.VMEM((S, DIL_HD), BF16), pltpu.VMEM((S, DIL_HD), F32)],
        compiler_params=_params(("parallel", "parallel"), 48),
        name="dil",
    )(proj, proj, proj, cs, sn)


def _outproj_kernel(og_ref, od_ref, w_ref, x_ref, ga_ref, o_ref):
    kg = og_ref.shape[1]
    mixed = jnp.dot(og_ref[...].astype(BF16), w_ref[:kg, :], preferred_element_type=F32)
    mixed = mixed + jnp.dot(od_ref[...].astype(BF16), w_ref[kg:, :], preferred_element_type=F32)
    o_ref[...] = x_ref[...] + ga_ref[0] * mixed


def _outproj(o_gla, o_dil, w_out, x2, mod3, S):
    T, D = x2.shape
    tm = 512
    per_b = S // tm
    kg, kd = o_gla.shape[1], o_dil.shape[1]
    return pl.pallas_call(
        _outproj_kernel,
        grid=(T // tm,),
        in_specs=[pl.BlockSpec((tm, kg), lambda i: (i, 0)),
                  pl.BlockSpec((tm, kd), lambda i: (i, 0)),
                  pl.BlockSpec((kg + kd, D), lambda i: (0, 0)),
                  pl.BlockSpec((tm, D), lambda i: (i, 0)),
                  pl.BlockSpec((1, 1, D), lambda i: ((i // per_b) * 6 + 2, 0, 0))],
        out_specs=pl.BlockSpec((tm, D), lambda i: (i, 0)),
        out_shape=jax.ShapeDtypeStruct((T, D), F32),
        compiler_params=_params(("parallel",), 48),
        name="outproj",
    )(o_gla, o_dil, w_out, x2, mod3)


def _pq_kernel(x_ref, g_ref, sc_ref, sh_ref, w_ref, o_ref):
    h = _rms(x_ref[...], g_ref[...]) * (1.0 + sc_ref[0]) + sh_ref[0]
    o_ref[...] = jnp.dot(h.astype(BF16), w_ref[...], preferred_element_type=F32)


def _pq(x1, g, mod3, wq, S):
    T, D = x1.shape
    N = wq.shape[1]
    tm = 512
    per_b = S // tm
    return pl.pallas_call(
        _pq_kernel,
        grid=(T // tm,),
        in_specs=[pl.BlockSpec((tm, D), lambda i: (i, 0)),
                  pl.BlockSpec((1, D), lambda i: (0, 0)),
                  pl.BlockSpec((1, 1, D), lambda i: ((i // per_b) * 6 + 4, 0, 0)),
                  pl.BlockSpec((1, 1, D), lambda i: ((i // per_b) * 6 + 3, 0, 0)),
                  pl.BlockSpec((D, N), lambda i: (0, 0))],
        out_specs=pl.BlockSpec((tm, N), lambda i: (i, 0)),
        out_shape=jax.ShapeDtypeStruct((T, N), F32),
        compiler_params=_params(("parallel",), 48),
        name="pq",
    )(x1, g, mod3, mod3, wq)


def _top_rows(s, k, payload=None):
    n_rows = s.shape[0]
    rid = lax.broadcasted_iota(jnp.int32, s.shape, 0).astype(F32)
    vals, picks = [], []
    for _ in range(k):
        m = jnp.max(s, axis=0, keepdims=True)
        pos = jnp.min(jnp.where(s == m, rid, float(n_rows)), axis=0, keepdims=True)
        hit = rid == pos
        vals.append(m)
        if payload is None:
            picks.append(pos)
        else:
            picks.append(jnp.sum(jnp.where(hit, payload, 0), axis=0, keepdims=True))
        s = jnp.where(hit, -jnp.inf, s)
    picks = jnp.concatenate(picks, axis=0)
    return jnp.concatenate(vals, axis=0), picks.astype(jnp.int32)


def _staircase(a, b, combine, fill):
    K = a.shape[0]
    half = K // 2
    jrow = lax.broadcasted_iota(jnp.int32, (half, a.shape[1]), 0)
    pieces = [combine(a[0:1], b)]
    for i in range(1, half):
        piece = combine(a[i:i + 1], b[0:half])
        width = K // (i + 1)
        pieces.append(piece if width >= half else jnp.where(jrow < width, piece, fill))
    pieces.append(combine(a[half:K], b[0:1]))
    return jnp.concatenate(pieces, axis=0)


def _topk_kernel(q_ref, keys_ref, idx_ref, gate_ref):
    K = PEER_TOPK
    for h in range(PEER_HEADS):
        tops = []
        for half in range(2):
            c0 = (h * 2 + half) * PEER_HALF
            qh = q_ref[:, c0:c0 + PEER_HALF].astype(BF16)
            sc = lax.dot_general(keys_ref[h, half], qh, _NT, preferred_element_type=F32)
            tops.append(_top_rows(sc, K))
        (s0, i0), (s1, i1) = tops
        cand_s = _staircase(s0, s1, lambda a, b: a + b, -jnp.inf)
        cand_i = _staircase(i0, i1, lambda a, b: a * PEER_NKEYS + b, 0)
        best, idx = _top_rows(cand_s, K, payload=cand_i)
        e = jnp.exp(best - best[0:1])
        gate = e / jnp.sum(e, axis=0, keepdims=True)
        idx_ref[h * K:(h + 1) * K, :] = idx
        gate_ref[h * K:(h + 1) * K, :] = gate


def _topk(qp, keys_bf):
    T, N = qp.shape
    tt = 256
    HK = PEER_HEADS * PEER_TOPK
    return pl.pallas_call(
        _topk_kernel,
        grid=(T // tt,),
        in_specs=[pl.BlockSpec((tt, N), lambda i: (i, 0)),
                  pl.BlockSpec(keys_bf.shape, lambda i: (0, 0, 0, 0))],
        out_specs=[pl.BlockSpec((HK, tt), lambda i: (0, i)),
                   pl.BlockSpec((HK, tt), lambda i: (0, i))],
        out_shape=[jax.ShapeDtypeStruct((HK, T), jnp.int32),
                   jax.ShapeDtypeStruct((HK, T), F32)],
        compiler_params=_params(("parallel",), 32),
        name="topk",
    )(qp, keys_bf)


_PEER_TB = 128
_PEER_SUB = 8
_SUBLANES = 8
_PEER_CHUNKS = 16
_SC_LANES = 16
_SC_ROWS = 16


def _peer_u_kernel(idx_hbm, gate_ref, x1_ref, gn_ref, sc_ref, sh_ref, u_hbm, w_ref,
                   idx_smem, ub0, ub1, h_scr, sem_i, sem_u, *, step0):
    HK = PEER_HEADS * PEER_TOPK
    TB, SUB = _PEER_TB, _PEER_SUB
    R = SUB * HK
    N = TB * HK
    nsub = TB // SUB
    D = x1_ref.shape[1]
    nchunk = D // LANE
    tiles = HK // _SUBLANES
    i = pl.program_id(0)
    n = pl.num_programs(0)
    cur = lax.rem(i, 2) * N
    nxt = N - cur
    more = i + 1 < n
    ubufs = (ub0, ub1)

    def idx_copy(step, base):
        return pltpu.make_async_copy(idx_hbm.at[pl.ds((step0 + step) * N, N)],
                                     idx_smem.at[pl.ds(base, N)], sem_i)

    def issue_token(base, t, slot):
        for k in range(HK):
            e = idx_smem[base + t * HK + k]
            rt, s = t * tiles + k // _SUBLANES, k % _SUBLANES
            pltpu.make_async_copy(u_hbm.at[e], ubufs[slot].at[rt, :, s, :],
                                  sem_u.at[slot]).start(priority=k % 2)

    def wait(slot):
        pltpu.make_async_copy(ubufs[slot], ubufs[slot], sem_u.at[slot]).wait()

    @pl.when(i == 0)
    def _():
        first = idx_copy(0, 0)
        first.start()
        first.wait()
        for t in range(SUB):
            issue_token(0, t, 0)

    @pl.when(more)
    def _():
        idx_copy(i + 1, nxt).start()

    h_scr[...] = _rms(x1_ref[...], gn_ref[...]) * (1.0 + sc_ref[0]) + sh_ref[0]
    lane = lax.broadcasted_iota(jnp.int32, (HK, TB), 1)

    def compute_token(j, t, slot, wacc):
        ub = ubufs[slot]
        tok = j * SUB + t
        xt = h_scr[pl.ds(tok, 1), :]
        rows = slice(t * tiles, (t + 1) * tiles)
        part = ub[rows, 0].reshape(HK, LANE) * xt[:, 0:LANE]
        for c in range(1, nchunk):
            part = part + ub[rows, c].reshape(HK, LANE) * xt[:, c * LANE:(c + 1) * LANE]
        a = jnp.sum(part, axis=1, keepdims=True)
        hit = lane == tok
        g = jnp.sum(jnp.where(hit, gate_ref[...], 0.0), axis=1, keepdims=True)
        wgt = g * (0.5 * a * (1.0 + lax.erf(a * (2.0 ** -0.5))))
        return jnp.where(hit, wgt, wacc)

    def half(j, slot, next_base, wacc):
        wait(slot)
        for t in range(SUB):
            issue_token(next_base, t, 1 - slot)
            wacc = compute_token(j, t, slot, wacc)
        return wacc

    def pair(jj, wacc):
        j0 = 2 * jj
        wacc = half(j0, 0, cur + (j0 + 1) * R, wacc)
        last = jj == nsub // 2 - 1

        @pl.when(jnp.logical_and(last, more))
        def _():
            idx_copy(i + 1, nxt).wait()

        after = jnp.where(more, nxt, cur)
        return half(j0 + 1, 1, jnp.where(last, after, cur + (j0 + 2) * R), wacc)

    wacc = lax.fori_loop(0, nsub // 2, pair, jnp.zeros((HK, TB), F32))

    @pl.when(jnp.logical_not(more))
    def _():
        wait(0)

    w_ref[:, :HK] = jnp.zeros((TB, HK), F32)
    w_ref[:, HK:] = wacc.T


def _peer_u(idx_flat, gate_t, x1, g_norm, mod3, u3, S, step0, nsteps):
    T, D = x1.shape
    HK = PEER_HEADS * PEER_TOPK
    TB, SUB = _PEER_TB, _PEER_SUB
    per_b = S // TB
    modrow = lambda k: (lambda i: (((step0 + i) // per_b) * 6 + k, 0, 0))
    gbuf = pltpu.VMEM((SUB * HK // _SUBLANES, D // LANE, _SUBLANES, LANE), F32)
    return pl.pallas_call(
        functools.partial(_peer_u_kernel, step0=step0),
        grid=(nsteps,),
        in_specs=[pl.BlockSpec(memory_space=pl.ANY),
                  pl.BlockSpec((HK, TB), lambda i: (0, step0 + i)),
                  pl.BlockSpec((TB, D), lambda i: (step0 + i, 0)),
                  pl.BlockSpec((1, D), lambda i: (0, 0)),
                  pl.BlockSpec((1, 1, D), modrow(4)),
                  pl.BlockSpec((1, 1, D), modrow(3)),
                  pl.BlockSpec(memory_space=pl.ANY)],
        out_specs=pl.BlockSpec((TB, 2 * HK), lambda i: (i, 0)),
        out_shape=jax.ShapeDtypeStruct((nsteps * TB, 2 * HK), F32),
        scratch_shapes=[pltpu.SMEM((2 * TB * HK,), jnp.int32),
                        gbuf, gbuf,
                        pltpu.VMEM((TB, D), F32),
                        pltpu.SemaphoreType.DMA,
                        pltpu.SemaphoreType.DMA((2,))],
        compiler_params=_params(("arbitrary",), 40),
        name="peer_u",
    )(idx_flat, gate_t, x1, g_norm, mod3, mod3, u3)


def _sc_peer_v(v_tab, idx_flat, wgt, tok_base):
    E, nblk, _ = v_tab.shape
    D = nblk * LANE
    Tc, HK = wgt.shape[0], wgt.shape[1] // 2
    info = plsc.get_sparse_core_info()
    nw = info.num_cores * info.num_subcores
    tpw = Tc // nw
    CH = _SC_ROWS
    nch = HK // CH
    nsl = D // _SC_LANES
    mesh = plsc.VectorSubcoreMesh(core_axis_name="c", subcore_axis_name="s")

    @functools.partial(
        pl.kernel, mesh=mesh, out_type=jax.ShapeDtypeStruct((Tc, D), F32),
        scratch_types=[pltpu.VMEM((tpw * HK,), jnp.int32), pltpu.VMEM((2 * HK,), F32),
                       pltpu.VMEM((D,), F32),
                       pltpu.VMEM((CH, nblk, LANE), F32), pltpu.VMEM((CH, nblk, LANE), F32),
                       pltpu.SemaphoreType.DMA, pltpu.SemaphoreType.DMA],
        compiler_params=pltpu.CompilerParams(needs_layout_passes=False),
        name="sc_peer_v",
    )
    def k(tab_hbm, idx_hbm, w_hbm, o_hbm, idx_v, w_v, o_v, buf0, buf1, g0, g1):
        wid = lax.axis_index("s") * info.num_cores + lax.axis_index("c")
        tok0 = wid * tpw
        pltpu.sync_copy(idx_hbm.at[pl.ds((tok_base + tok0) * HK, tpw * HK)], idx_v)
        bufs, gs = (buf0, buf1), (g0, g1)

        def gather(g, b):
            return pltpu.make_async_copy(tab_hbm.at[idx_v.at[pl.ds(g * CH, CH)]], bufs[b], gs[b])

        gather(0, 0).start()

        @pl.loop(0, tpw)
        def _(t):
            pltpu.sync_copy(w_hbm.at[tok0 + t], w_v)

            @pl.loop(0, nsl)
            def _(c):
                o_v[pl.ds(pl.multiple_of(c * _SC_LANES, _SC_LANES), _SC_LANES)] = jnp.zeros((_SC_LANES,), F32)

            for ch in range(nch):
                b = ch % 2
                g = t * nch + ch
                gather(g, b).wait()

                @pl.when(g + 1 < tpw * nch)
                def _():
                    gather(g + 1, 1 - b).start()

                ws = [plsc.load_gather(w_v, [jnp.full((_SC_LANES,), HK + ch * CH + r, jnp.int32)])
                      for r in range(CH)]

                @pl.loop(0, nsl, step=2)
                def _(c):
                    for half in range(2):
                        off = pl.multiple_of((c + half) * _SC_LANES, _SC_LANES)
                        blk = (c + half) // (LANE // _SC_LANES)
                        lo = pl.multiple_of(off - blk * LANE, _SC_LANES)
                        parts = [ws[r] * bufs[b][r, blk, pl.ds(lo, _SC_LANES)] for r in range(CH)]
                        while len(parts) > 1:
                            parts = [parts[p] + parts[p + 1] for p in range(0, len(parts), 2)]
                        o_v[pl.ds(off, _SC_LANES)] = o_v[pl.ds(off, _SC_LANES)] + parts[0]

            pltpu.sync_copy(o_v, o_hbm.at[tok0 + t])

    return k(v_tab, idx_flat, wgt)


def _peer_fin_kernel(x_ref, p_ref, ga_ref, gf_ref, o_ref):
    o_ref[...] = _rms(x_ref[...] + ga_ref[0] * p_ref[...], gf_ref[...])


def _peer_fin(x1, po, mod3, g_final, S):
    T, D = x1.shape
    tm = 512
    per_b = S // tm
    return pl.pallas_call(
        _peer_fin_kernel,
        grid=(T // tm,),
        in_specs=[pl.BlockSpec((tm, D), lambda i: (i, 0)),
                  pl.BlockSpec((tm, D), lambda i: (i, 0)),
                  pl.BlockSpec((1, 1, D), lambda i: ((i // per_b) * 6 + 5, 0, 0)),
                  pl.BlockSpec((1, D), lambda i: (0, 0))],
        out_specs=pl.BlockSpec((tm, D), lambda i: (i, 0)),
        out_shape=jax.ShapeDtypeStruct((T, D), F32),
        compiler_params=_params(("parallel",), 40),
        name="peer_fin",
    )(x1, po, mod3, g_final)


def _pad_gate(w, lo):
    rank = w.shape[0]
    wh = w.reshape(rank, GLA_HEADS, GLA_DK).transpose(1, 0, 2)
    return jnp.zeros((GLA_HEADS, LANE, GLA_DK), F32).at[:, lo:lo + rank, :].set(wh)


def kernel(x, c, positions, w_ada, b_ada, g_norm_mix, w_in, w_gate_f, b_gate_f, w_gate_b, b_gate_b,
           g_gla_out, w_out, g_norm_ffn, w_peer_q, peer_sub_keys, peer_u, peer_v, g_final):
    B, S, D = x.shape
    T = B * S
    depth = w_ada.shape[0]
    assert depth == 1, "the final norm is fused into the last PEER call; one layer only"
    xt = x.reshape(T, D)
    cs, sn = _rope_tables(positions.reshape(T, 1))
    gz0 = 2 * GLA_HEADS * GLA_DK + 2 * GLA_HEADS * GLA_DV
    gz1 = gz0 + 2 * GLA_GATE_RANK
    for l in range(depth):
        mod3 = _ada(c, w_ada[l], b_ada[l]).reshape(B * 6, 1, D)
        w_main = jnp.concatenate([w_in[l][:, :gz0], w_in[l][:, gz1:]], axis=1).astype(BF16)
        w_z = jnp.pad(w_in[l][:, gz0:gz1], ((0, 0), (0, LANE - (gz1 - gz0)))).astype(BF16)
        proj, gz = _inproj(xt, g_norm_mix[l].reshape(1, D), mod3, w_main, w_z, S)
        o_gla = _gla(proj, gz,
                     _pad_gate(w_gate_f[l], 0), b_gate_f[l].reshape(GLA_HEADS, 1, GLA_DK),
                     _pad_gate(w_gate_b[l], GLA_GATE_RANK), b_gate_b[l].reshape(GLA_HEADS, 1, GLA_DK),
                     g_gla_out[l].reshape(GLA_HEADS, 1, GLA_DV), B, S)
        o_dil = _dil(proj, cs, sn, B, S)
        x1 = _outproj(o_gla, o_dil, w_out[l].astype(BF16), xt, mod3, S)
        qp = _pq(x1, g_norm_ffn[l].reshape(1, D), mod3, w_peer_q[l].astype(BF16), S)
        idx_t, gate_t = _topk(qp, peer_sub_keys[l].astype(BF16))
        idx_flat = idx_t.T.reshape(-1)
        E = peer_u.shape[1]
        u3 = peer_u[l].reshape(E, D // LANE, LANE)
        v3 = peer_v[l].reshape(E, D // LANE, LANE)
        steps = T // _PEER_TB // _PEER_CHUNKS
        outs = []
        for ck in range(_PEER_CHUNKS):
            wgt = _peer_u(idx_flat, gate_t, x1, g_norm_ffn[l].reshape(1, D), mod3, u3, S, ck * steps, steps)
            outs.append(_sc_peer_v(v3, idx_flat, wgt, ck * steps * _PEER_TB))
        xt = _peer_fin(x1, jnp.concatenate(outs, axis=0), mod3, g_final.reshape(1, D), S)
    return xt.reshape(B, S, D)
```

```python
import functools

import jax
import jax.numpy as jnp
from jax import lax
from jax.experimental import pallas as pl
from jax.experimental.pallas import tpu as pltpu
from jax.experimental.pallas import tpu_sc as plsc

F32 = jnp.float32
BF16 = jnp.bfloat16
HIGHEST = lax.Precision.HIGHEST

NORM_EPS = 1e-6
GLA_HEADS = 4
GLA_DK = 128
GLA_DV = 256
GLA_GATE_RANK = 16
GLA_TAU = 16.0
GLA_CHUNK = 64
DIL_HD = 128
DIL_HEADS = 8
DIL_PATTERNS = ((128, 1), (512, 4), (2048, 16))
ROPE_THETA = 10000.0
NEG_INF = -1e30
PEER_HEADS = 8
PEER_NKEYS = 128
PEER_TOPK = 16
PEER_HALF = 128

LANE = 128
MIB = 1024 * 1024

_COL_GQ, _COL_GK, _COL_GV, _COL_GR, _COL_DQ, _COL_DK, _COL_DV = 0, 4, 8, 16, 24, 32, 40
_PROJ_W = 48 * LANE

_NT = (((1,), (1,)), ((), ()))
_TN = (((0,), (0,)), ((), ()))


def _params(sem, vmem_mib):
    return pltpu.CompilerParams(dimension_semantics=sem, vmem_limit_bytes=vmem_mib * MIB)


def _rms(x, g):
    return x * lax.rsqrt(jnp.mean(x * x, axis=-1, keepdims=True) + NORM_EPS) * g


def _silu(x):
    return x / (1.0 + jnp.exp(-x))


def _ada_kernel(c_ref, w_ref, b_ref, o_ref):
    s = _silu(c_ref[...]).astype(BF16)
    o_ref[...] = jnp.dot(s, w_ref[...].astype(BF16), preferred_element_type=F32) + b_ref[...]


def _ada(c, w, b):
    B, D = c.shape
    N = w.shape[1]
    tn = 1024
    cp = jnp.zeros((8, D), F32).at[:B].set(c)
    out = pl.pallas_call(
        _ada_kernel,
        grid=(N // tn,),
        in_specs=[pl.BlockSpec((8, D), lambda j: (0, 0)),
                  pl.BlockSpec((D, tn), lambda j: (0, j)),
                  pl.BlockSpec((1, tn), lambda j: (0, j))],
        out_specs=pl.BlockSpec((8, tn), lambda j: (0, j)),
        out_shape=jax.ShapeDtypeStruct((8, N), F32),
        compiler_params=_params(("parallel",), 40),
        name="ada",
    )(cp, w, b.reshape(1, N))
    return out[:B]


def _inproj_kernel(x_ref, g_ref, sc_ref, sh_ref, w_ref, wz_ref, o_ref, z_ref, h_scr):
    @pl.when(pl.program_id(1) == 0)
    def _():
        h = _rms(x_ref[...], g_ref[...]) * (1.0 + sc_ref[0]) + sh_ref[0]
        hb = h.astype(BF16)
        h_scr[...] = hb
        z_ref[...] = jnp.dot(hb, wz_ref[...], preferred_element_type=F32)

    o_ref[...] = jnp.dot(h_scr[...], w_ref[...], preferred_element_type=F32)


def _inproj(x2, g, mod3, w_main, w_z, S):
    T, D = x2.shape
    tm, tn = 1024, 768
    per_b = S // tm
    return pl.pallas_call(
        _inproj_kernel,
        grid=(T // tm, _PROJ_W // tn),
        in_specs=[pl.BlockSpec((tm, D), lambda i, j: (i, 0)),
                  pl.BlockSpec((1, D), lambda i, j: (0, 0)),
                  pl.BlockSpec((1, 1, D), lambda i, j: ((i // per_b) * 6 + 1, 0, 0)),
                  pl.BlockSpec((1, 1, D), lambda i, j: ((i // per_b) * 6 + 0, 0, 0)),
                  pl.BlockSpec((D, tn), lambda i, j: (0, j)),
                  pl.BlockSpec((D, LANE), lambda i, j: (0, 0))],
        out_specs=[pl.BlockSpec((tm, tn), lambda i, j: (i, j)),
                   pl.BlockSpec((tm, LANE), lambda i, j: (i, 0))],
        out_shape=[jax.ShapeDtypeStruct((T, _PROJ_W), F32),
                   jax.ShapeDtypeStruct((T, LANE), F32)],
        scratch_shapes=[pltpu.VMEM((tm, D), BF16)],
        compiler_params=_params(("parallel", "arbitrary"), 48),
        name="inproj",
    )(x2, g, mod3, mod3, w_main, w_z)


def _rope_kernel(pos_ref, f_ref, sg_ref, cs_ref, sn_ref):
    ang = pos_ref[...].astype(F32) * f_ref[...]
    cs_ref[...] = jnp.cos(ang)
    sn_ref[...] = jnp.sin(ang) * sg_ref[...]


def _rope_tables(pos_col):
    T = pos_col.shape[0]
    half = DIL_HD // 2
    inv = jnp.power(ROPE_THETA, -jnp.arange(half, dtype=F32) * 2.0 / DIL_HD)
    freq = jnp.concatenate([inv, inv]).reshape(1, DIL_HD)
    sign = jnp.concatenate([-jnp.ones((half,), F32), jnp.ones((half,), F32)]).reshape(1, DIL_HD)
    tm = 1024
    return pl.pallas_call(
        _rope_kernel,
        grid=(T // tm,),
        in_specs=[pl.BlockSpec((tm, 1), lambda i: (i, 0)),
                  pl.BlockSpec((1, DIL_HD), lambda i: (0, 0)),
                  pl.BlockSpec((1, DIL_HD), lambda i: (0, 0))],
        out_specs=[pl.BlockSpec((tm, DIL_HD), lambda i: (i, 0)),
                   pl.BlockSpec((tm, DIL_HD), lambda i: (i, 0))],
        out_shape=[jax.ShapeDtypeStruct((T, DIL_HD), F32)] * 2,
        compiler_params=_params(("parallel",), 32),
        name="rope",
    )(pos_col, freq, sign)


def _gla_kernel(q_ref, k_ref, v_ref, r_ref, z_ref, wgf_ref, bgf_ref, wgb_ref, bgb_ref, g_ref,
                o_ref, laf_scr, lab_scr, of_scr):
    S = q_ref.shape[0]
    C = GLA_CHUNK
    n = S // C
    scale = GLA_DK ** -0.5
    row = lax.broadcasted_iota(jnp.int32, (C, C), 0)
    col = lax.broadcasted_iota(jnp.int32, (C, C), 1)

    def log_gate(w_ref, b_ref):
        zz = jnp.dot(z_ref[...], w_ref[0], precision=HIGHEST, preferred_element_type=F32) + b_ref[0]
        return (jnp.minimum(zz, 0.0) - jnp.log(1.0 + jnp.exp(-jnp.abs(zz)))) * (1.0 / GLA_TAU)

    def chunk(fwd, c, st_t, la_scr, o_scr):
        keep = (col <= row) if fwd else (col >= row)
        sl = pl.ds(pl.multiple_of(c * C, C), C)
        cum = jnp.dot(keep.astype(F32), la_scr[sl, :], precision=HIGHEST, preferred_element_type=F32)
        tot = cum[C - 1:C, :] if fwd else cum[0:1, :]
        kk = k_ref[sl, :]
        qd = (q_ref[sl, :] * scale * jnp.exp(cum)).astype(BF16)
        ki = (kk * jnp.exp(-cum)).astype(BF16)
        kte = (kk * jnp.exp(tot - cum)).astype(BF16)
        vb = v_ref[sl, :].astype(BF16)
        attn = lax.dot_general(qd, ki, _NT, preferred_element_type=F32)
        attn = jnp.where(keep, attn, 0.0).astype(BF16)
        o = jnp.dot(attn, vb, preferred_element_type=F32)
        o_scr[sl, :] = o + lax.dot_general(qd, st_t.astype(BF16), _NT, preferred_element_type=F32)
        upd = lax.dot_general(vb, kte, _TN, preferred_element_type=F32)
        return st_t * jnp.exp(tot) + upd

    laf_scr[...] = log_gate(wgf_ref, bgf_ref)
    lab_scr[...] = log_gate(wgb_ref, bgb_ref)

    def body(i, states):
        return (chunk(True, i, states[0], laf_scr, of_scr),
                chunk(False, n - 1 - i, states[1], lab_scr, o_ref))

    zero = jnp.zeros((GLA_DV, GLA_DK), F32)
    lax.fori_loop(0, n, body, (zero, zero), unroll=4)
    y = _rms(of_scr[...] + o_ref[...], g_ref[0])
    o_ref[...] = y * _silu(r_ref[...])


def _gla(proj, gz, wgf, bgf, wgb, bgb, g_out, B, S):
    T = proj.shape[0]
    H = GLA_HEADS
    return pl.pallas_call(
        _gla_kernel,
        grid=(B, H),
        in_specs=[pl.BlockSpec((S, GLA_DK), lambda b, h: (b, _COL_GQ + h)),
                  pl.BlockSpec((S, GLA_DK), lambda b, h: (b, _COL_GK + h)),
                  pl.BlockSpec((S, GLA_DV), lambda b, h: (b, _COL_GV // 2 + h)),
                  pl.BlockSpec((S, GLA_DV), lambda b, h: (b, _COL_GR // 2 + h)),
                  pl.BlockSpec((S, LANE), lambda b, h: (b, 0)),
                  pl.BlockSpec((1, LANE, GLA_DK), lambda b, h: (h, 0, 0)),
                  pl.BlockSpec((1, 1, GLA_DK), lambda b, h: (h, 0, 0)),
                  pl.BlockSpec((1, LANE, GLA_DK), lambda b, h: (h, 0, 0)),
                  pl.BlockSpec((1, 1, GLA_DK), lambda b, h: (h, 0, 0)),
                  pl.BlockSpec((1, 1, GLA_DV), lambda b, h: (h, 0, 0))],
        out_specs=pl.BlockSpec((S, GLA_DV), lambda b, h: (b, h)),
        out_shape=jax.ShapeDtypeStruct((T, GLA_HEADS * GLA_DV), F32),
        scratch_shapes=[pltpu.VMEM((S, GLA_DK), F32), pltpu.VMEM((S, GLA_DK), F32),
                        pltpu.VMEM((S, GLA_DV), F32)],
        compiler_params=_params(("parallel", "parallel"), 56),
        name="gla",
    )(proj, proj, proj, proj, gz, wgf, bgf, wgb, bgb, g_out)


_DIL_QB = 128
_DIL_HALF = 64
_DIL_KW = _DIL_QB + 2 * _DIL_HALF


def _dil_kernel(q_ref, k_ref, v_ref, cs_ref, sn_ref, o_ref, qf_scr, kf_scr, qb_scr, kb_scr, vb_scr, lse_scr):
    S = q_ref.shape[0]
    QB, KW, HALF = _DIL_QB, _DIL_KW, _DIL_HALF
    rot = DIL_HD // 2
    cs, sn = cs_ref[...], sn_ref[...]
    q = q_ref[...]
    qf_scr[...] = (q * cs + pltpu.roll(q, rot, 1) * sn) * (DIL_HD ** -0.5)
    k = k_ref[...]
    kf_scr[...] = k * cs + pltpu.roll(k, rot, 1) * sn
    rel = (lax.broadcasted_iota(jnp.int32, (QB, KW), 1) - lax.broadcasted_iota(jnp.int32, (QB, KW), 0))

    for pi, (window, d) in enumerate(DIL_PATTERNS):
        assert window // (2 * d) == HALF
        M = S // d
        per = M // QB
        for r in range(d):
            rows = pl.ds(r * M, M)
            src = pl.ds(r, M, stride=d) if d > 1 else pl.ds(0, M)
            qb_scr[rows, :] = qf_scr[src, :].astype(BF16)
            kb_scr[rows, :] = kf_scr[src, :].astype(BF16)
            vb_scr[rows, :] = v_ref[src, :].astype(BF16)

        for r in range(d):
            def block(mb, carry, r=r, first=(pi == 0)):
                m0 = mb * QB
                start = jnp.clip(m0 - HALF, 0, M - KW)
                qrow = pl.multiple_of(r * M + m0, QB)
                krow = pl.multiple_of(r * M + start, HALF)
                s = lax.dot_general(qb_scr[pl.ds(qrow, QB), :], kb_scr[pl.ds(krow, KW), :], _NT,
                                    preferred_element_type=F32)
                s = jnp.where(jnp.abs(rel + (start - m0)) <= HALF, s, NEG_INF)
                m = jnp.max(s, axis=-1, keepdims=True)
                p = jnp.exp(s - m)
                l = jnp.sum(p, axis=-1, keepdims=True)
                o = jnp.dot(p.astype(BF16), vb_scr[pl.ds(krow, KW), :], preferred_element_type=F32) / l
                ls = jnp.broadcast_to(m + jnp.log(l), (QB, DIL_HD))
                base = pl.multiple_of(m0 * d, QB)
                nat = pl.ds(base + r, QB, stride=d) if d > 1 else pl.ds(base, QB)
                if first:
                    o_ref[nat, :] = o
                    lse_scr[nat, :] = ls
                else:
                    o0, l0 = o_ref[nat, :], lse_scr[nat, :]
                    mx = jnp.maximum(l0, ls)
                    w0, w1 = jnp.exp(l0 - mx), jnp.exp(ls - mx)
                    den = w0 + w1
                    o_ref[nat, :] = (w0 * o0 + w1 * o) / den
                    lse_scr[nat, :] = mx + jnp.log(den)
                return carry

            lax.fori_loop(0, per, block, 0, unroll=min(per, 4))


def _dil(proj, cs, sn, B, S):
    T = proj.shape[0]
    blk = lambda col: pl.BlockSpec((S, DIL_HD), lambda b, h: (b, col + h))
    return pl.pallas_call(
        _dil_kernel,
        grid=(B, DIL_HEADS),
        in_specs=[blk(_COL_DQ), blk(_COL_DK), blk(_COL_DV),
                  pl.BlockSpec((S, DIL_HD), lambda b, h: (b, 0)),
                  pl.BlockSpec((S, DIL_HD), lambda b, h: (b, 0))],
        out_specs=pl.BlockSpec((S, DIL_HD), lambda b, h: (b, h)),
        out_shape=jax.ShapeDtypeStruct((T, DIL_HEADS * DIL_HD), F32),
        scratch_shapes=[pltpu.VMEM((S, DIL_HD), F32), pltpu.VMEM((S, DIL_HD), F32),
                        pltpu.VMEM((S, DIL_HD), BF16), pltpu.VMEM((S, DIL_HD), BF16),
                        pltpu.VMEM((S, DIL_HD), BF16), pltpu.VMEM((S, DIL_HD), F32)],
        compiler_params=_params(("parallel", "parallel"), 48),
        name="dil",
    )(proj, proj, proj, cs, sn)


def _outproj_kernel(og_ref, od_ref, w_ref, x_ref, ga_ref, o_ref):
    kg = og_ref.shape[1]
    mixed = jnp.dot(og_ref[...].astype(BF16), w_ref[:kg, :], preferred_element_type=F32)
    mixed = mixed + jnp.dot(od_ref[...].astype(BF16), w_ref[kg:, :], preferred_element_type=F32)
    o_ref[...] = x_ref[...] + ga_ref[0] * mixed


def _outproj(o_gla, o_dil, w_out, x2, mod3, S):
    T, D = x2.shape
    tm = 512
    per_b = S // tm
    kg, kd = o_gla.shape[1], o_dil.shape[1]
    return pl.pallas_call(
        _outproj_kernel,
        grid=(T // tm,),
        in_specs=[pl.BlockSpec((tm, kg), lambda i: (i, 0)),
                  pl.BlockSpec((tm, kd), lambda i: (i, 0)),
                  pl.BlockSpec((kg + kd, D), lambda i: (0, 0)),
                  pl.BlockSpec((tm, D), lambda i: (i, 0)),
                  pl.BlockSpec((1, 1, D), lambda i: ((i // per_b) * 6 + 2, 0, 0))],
        out_specs=pl.BlockSpec((tm, D), lambda i: (i, 0)),
        out_shape=jax.ShapeDtypeStruct((T, D), F32),
        compiler_params=_params(("parallel",), 48),
        name="outproj",
    )(o_gla, o_dil, w_out, x2, mod3)


def _pq_kernel(x_ref, g_ref, sc_ref, sh_ref, w_ref, o_ref):
    h = _rms(x_ref[...], g_ref[...]) * (1.0 + sc_ref[0]) + sh_ref[0]
    o_ref[...] = jnp.dot(h.astype(BF16), w_ref[...], preferred_element_type=F32)


def _pq(x1, g, mod3, wq, S):
    T, D = x1.shape
    N = wq.shape[1]
    tm = 512
    per_b = S // tm
    return pl.pallas_call(
        _pq_kernel,
        grid=(T // tm,),
        in_specs=[pl.BlockSpec((tm, D), lambda i: (i, 0)),
                  pl.BlockSpec((1, D), lambda i: (0, 0)),
                  pl.BlockSpec((1, 1, D), lambda i: ((i // per_b) * 6 + 4, 0, 0)),
                  pl.BlockSpec((1, 1, D), lambda i: ((i // per_b) * 6 + 3, 0, 0)),
                  pl.BlockSpec((D, N), lambda i: (0, 0))],
        out_specs=pl.BlockSpec((tm, N), lambda i: (i, 0)),
        out_shape=jax.ShapeDtypeStruct((T, N), F32),
        compiler_params=_params(("parallel",), 48),
        name="pq",
    )(x1, g, mod3, mod3, wq)


def _top_rows(s, k, payload=None):
    n_rows = s.shape[0]
    rid = lax.broadcasted_iota(jnp.int32, s.shape, 0).astype(F32)
    vals, picks = [], []
    for _ in range(k):
        m = jnp.max(s, axis=0, keepdims=True)
        pos = jnp.min(jnp.where(s == m, rid, float(n_rows)), axis=0, keepdims=True)
        hit = rid == pos
        vals.append(m)
        if payload is None:
            picks.append(pos)
        else:
            picks.append(jnp.sum(jnp.where(hit, payload, 0), axis=0, keepdims=True))
        s = jnp.where(hit, -jnp.inf, s)
    picks = jnp.concatenate(picks, axis=0)
    return jnp.concatenate(vals, axis=0), picks.astype(jnp.int32)


def _staircase(a, b, combine, fill):
    K = a.shape[0]
    half = K // 2
    jrow = lax.broadcasted_iota(jnp.int32, (half, a.shape[1]), 0)
    pieces = [combine(a[0:1], b)]
    for i in range(1, half):
        piece = combine(a[i:i + 1], b[0:half])
        width = K // (i + 1)
        pieces.append(piece if width >= half else jnp.where(jrow < width, piece, fill))
    pieces.append(combine(a[half:K], b[0:1]))
    return jnp.concatenate(pieces, axis=0)


def _topk_kernel(q_ref, keys_ref, idx_ref, gate_ref):
    K = PEER_TOPK
    for h in range(PEER_HEADS):
        tops = []
        for half in range(2):
            c0 = (h * 2 + half) * PEER_HALF
            qh = q_ref[:, c0:c0 + PEER_HALF].astype(BF16)
            sc = lax.dot_general(keys_ref[h, half], qh, _NT, preferred_element_type=F32)
            tops.append(_top_rows(sc, K))
        (s0, i0), (s1, i1) = tops
        cand_s = _staircase(s0, s1, lambda a, b: a + b, -jnp.inf)
        cand_i = _staircase(i0, i1, lambda a, b: a * PEER_NKEYS + b, 0)
        best, idx = _top_rows(cand_s, K, payload=cand_i)
        e = jnp.exp(best - best[0:1])
        gate = e / jnp.sum(e, axis=0, keepdims=True)
        idx_ref[h * K:(h + 1) * K, :] = idx
        gate_ref[h * K:(h + 1) * K, :] = gate


def _topk(qp, keys_bf):
    T, N = qp.shape
    tt = 256
    HK = PEER_HEADS * PEER_TOPK
    return pl.pallas_call(
        _topk_kernel,
        grid=(T // tt,),
        in_specs=[pl.BlockSpec((tt, N), lambda i: (i, 0)),
                  pl.BlockSpec(keys_bf.shape, lambda i: (0, 0, 0, 0))],
        out_specs=[pl.BlockSpec((HK, tt), lambda i: (0, i)),
                   pl.BlockSpec((HK, tt), lambda i: (0, i))],
        out_shape=[jax.ShapeDtypeStruct((HK, T), jnp.int32),
                   jax.ShapeDtypeStruct((HK, T), F32)],
        compiler_params=_params(("parallel",), 32),
        name="topk",
    )(qp, keys_bf)


_PEER_TB = 128
_PEER_SUB = 8
_SUBLANES = 8
_PEER_CHUNKS = 16
_SC_LANES = 16
_SC_ROWS = 16


def _peer_u_kernel(idx_hbm, gate_ref, x1_ref, gn_ref, sc_ref, sh_ref, u_hbm, w_ref,
                   idx_smem, ub0, ub1, h_scr, sem_i, sem_u, *, step0):
    HK = PEER_HEADS * PEER_TOPK
    TB, SUB = _PEER_TB, _PEER_SUB
    R = SUB * HK
    N = TB * HK
    nsub = TB // SUB
    D = x1_ref.shape[1]
    nchunk = D // LANE
    tiles = HK // _SUBLANES
    i = pl.program_id(0)
    n = pl.num_programs(0)
    cur = lax.rem(i, 2) * N
    nxt = N - cur
    more = i + 1 < n
    ubufs = (ub0, ub1)

    def idx_copy(step, base):
        return pltpu.make_async_copy(idx_hbm.at[pl.ds((step0 + step) * N, N)],
                                     idx_smem.at[pl.ds(base, N)], sem_i)

    def issue_token(base, t, slot):
        for k in range(HK):
            e = idx_smem[base + t * HK + k]
            rt, s = t * tiles + k // _SUBLANES, k % _SUBLANES
            pltpu.make_async_copy(u_hbm.at[e], ubufs[slot].at[rt, :, s, :],
                                  sem_u.at[slot]).start(priority=k % 2)

    def wait(slot):
        pltpu.make_async_copy(ubufs[slot], ubufs[slot], sem_u.at[slot]).wait()

    @pl.when(i == 0)
    def _():
        first = idx_copy(0, 0)
        first.start()
        first.wait()
        for t in range(SUB):
            issue_token(0, t, 0)

    @pl.when(more)
    def _():
        idx_copy(i + 1, nxt).start()

    h_scr[...] = _rms(x1_ref[...], gn_ref[...]) * (1.0 + sc_ref[0]) + sh_ref[0]
    lane = lax.broadcasted_iota(jnp.int32, (HK, TB), 1)

    def compute_token(j, t, slot, wacc):
        ub = ubufs[slot]
        tok = j * SUB + t
        xt = h_scr[pl.ds(tok, 1), :]
        rows = slice(t * tiles, (t + 1) * tiles)
        part = ub[rows, 0].reshape(HK, LANE) * xt[:, 0:LANE]
        for c in range(1, nchunk):
            part = part + ub[rows, c].reshape(HK, LANE) * xt[:, c * LANE:(c + 1) * LANE]
        a = jnp.sum(part, axis=1, keepdims=True)
        hit = lane == tok
        g = jnp.sum(jnp.where(hit, gate_ref[...], 0.0), axis=1, keepdims=True)
        wgt = g * (0.5 * a * (1.0 + lax.erf(a * (2.0 ** -0.5))))
        return jnp.where(hit, wgt, wacc)

    def half(j, slot, next_base, wacc):
        wait(slot)
        for t in range(SUB):
            issue_token(next_base, t, 1 - slot)
            wacc = compute_token(j, t, slot, wacc)
        return wacc

    def pair(jj, wacc):
        j0 = 2 * jj
        wacc = half(j0, 0, cur + (j0 + 1) * R, wacc)
        last = jj == nsub // 2 - 1

        @pl.when(jnp.logical_and(last, more))
        def _():
            idx_copy(i + 1, nxt).wait()

        after = jnp.where(more, nxt, cur)
        return half(j0 + 1, 1, jnp.where(last, after, cur + (j0 + 2) * R), wacc)

    wacc = lax.fori_loop(0, nsub // 2, pair, jnp.zeros((HK, TB), F32))

    @pl.when(jnp.logical_not(more))
    def _():
        wait(0)

    w_ref[:, :HK] = jnp.zeros((TB, HK), F32)
    w_ref[:, HK:] = wacc.T


def _peer_u(idx_flat, gate_t, x1, g_norm, mod3, u3, S, step0, nsteps):
    T, D = x1.shape
    HK = PEER_HEADS * PEER_TOPK
    TB, SUB = _PEER_TB, _PEER_SUB
    per_b = S // TB
    modrow = lambda k: (lambda i: (((step0 + i) // per_b) * 6 + k, 0, 0))
    gbuf = pltpu.VMEM((SUB * HK // _SUBLANES, D // LANE, _SUBLANES, LANE), F32)
    return pl.pallas_call(
        functools.partial(_peer_u_kernel, step0=step0),
        grid=(nsteps,),
        in_specs=[pl.BlockSpec(memory_space=pl.ANY),
                  pl.BlockSpec((HK, TB), lambda i: (0, step0 + i)),
                  pl.BlockSpec((TB, D), lambda i: (step0 + i, 0)),
                  pl.BlockSpec((1, D), lambda i: (0, 0)),
                  pl.BlockSpec((1, 1, D), modrow(4)),
                  pl.BlockSpec((1, 1, D), modrow(3)),
                  pl.BlockSpec(memory_space=pl.ANY)],
        out_specs=pl.BlockSpec((TB, 2 * HK), lambda i: (i, 0)),
        out_shape=jax.ShapeDtypeStruct((nsteps * TB, 2 * HK), F32),
        scratch_shapes=[pltpu.SMEM((2 * TB * HK,), jnp.int32),
                        gbuf, gbuf,
                        pltpu.VMEM((TB, D), F32),
                        pltpu.SemaphoreType.DMA,
                        pltpu.SemaphoreType.DMA((2,))],
        compiler_params=_params(("arbitrary",), 40),
        name="peer_u",
    )(idx_flat, gate_t, x1, g_norm, mod3, mod3, u3)


def _sc_peer_v(v_tab, idx_flat, wgt, tok_base):
    E, nblk, _ = v_tab.shape
    D = nblk * LANE
    Tc, HK = wgt.shape[0], wgt.shape[1] // 2
    info = plsc.get_sparse_core_info()
    nw = info.num_cores * info.num_subcores
    tpw = Tc // nw
    CH = _SC_ROWS
    nch = HK // CH
    nsl = D // _SC_LANES
    mesh = plsc.VectorSubcoreMesh(core_axis_name="c", subcore_axis_name="s")

    @functools.partial(
        pl.kernel, mesh=mesh, out_type=jax.ShapeDtypeStruct((Tc, D), F32),
        scratch_types=[pltpu.VMEM((tpw * HK,), jnp.int32), pltpu.VMEM((2 * HK,), F32),
                       pltpu.VMEM((D,), F32),
                       pltpu.VMEM((CH, nblk, LANE), F32), pltpu.VMEM((CH, nblk, LANE), F32),
                       pltpu.SemaphoreType.DMA, pltpu.SemaphoreType.DMA],
        compiler_params=pltpu.CompilerParams(needs_layout_passes=False),
        name="sc_peer_v",
    )
    def k(tab_hbm, idx_hbm, w_hbm, o_hbm, idx_v, w_v, o_v, buf0, buf1, g0, g1):
        wid = lax.axis_index("s") * info.num_cores + lax.axis_index("c")
        tok0 = wid * tpw
        pltpu.sync_copy(idx_hbm.at[pl.ds((tok_base + tok0) * HK, tpw * HK)], idx_v)
        bufs, gs = (buf0, buf1), (g0, g1)

        def gather(g, b):
            return pltpu.make_async_copy(tab_hbm.at[idx_v.at[pl.ds(g * CH, CH)]], bufs[b], gs[b])

        gather(0, 0).start()

        @pl.loop(0, tpw)
        def _(t):
            pltpu.sync_copy(w_hbm.at[tok0 + t], w_v)

            @pl.loop(0, nsl)
            def _(c):
                o_v[pl.ds(pl.multiple_of(c * _SC_LANES, _SC_LANES), _SC_LANES)] = jnp.zeros((_SC_LANES,), F32)

            for ch in range(nch):
                b = ch % 2
                g = t * nch + ch
                gather(g, b).wait()

                @pl.when(g + 1 < tpw * nch)
                def _():
                    gather(g + 1, 1 - b).start()

                ws = [plsc.load_gather(w_v, [jnp.full((_SC_LANES,), HK + ch * CH + r, jnp.int32)])
                      for r in range(CH)]

                @plsc.parallel_loop(0, nsl, step=2)
                def _(c):
                    for half in range(2):
                        off = pl.multiple_of((c + half) * _SC_LANES, _SC_LANES)
                        blk = (c + half) // (LANE // _SC_LANES)
                        lo = pl.multiple_of(off - blk * LANE, _SC_LANES)
                        parts = [ws[r] * bufs[b][r, blk, pl.ds(lo, _SC_LANES)] for r in range(CH)]
                        while len(parts) > 1:
                            parts = [parts[p] + parts[p + 1] for p in range(0, len(parts), 2)]
                        o_v[pl.ds(off, _SC_LANES)] = o_v[pl.ds(off, _SC_LANES)] + parts[0]

            pltpu.sync_copy(o_v, o_hbm.at[tok0 + t])

    return k(v_tab, idx_flat, wgt)


def _peer_fin_kernel(x_ref, p_ref, ga_ref, gf_ref, o_ref):
    o_ref[...] = _rms(x_ref[...] + ga_ref[0] * p_ref[...], gf_ref[...])


def _peer_fin(x1, po, mod3, g_final, S):
    T, D = x1.shape
    tm = 512
    per_b = S // tm
    return pl.pallas_call(
        _peer_fin_kernel,
        grid=(T // tm,),
        in_specs=[pl.BlockSpec((tm, D), lambda i: (i, 0)),
                  pl.BlockSpec((tm, D), lambda i: (i, 0)),
                  pl.BlockSpec((1, 1, D), lambda i: ((i // per_b) * 6 + 5, 0, 0)),
                  pl.BlockSpec((1, D), lambda i: (0, 0))],
        out_specs=pl.BlockSpec((tm, D), lambda i: (i, 0)),
        out_shape=jax.ShapeDtypeStruct((T, D), F32),
        compiler_params=_params(("parallel",), 40),
        name="peer_fin",
    )(x1, po, mod3, g_final)


def _pad_gate(w, lo):
    rank = w.shape[0]
    wh = w.reshape(rank, GLA_HEADS, GLA_DK).transpose(1, 0, 2)
    return jnp.zeros((GLA_HEADS, LANE, GLA_DK), F32).at[:, lo:lo + rank, :].set(wh)


def kernel(x, c, positions, w_ada, b_ada, g_norm_mix, w_in, w_gate_f, b_gate_f, w_gate_b, b_gate_b,
           g_gla_out, w_out, g_norm_ffn, w_peer_q, peer_sub_keys, peer_u, peer_v, g_final):
    B, S, D = x.shape
    T = B * S
    depth = w_ada.shape[0]
    assert depth == 1, "the final norm is fused into the last PEER call; one layer only"
    xt = x.reshape(T, D)
    cs, sn = _rope_tables(positions.reshape(T, 1))
    gz0 = 2 * GLA_HEADS * GLA_DK + 2 * GLA_HEADS * GLA_DV
    gz1 = gz0 + 2 * GLA_GATE_RANK
    for l in range(depth):
        mod3 = _ada(c, w_ada[l], b_ada[l]).reshape(B * 6, 1, D)
        w_main = jnp.concatenate([w_in[l][:, :gz0], w_in[l][:, gz1:]], axis=1).astype(BF16)
        w_z = jnp.pad(w_in[l][:, gz0:gz1], ((0, 0), (0, LANE - (gz1 - gz0)))).astype(BF16)
        proj, gz = _inproj(xt, g_norm_mix[l].reshape(1, D), mod3, w_main, w_z, S)
        o_gla = _gla(proj, gz,
                     _pad_gate(w_gate_f[l], 0), b_gate_f[l].reshape(GLA_HEADS, 1, GLA_DK),
                     _pad_gate(w_gate_b[l], GLA_GATE_RANK), b_gate_b[l].reshape(GLA_HEADS, 1, GLA_DK),
                     g_gla_out[l].reshape(GLA_HEADS, 1, GLA_DV), B, S)
        o_dil = _dil(proj, cs, sn, B, S)
        x1 = _outproj(o_gla, o_dil, w_out[l].astype(BF16), xt, mod3, S)
        qp = _pq(x1, g_norm_ffn[l].reshape(1, D), mod3, w_peer_q[l].astype(BF16), S)
        idx_t, gate_t = _topk(qp, peer_sub_keys[l].astype(BF16))
        idx_flat = idx_t.T.reshape(-1)
        E = peer_u.shape[1]
        u3 = peer_u[l].reshape(E, D // LANE, LANE)
        v3 = peer_v[l].reshape(E, D // LANE, LANE)
        steps = T // _PEER_TB // _PEER_CHUNKS
        outs = []
        for ck in range(_PEER_CHUNKS):
            wgt = _peer_u(idx_flat, gate_t, x1, g_norm_ffn[l].reshape(1, D), mod3, u3, S, ck * steps, steps)
            outs.append(_sc_peer_v(v3, idx_flat, wgt, ck * steps * _PEER_TB))
        xt = _peer_fin(x1, jnp.concatenate(outs, axis=0), mod3, g_final.reshape(1, D), S)
    return xt.reshape(B, S, D)
```

```python
import functools

import jax
import jax.numpy as jnp
from jax import lax
from jax.experimental import pallas as pl
from jax.experimental.pallas import tpu as pltpu
from jax.experimental.pallas import tpu_sc as plsc

F32 = jnp.float32
BF16 = jnp.bfloat16
HIGHEST = lax.Precision.HIGHEST

NORM_EPS = 1e-6
GLA_HEADS = 4
GLA_DK = 128
GLA_DV = 256
GLA_GATE_RANK = 16
GLA_TAU = 16.0
GLA_CHUNK = 64
DIL_HD = 128
DIL_HEADS = 8
DIL_PATTERNS = ((128, 1), (512, 4), (2048, 16))
ROPE_THETA = 10000.0
NEG_INF = -1e30
PEER_HEADS = 8
PEER_NKEYS = 128
PEER_TOPK = 16
PEER_HALF = 128

LANE = 128
MIB = 1024 * 1024

_COL_GQ, _COL_GK, _COL_GV, _COL_GR, _COL_DQ, _COL_DK, _COL_DV = 0, 4, 8, 16, 24, 32, 40
_PROJ_W = 48 * LANE

_NT = (((1,), (1,)), ((), ()))
_TN = (((0,), (0,)), ((), ()))


def _params(sem, vmem_mib):
    return pltpu.CompilerParams(dimension_semantics=sem, vmem_limit_bytes=vmem_mib * MIB)


def _rms(x, g):
    return x * lax.rsqrt(jnp.mean(x * x, axis=-1, keepdims=True) + NORM_EPS) * g


def _silu(x):
    return x / (1.0 + jnp.exp(-x))


def _ada_kernel(c_ref, w_ref, b_ref, o_ref):
    s = _silu(c_ref[...]).astype(BF16)
    o_ref[...] = jnp.dot(s, w_ref[...].astype(BF16), preferred_element_type=F32) + b_ref[...]


def _ada(c, w, b):
    B, D = c.shape
    N = w.shape[1]
    tn = 1024
    cp = jnp.zeros((8, D), F32).at[:B].set(c)
    out = pl.pallas_call(
        _ada_kernel,
        grid=(N // tn,),
        in_specs=[pl.BlockSpec((8, D), lambda j: (0, 0)),
                  pl.BlockSpec((D, tn), lambda j: (0, j)),
                  pl.BlockSpec((1, tn), lambda j: (0, j))],
        out_specs=pl.BlockSpec((8, tn), lambda j: (0, j)),
        out_shape=jax.ShapeDtypeStruct((8, N), F32),
        compiler_params=_params(("parallel",), 40),
        name="ada",
    )(cp, w, b.reshape(1, N))
    return out[:B]


def _inproj_kernel(x_ref, g_ref, sc_ref, sh_ref, w_ref, wz_ref, o_ref, z_ref, h_scr):
    @pl.when(pl.program_id(1) == 0)
    def _():
        h = _rms(x_ref[...], g_ref[...]) * (1.0 + sc_ref[0]) + sh_ref[0]
        hb = h.astype(BF16)
        h_scr[...] = hb
        z_ref[...] = jnp.dot(hb, wz_ref[...], preferred_element_type=F32)

    o_ref[...] = jnp.dot(h_scr[...], w_ref[...], preferred_element_type=F32)


def _inproj(x2, g, mod3, w_main, w_z, S):
    T, D = x2.shape
    tm, tn = 1024, 768
    per_b = S // tm
    return pl.pallas_call(
        _inproj_kernel,
        grid=(T // tm, _PROJ_W // tn),
        in_specs=[pl.BlockSpec((tm, D), lambda i, j: (i, 0)),
                  pl.BlockSpec((1, D), lambda i, j: (0, 0)),
                  pl.BlockSpec((1, 1, D), lambda i, j: ((i // per_b) * 6 + 1, 0, 0)),
                  pl.BlockSpec((1, 1, D), lambda i, j: ((i // per_b) * 6 + 0, 0, 0)),
                  pl.BlockSpec((D, tn), lambda i, j: (0, j)),
                  pl.BlockSpec((D, LANE), lambda i, j: (0, 0))],
        out_specs=[pl.BlockSpec((tm, tn), lambda i, j: (i, j)),
                   pl.BlockSpec((tm, LANE), lambda i, j: (i, 0))],
        out_shape=[jax.ShapeDtypeStruct((T, _PROJ_W), F32),
                   jax.ShapeDtypeStruct((T, LANE), F32)],
        scratch_shapes=[pltpu.VMEM((tm, D), BF16)],
        compiler_params=_params(("parallel", "arbitrary"), 48),
        name="inproj",
    )(x2, g, mod3, mod3, w_main, w_z)


def _rope_kernel(pos_ref, f_ref, sg_ref, cs_ref, sn_ref):
    ang = pos_ref[...].astype(F32) * f_ref[...]
    cs_ref[...] = jnp.cos(ang)
    sn_ref[...] = jnp.sin(ang) * sg_ref[...]


def _rope_tables(pos_col):
    T = pos_col.shape[0]
    half = DIL_HD // 2
    inv = jnp.power(ROPE_THETA, -jnp.arange(half, dtype=F32) * 2.0 / DIL_HD)
    freq = jnp.concatenate([inv, inv]).reshape(1, DIL_HD)
    sign = jnp.concatenate([-jnp.ones((half,), F32), jnp.ones((half,), F32)]).reshape(1, DIL_HD)
    tm = 1024
    return pl.pallas_call(
        _rope_kernel,
        grid=(T // tm,),
        in_specs=[pl.BlockSpec((tm, 1), lambda i: (i, 0)),
                  pl.BlockSpec((1, DIL_HD), lambda i: (0, 0)),
                  pl.BlockSpec((1, DIL_HD), lambda i: (0, 0))],
        out_specs=[pl.BlockSpec((tm, DIL_HD), lambda i: (i, 0)),
                   pl.BlockSpec((tm, DIL_HD), lambda i: (i, 0))],
        out_shape=[jax.ShapeDtypeStruct((T, DIL_HD), F32)] * 2,
        compiler_params=_params(("parallel",), 32),
        name="rope",
    )(pos_col, freq, sign)


def _gla_kernel(q_ref, k_ref, v_ref, r_ref, z_ref, wgf_ref, bgf_ref, wgb_ref, bgb_ref, g_ref,
                o_ref, laf_scr, lab_scr, of_scr):
    S = q_ref.shape[0]
    C = GLA_CHUNK
    n = S // C
    scale = GLA_DK ** -0.5
    row = lax.broadcasted_iota(jnp.int32, (C, C), 0)
    col = lax.broadcasted_iota(jnp.int32, (C, C), 1)

    def log_gate(w_ref, b_ref):
        zz = jnp.dot(z_ref[...], w_ref[0], precision=HIGHEST, preferred_element_type=F32) + b_ref[0]
        return (jnp.minimum(zz, 0.0) - jnp.log(1.0 + jnp.exp(-jnp.abs(zz)))) * (1.0 / GLA_TAU)

    def chunk(fwd, c, st_t, la_scr, o_scr):
        keep = (col <= row) if fwd else (col >= row)
        sl = pl.ds(pl.multiple_of(c * C, C), C)
        cum = jnp.dot(keep.astype(F32), la_scr[sl, :], precision=HIGHEST, preferred_element_type=F32)
        tot = cum[C - 1:C, :] if fwd else cum[0:1, :]
        kk = k_ref[sl, :]
        qd = (q_ref[sl, :] * scale * jnp.exp(cum)).astype(BF16)
        ki = (kk * jnp.exp(-cum)).astype(BF16)
        kte = (kk * jnp.exp(tot - cum)).astype(BF16)
        vb = v_ref[sl, :].astype(BF16)
        attn = lax.dot_general(qd, ki, _NT, preferred_element_type=F32)
        attn = jnp.where(keep, attn, 0.0).astype(BF16)
        o = jnp.dot(attn, vb, preferred_element_type=F32)
        o_scr[sl, :] = o + lax.dot_general(qd, st_t.astype(BF16), _NT, preferred_element_type=F32)
        upd = lax.dot_general(vb, kte, _TN, preferred_element_type=F32)
        return st_t * jnp.exp(tot) + upd

    laf_scr[...] = log_gate(wgf_ref, bgf_ref)
    lab_scr[...] = log_gate(wgb_ref, bgb_ref)

    def body(i, states):
        return (chunk(True, i, states[0], laf_scr, of_scr),
                chunk(False, n - 1 - i, states[1], lab_scr, o_ref))

    zero = jnp.zeros((GLA_DV, GLA_DK), F32)
    lax.fori_loop(0, n, body, (zero, zero), unroll=4)
    y = _rms(of_scr[...] + o_ref[...], g_ref[0])
    o_ref[...] = y * _silu(r_ref[...])


def _gla(proj, gz, wgf, bgf, wgb, bgb, g_out, B, S):
    T = proj.shape[0]
    H = GLA_HEADS
    return pl.pallas_call(
        _gla_kernel,
        grid=(B, H),
        in_specs=[pl.BlockSpec((S, GLA_DK), lambda b, h: (b, _COL_GQ + h)),
                  pl.BlockSpec((S, GLA_DK), lambda b, h: (b, _COL_GK + h)),
                  pl.BlockSpec((S, GLA_DV), lambda b, h: (b, _COL_GV // 2 + h)),
                  pl.BlockSpec((S, GLA_DV), lambda b, h: (b, _COL_GR // 2 + h)),
                  pl.BlockSpec((S, LANE), lambda b, h: (b, 0)),
                  pl.BlockSpec((1, LANE, GLA_DK), lambda b, h: (h, 0, 0)),
                  pl.BlockSpec((1, 1, GLA_DK), lambda b, h: (h, 0, 0)),
                  pl.BlockSpec((1, LANE, GLA_DK), lambda b, h: (h, 0, 0)),
                  pl.BlockSpec((1, 1, GLA_DK), lambda b, h: (h, 0, 0)),
                  pl.BlockSpec((1, 1, GLA_DV), lambda b, h: (h, 0, 0))],
        out_specs=pl.BlockSpec((S, GLA_DV), lambda b, h: (b, h)),
        out_shape=jax.ShapeDtypeStruct((T, GLA_HEADS * GLA_DV), F32),
        scratch_shapes=[pltpu.VMEM((S, GLA_DK), F32), pltpu.VMEM((S, GLA_DK), F32),
                        pltpu.VMEM((S, GLA_DV), F32)],
        compiler_params=_params(("parallel", "parallel"), 56),
        name="gla",
    )(proj, proj, proj, proj, gz, wgf, bgf, wgb, bgb, g_out)


_DIL_QB = 128
_DIL_HALF = 64
_DIL_KW = _DIL_QB + 2 * _DIL_HALF


def _dil_kernel(q_ref, k_ref, v_ref, cs_ref, sn_ref, o_ref, qf_scr, kf_scr, qb_scr, kb_scr, vb_scr, lse_scr):
    S = q_ref.shape[0]
    QB, KW, HALF = _DIL_QB, _DIL_KW, _DIL_HALF
    rot = DIL_HD // 2
    cs, sn = cs_ref[...], sn_ref[...]
    q = q_ref[...]
    qf_scr[...] = (q * cs + pltpu.roll(q, rot, 1) * sn) * (DIL_HD ** -0.5)
    k = k_ref[...]
    kf_scr[...] = k * cs + pltpu.roll(k, rot, 1) * sn
    rel = (lax.broadcasted_iota(jnp.int32, (QB, KW), 1) - lax.broadcasted_iota(jnp.int32, (QB, KW), 0))

    for pi, (window, d) in enumerate(DIL_PATTERNS):
        assert window // (2 * d) == HALF
        M = S // d
        per = M // QB
        for r in range(d):
            rows = pl.ds(r * M, M)
            src = pl.ds(r, M, stride=d) if d > 1 else pl.ds(0, M)
            qb_scr[rows, :] = qf_scr[src, :].astype(BF16)
            kb_scr[rows, :] = kf_scr[src, :].astype(BF16)
            vb_scr[rows, :] = v_ref[src, :].astype(BF16)

        for r in range(d):
            def block(mb, carry, r=r, first=(pi == 0)):
                m0 = mb * QB
                start = jnp.clip(m0 - HALF, 0, M - KW)
                qrow = pl.multiple_of(r * M + m0, QB)
                krow = pl.multiple_of(r * M + start, HALF)
                s = lax.dot_general(qb_scr[pl.ds(qrow, QB), :], kb_scr[pl.ds(krow, KW), :], _NT,
                                    preferred_element_type=F32)
                s = jnp.where(jnp.abs(rel + (start - m0)) <= HALF, s, NEG_INF)
                m = jnp.max(s, axis=-1, keepdims=True)
                p = jnp.exp(s - m)
                l = jnp.sum(p, axis=-1, keepdims=True)
                o = jnp.dot(p.astype(BF16), vb_scr[pl.ds(krow, KW), :], preferred_element_type=F32) / l
                ls = jnp.broadcast_to(m + jnp.log(l), (QB, DIL_HD))
                base = pl.multiple_of(m0 * d, QB)
                nat = pl.ds(base + r, QB, stride=d) if d > 1 else pl.ds(base, QB)
                if first:
                    o_ref[nat, :] = o
                    lse_scr[nat, :] = ls
                else:
                    o0, l0 = o_ref[nat, :], lse_scr[nat, :]
                    mx = jnp.maximum(l0, ls)
                    w0, w1 = jnp.exp(l0 - mx), jnp.exp(ls - mx)
                    den = w0 + w1
                    o_ref[nat, :] = (w0 * o0 + w1 * o) / den
                    lse_scr[nat, :] = mx + jnp.log(den)
                return carry

            lax.fori_loop(0, per, block, 0, unroll=min(per, 4))


def _dil(proj, cs, sn, B, S):
    T = proj.shape[0]
    blk = lambda col: pl.BlockSpec((S, DIL_HD), lambda b, h: (b, col + h))
    return pl.pallas_call(
        _dil_kernel,
        grid=(B, DIL_HEADS),
        in_specs=[blk(_COL_DQ), blk(_COL_DK), blk(_COL_DV),
                  pl.BlockSpec((S, DIL_HD), lambda b, h: (b, 0)),
                  pl.BlockSpec((S, DIL_HD), lambda b, h: (b, 0))],
        out_specs=pl.BlockSpec((S, DIL_HD), lambda b, h: (b, h)),
        out_shape=jax.ShapeDtypeStruct((T, DIL_HEADS * DIL_HD), F32),
        scratch_shapes=[pltpu.VMEM((S, DIL_HD), F32), pltpu.VMEM((S, DIL_HD), F32),
                        pltpu.VMEM((S, DIL_HD), BF16), pltpu.VMEM((S, DIL_HD), BF16),
                        pltpu.VMEM((S, DIL_HD), BF16), pltpu.VMEM((S, DIL_HD), F32)],
        compiler_params=_params(("parallel", "parallel"), 48),
        name="dil",
    )(proj, proj, proj, cs, sn)


def _outproj_kernel(og_ref, od_ref, w_ref, x_ref, ga_ref, o_ref):
    kg = og_ref.shape[1]
    mixed = jnp.dot(og_ref[...].astype(BF16), w_ref[:kg, :], preferred_element_type=F32)
    mixed = mixed + jnp.dot(od_ref[...].astype(BF16), w_ref[kg:, :], preferred_element_type=F32)
    o_ref[...] = x_ref[...] + ga_ref[0] * mixed


def _outproj(o_gla, o_dil, w_out, x2, mod3, S):
    T, D = x2.shape
    tm = 512
    per_b = S // tm
    kg, kd = o_gla.shape[1], o_dil.shape[1]
    return pl.pallas_call(
        _outproj_kernel,
        grid=(T // tm,),
        in_specs=[pl.BlockSpec((tm, kg), lambda i: (i, 0)),
                  pl.BlockSpec((tm, kd), lambda i: (i, 0)),
                  pl.BlockSpec((kg + kd, D), lambda i: (0, 0)),
                  pl.BlockSpec((tm, D), lambda i: (i, 0)),
                  pl.BlockSpec((1, 1, D), lambda i: ((i // per_b) * 6 + 2, 0, 0))],
        out_specs=pl.BlockSpec((tm, D), lambda i: (i, 0)),
        out_shape=jax.ShapeDtypeStruct((T, D), F32),
        compiler_params=_params(("parallel",), 48),
        name="outproj",
    )(o_gla, o_dil, w_out, x2, mod3)


def _pq_kernel(x_ref, g_ref, sc_ref, sh_ref, w_ref, o_ref, h_ref):
    h = _rms(x_ref[...], g_ref[...]) * (1.0 + sc_ref[0]) + sh_ref[0]
    h_ref[...] = h
    o_ref[...] = jnp.dot(h.astype(BF16), w_ref[...], preferred_element_type=F32)


def _pq(x1, g, mod3, wq, S):
    T, D = x1.shape
    N = wq.shape[1]
    tm = 512
    per_b = S // tm
    return pl.pallas_call(
        _pq_kernel,
        grid=(T // tm,),
        in_specs=[pl.BlockSpec((tm, D), lambda i: (i, 0)),
                  pl.BlockSpec((1, D), lambda i: (0, 0)),
                  pl.BlockSpec((1, 1, D), lambda i: ((i // per_b) * 6 + 4, 0, 0)),
                  pl.BlockSpec((1, 1, D), lambda i: ((i // per_b) * 6 + 3, 0, 0)),
                  pl.BlockSpec((D, N), lambda i: (0, 0))],
        out_specs=[pl.BlockSpec((tm, N), lambda i: (i, 0)),
                   pl.BlockSpec((tm, D), lambda i: (i, 0))],
        out_shape=[jax.ShapeDtypeStruct((T, N), F32), jax.ShapeDtypeStruct((T, D), F32)],
        compiler_params=_params(("parallel",), 56),
        name="pq",
    )(x1, g, mod3, mod3, wq)


def _top_rows(s, k, payload=None):
    n_rows = s.shape[0]
    rid = lax.broadcasted_iota(jnp.int32, s.shape, 0).astype(F32)
    vals, picks = [], []
    for _ in range(k):
        m = jnp.max(s, axis=0, keepdims=True)
        pos = jnp.min(jnp.where(s == m, rid, float(n_rows)), axis=0, keepdims=True)
        hit = rid == pos
        vals.append(m)
        if payload is None:
            picks.append(pos)
        else:
            picks.append(jnp.sum(jnp.where(hit, payload, 0), axis=0, keepdims=True))
        s = jnp.where(hit, -jnp.inf, s)
    picks = jnp.concatenate(picks, axis=0)
    return jnp.concatenate(vals, axis=0), picks.astype(jnp.int32)


def _staircase(a, b, combine, fill):
    K = a.shape[0]
    half = K // 2
    jrow = lax.broadcasted_iota(jnp.int32, (half, a.shape[1]), 0)
    pieces = [combine(a[0:1], b)]
    for i in range(1, half):
        piece = combine(a[i:i + 1], b[0:half])
        width = K // (i + 1)
        pieces.append(piece if width >= half else jnp.where(jrow < width, piece, fill))
    pieces.append(combine(a[half:K], b[0:1]))
    return jnp.concatenate(pieces, axis=0)


def _topk_kernel(q_ref, keys_ref, idx_ref, gate_ref):
    K = PEER_TOPK
    for h in range(PEER_HEADS):
        tops = []
        for half in range(2):
            c0 = (h * 2 + half) * PEER_HALF
            qh = q_ref[:, c0:c0 + PEER_HALF].astype(BF16)
            sc = lax.dot_general(keys_ref[h, half], qh, _NT, preferred_element_type=F32)
            tops.append(_top_rows(sc, K))
        (s0, i0), (s1, i1) = tops
        cand_s = _staircase(s0, s1, lambda a, b: a + b, -jnp.inf)
        cand_i = _staircase(i0, i1, lambda a, b: a * PEER_NKEYS + b, 0)
        best, idx = _top_rows(cand_s, K, payload=cand_i)
        e = jnp.exp(best - best[0:1])
        gate = e / jnp.sum(e, axis=0, keepdims=True)
        idx_ref[h * K:(h + 1) * K, :] = idx
        gate_ref[h * K:(h + 1) * K, :] = gate


def _topk(qp, keys_bf):
    T, N = qp.shape
    tt = 256
    HK = PEER_HEADS * PEER_TOPK
    return pl.pallas_call(
        _topk_kernel,
        grid=(T // tt,),
        in_specs=[pl.BlockSpec((tt, N), lambda i: (i, 0)),
                  pl.BlockSpec(keys_bf.shape, lambda i: (0, 0, 0, 0))],
        out_specs=[pl.BlockSpec((HK, tt), lambda i: (0, i)),
                   pl.BlockSpec((HK, tt), lambda i: (0, i))],
        out_shape=[jax.ShapeDtypeStruct((HK, T), jnp.int32),
                   jax.ShapeDtypeStruct((HK, T), F32)],
        compiler_params=_params(("parallel",), 32),
        name="topk",
    )(qp, keys_bf)


_PEER_TB = 128
_PEER_SUB = 8
_SUBLANES = 8
_PEER_CHUNKS = 16
_PEER_SC_U_CHUNKS = 3
_SC_LANES = 16
_SC_ROWS = 16


def _peer_u_kernel(idx_hbm, gate_ref, x1_ref, gn_ref, sc_ref, sh_ref, u_hbm, w_ref,
                   idx_smem, ub0, ub1, h_scr, sem_i, sem_u, *, step0):
    HK = PEER_HEADS * PEER_TOPK
    TB, SUB = _PEER_TB, _PEER_SUB
    R = SUB * HK
    N = TB * HK
    nsub = TB // SUB
    D = x1_ref.shape[1]
    nchunk = D // LANE
    tiles = HK // _SUBLANES
    i = pl.program_id(0)
    n = pl.num_programs(0)
    cur = lax.rem(i, 2) * N
    nxt = N - cur
    more = i + 1 < n
    ubufs = (ub0, ub1)

    def idx_copy(step, base):
        return pltpu.make_async_copy(idx_hbm.at[pl.ds((step0 + step) * N, N)],
                                     idx_smem.at[pl.ds(base, N)], sem_i)

    def issue_token(base, t, slot):
        for k in range(HK):
            e = idx_smem[base + t * HK + k]
            rt, s = t * tiles + k // _SUBLANES, k % _SUBLANES
            pltpu.make_async_copy(u_hbm.at[e], ubufs[slot].at[rt, :, s, :],
                                  sem_u.at[slot]).start(priority=k % 2)

    def wait(slot):
        pltpu.make_async_copy(ubufs[slot], ubufs[slot], sem_u.at[slot]).wait()

    @pl.when(i == 0)
    def _():
        first = idx_copy(0, 0)
        first.start()
        first.wait()
        for t in range(SUB):
            issue_token(0, t, 0)

    @pl.when(more)
    def _():
        idx_copy(i + 1, nxt).start()

    h_scr[...] = _rms(x1_ref[...], gn_ref[...]) * (1.0 + sc_ref[0]) + sh_ref[0]
    lane = lax.broadcasted_iota(jnp.int32, (HK, TB), 1)

    def compute_token(j, t, slot, wacc):
        ub = ubufs[slot]
        tok = j * SUB + t
        xt = h_scr[pl.ds(tok, 1), :]
        rows = slice(t * tiles, (t + 1) * tiles)
        part = ub[rows, 0].reshape(HK, LANE) * xt[:, 0:LANE]
        for c in range(1, nchunk):
            part = part + ub[rows, c].reshape(HK, LANE) * xt[:, c * LANE:(c + 1) * LANE]
        a = jnp.sum(part, axis=1, keepdims=True)
        hit = lane == tok
        g = jnp.sum(jnp.where(hit, gate_ref[...], 0.0), axis=1, keepdims=True)
        wgt = g * (0.5 * a * (1.0 + lax.erf(a * (2.0 ** -0.5))))
        return jnp.where(hit, wgt, wacc)

    def half(j, slot, next_base, wacc):
        wait(slot)
        for t in range(SUB):
            issue_token(next_base, t, 1 - slot)
            wacc = compute_token(j, t, slot, wacc)
        return wacc

    def pair(jj, wacc):
        j0 = 2 * jj
        wacc = half(j0, 0, cur + (j0 + 1) * R, wacc)
        last = jj == nsub // 2 - 1

        @pl.when(jnp.logical_and(last, more))
        def _():
            idx_copy(i + 1, nxt).wait()

        after = jnp.where(more, nxt, cur)
        return half(j0 + 1, 1, jnp.where(last, after, cur + (j0 + 2) * R), wacc)

    wacc = lax.fori_loop(0, nsub // 2, pair, jnp.zeros((HK, TB), F32))

    @pl.when(jnp.logical_not(more))
    def _():
        wait(0)

    w_ref[:, :HK] = jnp.zeros((TB, HK), F32)
    w_ref[:, HK:] = wacc.T


def _peer_u(idx_flat, gate_t, x1, g_norm, mod3, u3, S, step0, nsteps):
    T, D = x1.shape
    HK = PEER_HEADS * PEER_TOPK
    TB, SUB = _PEER_TB, _PEER_SUB
    per_b = S // TB
    modrow = lambda k: (lambda i: (((step0 + i) // per_b) * 6 + k, 0, 0))
    gbuf = pltpu.VMEM((SUB * HK // _SUBLANES, D // LANE, _SUBLANES, LANE), F32)
    return pl.pallas_call(
        functools.partial(_peer_u_kernel, step0=step0),
        grid=(nsteps,),
        in_specs=[pl.BlockSpec(memory_space=pl.ANY),
                  pl.BlockSpec((HK, TB), lambda i: (0, step0 + i)),
                  pl.BlockSpec((TB, D), lambda i: (step0 + i, 0)),
                  pl.BlockSpec((1, D), lambda i: (0, 0)),
                  pl.BlockSpec((1, 1, D), modrow(4)),
                  pl.BlockSpec((1, 1, D), modrow(3)),
                  pl.BlockSpec(memory_space=pl.ANY)],
        out_specs=pl.BlockSpec((TB, 2 * HK), lambda i: (i, 0)),
        out_shape=jax.ShapeDtypeStruct((nsteps * TB, 2 * HK), F32),
        scratch_shapes=[pltpu.SMEM((2 * TB * HK,), jnp.int32),
                        gbuf, gbuf,
                        pltpu.VMEM((TB, D), F32),
                        pltpu.SemaphoreType.DMA,
                        pltpu.SemaphoreType.DMA((2,))],
        compiler_params=_params(("arbitrary",), 40),
        name="peer_u",
    )(idx_flat, gate_t, x1, g_norm, mod3, mod3, u3)


def _sc_peer_u(u_tab, idx_flat, h, tok_base, ntok):
    E, nblk, _ = u_tab.shape
    D = nblk * LANE
    HK = PEER_HEADS * PEER_TOPK
    info = plsc.get_sparse_core_info()
    nw = info.num_cores * info.num_subcores
    tpw = ntok // nw
    CH = _SC_ROWS
    nch = HK // CH
    nsl = D // _SC_LANES
    GRP = 8
    mesh = plsc.VectorSubcoreMesh(core_axis_name="c", subcore_axis_name="s")

    @functools.partial(
        pl.kernel, mesh=mesh, out_type=jax.ShapeDtypeStruct((ntok, HK), F32),
        scratch_types=[pltpu.VMEM((tpw * HK,), jnp.int32), pltpu.VMEM((D,), F32), pltpu.VMEM((HK,), F32),
                       pltpu.VMEM((CH, nblk, LANE), F32), pltpu.VMEM((CH, nblk, LANE), F32),
                       pltpu.SemaphoreType.DMA, pltpu.SemaphoreType.DMA],
        compiler_params=pltpu.CompilerParams(needs_layout_passes=False),
        name="sc_peer_u",
    )
    def k(tab_hbm, idx_hbm, h_hbm, a_hbm, idx_v, x_v, a_v, buf0, buf1, g0, g1):
        wid = lax.axis_index("s") * info.num_cores + lax.axis_index("c")
        tok0 = tok_base + wid * tpw
        pltpu.sync_copy(idx_hbm.at[pl.ds(tok0 * HK, tpw * HK)], idx_v)
        bufs, gs = (buf0, buf1), (g0, g1)
        lane = lax.broadcasted_iota(jnp.int32, (_SC_LANES,), 0)

        def gather(g, b):
            return pltpu.make_async_copy(tab_hbm.at[idx_v.at[pl.ds(g * CH, CH)]], bufs[b], gs[b])

        gather(0, 0).start()

        @pl.loop(0, tpw)
        def _(t):
            pltpu.sync_copy(h_hbm.at[tok0 + t], x_v)
            for ch in range(nch):
                b = ch % 2
                g = t * nch + ch
                gather(g, b).wait()

                @pl.when(g + 1 < tpw * nch)
                def _():
                    gather(g + 1, 1 - b).start()

                a_vec = jnp.zeros((_SC_LANES,), F32)
                for grp in range(CH // GRP):
                    def body(c, accs):
                        blk = c // (LANE // _SC_LANES)
                        lo = pl.multiple_of(c * _SC_LANES - blk * LANE, _SC_LANES)
                        xv = x_v[pl.ds(pl.multiple_of(c * _SC_LANES, _SC_LANES), _SC_LANES)]
                        return tuple(accs[r] + bufs[b][grp * GRP + r, blk, pl.ds(lo, _SC_LANES)] * xv
                                     for r in range(GRP))

                    accs = lax.fori_loop(0, nsl, body,
                                         tuple(jnp.zeros((_SC_LANES,), F32) for _ in range(GRP)))
                    for r in range(GRP):
                        a_vec = jnp.where(lane == grp * GRP + r, jnp.sum(accs[r]), a_vec)
                a_v[pl.ds(ch * CH, CH)] = a_vec
            pltpu.sync_copy(a_v, a_hbm.at[tok0 - tok_base + t])

    return k(u_tab, idx_flat, h)


def _peer_gate_kernel(a_ref, gate_ref, w_ref):
    HK = a_ref.shape[1]
    a = a_ref[...]
    w_ref[:, :HK] = jnp.zeros(a.shape, F32)
    w_ref[:, HK:] = gate_ref[...].T * (0.5 * a * (1.0 + lax.erf(a * (2.0 ** -0.5))))


def _peer_gate(a, gate_t, step0):
    Tc, HK = a.shape
    TB = _PEER_TB
    return pl.pallas_call(
        _peer_gate_kernel,
        grid=(Tc // TB,),
        in_specs=[pl.BlockSpec((TB, HK), lambda i: (i, 0)),
                  pl.BlockSpec((HK, TB), lambda i: (0, step0 + i))],
        out_specs=pl.BlockSpec((TB, 2 * HK), lambda i: (i, 0)),
        out_shape=jax.ShapeDtypeStruct((Tc, 2 * HK), F32),
        compiler_params=_params(("parallel",), 32),
        name="peer_gate",
    )(a, gate_t)


def _sc_peer_v(v_tab, idx_flat, wgt, tok_base):
    E, nblk, _ = v_tab.shape
    D = nblk * LANE
    Tc, HK = wgt.shape[0], wgt.shape[1] // 2
    info = plsc.get_sparse_core_info()
    nw = info.num_cores * info.num_subcores
    tpw = Tc // nw
    CH = _SC_ROWS
    nch = HK // CH
    nsl = D // _SC_LANES
    mesh = plsc.VectorSubcoreMesh(core_axis_name="c", subcore_axis_name="s")

    @functools.partial(
        pl.kernel, mesh=mesh, out_type=jax.ShapeDtypeStruct((Tc, D), F32),
        scratch_types=[pltpu.VMEM((tpw * HK,), jnp.int32), pltpu.VMEM((2 * HK,), F32),
                       pltpu.VMEM((D,), F32),
                       pltpu.VMEM((CH, nblk, LANE), F32), pltpu.VMEM((CH, nblk, LANE), F32),
                       pltpu.SemaphoreType.DMA, pltpu.SemaphoreType.DMA],
        compiler_params=pltpu.CompilerParams(needs_layout_passes=False),
        name="sc_peer_v",
    )
    def k(tab_hbm, idx_hbm, w_hbm, o_hbm, idx_v, w_v, o_v, buf0, buf1, g0, g1):
        wid = lax.axis_index("s") * info.num_cores + lax.axis_index("c")
        tok0 = wid * tpw
        pltpu.sync_copy(idx_hbm.at[pl.ds((tok_base + tok0) * HK, tpw * HK)], idx_v)
        bufs, gs = (buf0, buf1), (g0, g1)

        def gather(g, b):
            return pltpu.make_async_copy(tab_hbm.at[idx_v.at[pl.ds(g * CH, CH)]], bufs[b], gs[b])

        gather(0, 0).start()

        @pl.loop(0, tpw)
        def _(t):
            pltpu.sync_copy(w_hbm.at[tok0 + t], w_v)

            @pl.loop(0, nsl)
            def _(c):
                o_v[pl.ds(pl.multiple_of(c * _SC_LANES, _SC_LANES), _SC_LANES)] = jnp.zeros((_SC_LANES,), F32)

            for ch in range(nch):
                b = ch % 2
                g = t * nch + ch
                gather(g, b).wait()

                @pl.when(g + 1 < tpw * nch)
                def _():
                    gather(g + 1, 1 - b).start()

                ws = [plsc.load_gather(w_v, [jnp.full((_SC_LANES,), HK + ch * CH + r, jnp.int32)])
                      for r in range(CH)]

                @plsc.parallel_loop(0, nsl, step=2)
                def _(c):
                    for half in range(2):
                        off = pl.multiple_of((c + half) * _SC_LANES, _SC_LANES)
                        blk = (c + half) // (LANE // _SC_LANES)
                        lo = pl.multiple_of(off - blk * LANE, _SC_LANES)
                        parts = [ws[r] * bufs[b][r, blk, pl.ds(lo, _SC_LANES)] for r in range(CH)]
                        while len(parts) > 1:
                            parts = [parts[p] + parts[p + 1] for p in range(0, len(parts), 2)]
                        o_v[pl.ds(off, _SC_LANES)] = o_v[pl.ds(off, _SC_LANES)] + parts[0]

            pltpu.sync_copy(o_v, o_hbm.at[tok0 + t])

    return k(v_tab, idx_flat, wgt)


def _peer_fin_kernel(x_ref, p_ref, ga_ref, gf_ref, o_ref):
    o_ref[...] = _rms(x_ref[...] + ga_ref[0] * p_ref[...], gf_ref[...])


def _peer_fin(x1, po, mod3, g_final, S):
    T, D = x1.shape
    tm = 512
    per_b = S // tm
    return pl.pallas_call(
        _peer_fin_kernel,
        grid=(T // tm,),
        in_specs=[pl.BlockSpec((tm, D), lambda i: (i, 0)),
                  pl.BlockSpec((tm, D), lambda i: (i, 0)),
                  pl.BlockSpec((1, 1, D), lambda i: ((i // per_b) * 6 + 5, 0, 0)),
                  pl.BlockSpec((1, D), lambda i: (0, 0))],
        out_specs=pl.BlockSpec((tm, D), lambda i: (i, 0)),
        out_shape=jax.ShapeDtypeStruct((T, D), F32),
        compiler_params=_params(("parallel",), 40),
        name="peer_fin",
    )(x1, po, mod3, g_final)


def _pad_gate(w, lo):
    rank = w.shape[0]
    wh = w.reshape(rank, GLA_HEADS, GLA_DK).transpose(1, 0, 2)
    return jnp.zeros((GLA_HEADS, LANE, GLA_DK), F32).at[:, lo:lo + rank, :].set(wh)


def kernel(x, c, positions, w_ada, b_ada, g_norm_mix, w_in, w_gate_f, b_gate_f, w_gate_b, b_gate_b,
           g_gla_out, w_out, g_norm_ffn, w_peer_q, peer_sub_keys, peer_u, peer_v, g_final):
    B, S, D = x.shape
    T = B * S
    depth = w_ada.shape[0]
    assert depth == 1, "the final norm is fused into the last PEER call; one layer only"
    xt = x.reshape(T, D)
    cs, sn = _rope_tables(positions.reshape(T, 1))
    gz0 = 2 * GLA_HEADS * GLA_DK + 2 * GLA_HEADS * GLA_DV
    gz1 = gz0 + 2 * GLA_GATE_RANK
    for l in range(depth):
        mod3 = _ada(c, w_ada[l], b_ada[l]).reshape(B * 6, 1, D)
        w_main = jnp.concatenate([w_in[l][:, :gz0], w_in[l][:, gz1:]], axis=1).astype(BF16)
        w_z = jnp.pad(w_in[l][:, gz0:gz1], ((0, 0), (0, LANE - (gz1 - gz0)))).astype(BF16)
        proj, gz = _inproj(xt, g_norm_mix[l].reshape(1, D), mod3, w_main, w_z, S)
        o_gla = _gla(proj, gz,
                     _pad_gate(w_gate_f[l], 0), b_gate_f[l].reshape(GLA_HEADS, 1, GLA_DK),
                     _pad_gate(w_gate_b[l], GLA_GATE_RANK), b_gate_b[l].reshape(GLA_HEADS, 1, GLA_DK),
                     g_gla_out[l].reshape(GLA_HEADS, 1, GLA_DV), B, S)
        o_dil = _dil(proj, cs, sn, B, S)
        x1 = _outproj(o_gla, o_dil, w_out[l].astype(BF16), xt, mod3, S)
        qp, h2 = _pq(x1, g_norm_ffn[l].reshape(1, D), mod3, w_peer_q[l].astype(BF16), S)
        idx_t, gate_t = _topk(qp, peer_sub_keys[l].astype(BF16))
        idx_flat = idx_t.T.reshape(-1)
        E = peer_u.shape[1]
        u3 = peer_u[l].reshape(E, D // LANE, LANE)
        v3 = peer_v[l].reshape(E, D // LANE, LANE)
        steps = T // _PEER_TB // _PEER_CHUNKS
        tc_chunks = _PEER_CHUNKS - _PEER_SC_U_CHUNKS
        chunk_tok = steps * _PEER_TB
        a_sc = _sc_peer_u(u3, idx_flat, h2, tc_chunks * chunk_tok, _PEER_SC_U_CHUNKS * chunk_tok)
        outs = []
        for ck in range(_PEER_CHUNKS):
            if ck < tc_chunks:
                wgt = _peer_u(idx_flat, gate_t, x1, g_norm_ffn[l].reshape(1, D), mod3, u3, S,
                              ck * steps, steps)
            else:
                lo = (ck - tc_chunks) * chunk_tok
                wgt = _peer_gate(a_sc[lo:lo + chunk_tok], gate_t, ck * steps)
            outs.append(_sc_peer_v(v3, idx_flat, wgt, ck * chunk_tok))
        xt = _peer_fin(x1, jnp.concatenate(outs, axis=0), mod3, g_final.reshape(1, D), S)
    return xt.reshape(B, S, D)
```

```python
import functools

import jax
import jax.numpy as jnp
from jax import lax
from jax.experimental import pallas as pl
from jax.experimental.pallas import tpu as pltpu
from jax.experimental.pallas import tpu_sc as plsc

F32 = jnp.float32
BF16 = jnp.bfloat16
HIGHEST = lax.Precision.HIGHEST

NORM_EPS = 1e-6
GLA_HEADS = 4
GLA_DK = 128
GLA_DV = 256
GLA_GATE_RANK = 16
GLA_TAU = 16.0
GLA_CHUNK = 64
DIL_HD = 128
DIL_HEADS = 8
DIL_PATTERNS = ((128, 1), (512, 4), (2048, 16))
ROPE_THETA = 10000.0
NEG_INF = -1e30
PEER_HEADS = 8
PEER_NKEYS = 128
PEER_TOPK = 16
PEER_HALF = 128

LANE = 128
MIB = 1024 * 1024

_COL_GQ, _COL_GK, _COL_GV, _COL_GR, _COL_DQ, _COL_DK, _COL_DV = 0, 4, 8, 16, 24, 32, 40
_PROJ_W = 48 * LANE

_NT = (((1,), (1,)), ((), ()))
_TN = (((0,), (0,)), ((), ()))


def _params(sem, vmem_mib):
    return pltpu.CompilerParams(dimension_semantics=sem, vmem_limit_bytes=vmem_mib * MIB)


def _rms(x, g):
    return x * lax.rsqrt(jnp.mean(x * x, axis=-1, keepdims=True) + NORM_EPS) * g


def _silu(x):
    return x / (1.0 + jnp.exp(-x))


def _ada_kernel(c_ref, w_ref, b_ref, o_ref):
    s = _silu(c_ref[...]).astype(BF16)
    o_ref[...] = jnp.dot(s, w_ref[...].astype(BF16), preferred_element_type=F32) + b_ref[...]


def _ada(c, w, b):
    B, D = c.shape
    N = w.shape[1]
    tn = 1024
    cp = jnp.zeros((8, D), F32).at[:B].set(c)
    out = pl.pallas_call(
        _ada_kernel,
        grid=(N // tn,),
        in_specs=[pl.BlockSpec((8, D), lambda j: (0, 0)),
                  pl.BlockSpec((D, tn), lambda j: (0, j)),
                  pl.BlockSpec((1, tn), lambda j: (0, j))],
        out_specs=pl.BlockSpec((8, tn), lambda j: (0, j)),
        out_shape=jax.ShapeDtypeStruct((8, N), F32),
        compiler_params=_params(("parallel",), 40),
        name="ada",
    )(cp, w, b.reshape(1, N))
    return out[:B]


def _inproj_kernel(x_ref, g_ref, sc_ref, sh_ref, w_ref, wz_ref, o_ref, z_ref, h_scr):
    @pl.when(pl.program_id(1) == 0)
    def _():
        h = _rms(x_ref[...], g_ref[...]) * (1.0 + sc_ref[0]) + sh_ref[0]
        hb = h.astype(BF16)
        h_scr[...] = hb
        z_ref[...] = jnp.dot(hb, wz_ref[...], preferred_element_type=F32)

    o_ref[...] = jnp.dot(h_scr[...], w_ref[...], preferred_element_type=F32)


def _inproj(x2, g, mod3, w_main, w_z, S):
    T, D = x2.shape
    tm, tn = 1024, 768
    per_b = S // tm
    return pl.pallas_call(
        _inproj_kernel,
        grid=(T // tm, _PROJ_W // tn),
        in_specs=[pl.BlockSpec((tm, D), lambda i, j: (i, 0)),
                  pl.BlockSpec((1, D), lambda i, j: (0, 0)),
                  pl.BlockSpec((1, 1, D), lambda i, j: ((i // per_b) * 6 + 1, 0, 0)),
                  pl.BlockSpec((1, 1, D), lambda i, j: ((i // per_b) * 6 + 0, 0, 0)),
                  pl.BlockSpec((D, tn), lambda i, j: (0, j)),
                  pl.BlockSpec((D, LANE), lambda i, j: (0, 0))],
        out_specs=[pl.BlockSpec((tm, tn), lambda i, j: (i, j)),
                   pl.BlockSpec((tm, LANE), lambda i, j: (i, 0))],
        out_shape=[jax.ShapeDtypeStruct((T, _PROJ_W), F32),
                   jax.ShapeDtypeStruct((T, LANE), F32)],
        scratch_shapes=[pltpu.VMEM((tm, D), BF16)],
        compiler_params=_params(("parallel", "arbitrary"), 48),
        name="inproj",
    )(x2, g, mod3, mod3, w_main, w_z)


def _rope_kernel(pos_ref, f_ref, sg_ref, cs_ref, sn_ref):
    ang = pos_ref[...].astype(F32) * f_ref[...]
    cs_ref[...] = jnp.cos(ang)
    sn_ref[...] = jnp.sin(ang) * sg_ref[...]


def _rope_tables(pos_col):
    T = pos_col.shape[0]
    half = DIL_HD // 2
    inv = jnp.power(ROPE_THETA, -jnp.arange(half, dtype=F32) * 2.0 / DIL_HD)
    freq = jnp.concatenate([inv, inv]).reshape(1, DIL_HD)
    sign = jnp.concatenate([-jnp.ones((half,), F32), jnp.ones((half,), F32)]).reshape(1, DIL_HD)
    tm = 1024
    return pl.pallas_call(
        _rope_kernel,
        grid=(T // tm,),
        in_specs=[pl.BlockSpec((tm, 1), lambda i: (i, 0)),
                  pl.BlockSpec((1, DIL_HD), lambda i: (0, 0)),
                  pl.BlockSpec((1, DIL_HD), lambda i: (0, 0))],
        out_specs=[pl.BlockSpec((tm, DIL_HD), lambda i: (i, 0)),
                   pl.BlockSpec((tm, DIL_HD), lambda i: (i, 0))],
        out_shape=[jax.ShapeDtypeStruct((T, DIL_HD), F32)] * 2,
        compiler_params=_params(("parallel",), 32),
        name="rope",
    )(pos_col, freq, sign)


def _gla_kernel(q_ref, k_ref, v_ref, r_ref, z_ref, wgf_ref, bgf_ref, wgb_ref, bgb_ref, g_ref,
                o_ref, laf_scr, lab_scr, of_scr):
    S = q_ref.shape[0]
    C = GLA_CHUNK
    n = S // C
    scale = GLA_DK ** -0.5
    row = lax.broadcasted_iota(jnp.int32, (C, C), 0)
    col = lax.broadcasted_iota(jnp.int32, (C, C), 1)

    def log_gate(w_ref, b_ref):
        zz = jnp.dot(z_ref[...], w_ref[0], precision=HIGHEST, preferred_element_type=F32) + b_ref[0]
        return (jnp.minimum(zz, 0.0) - jnp.log(1.0 + jnp.exp(-jnp.abs(zz)))) * (1.0 / GLA_TAU)

    def chunk(fwd, c, st_t, la_scr, o_scr):
        keep = (col <= row) if fwd else (col >= row)
        sl = pl.ds(pl.multiple_of(c * C, C), C)
        cum = la_scr[sl, :]
        tot = cum[C - 1:C, :] if fwd else cum[0:1, :]
        kk = k_ref[sl, :]
        qd = (q_ref[sl, :] * scale * jnp.exp(cum)).astype(BF16)
        ki = (kk * jnp.exp(-cum)).astype(BF16)
        kte = (kk * jnp.exp(tot - cum)).astype(BF16)
        vb = v_ref[sl, :].astype(BF16)
        attn = lax.dot_general(qd, ki, _NT, preferred_element_type=F32)
        attn = jnp.where(keep, attn, 0.0).astype(BF16)
        o = jnp.dot(attn, vb, preferred_element_type=F32)
        o_scr[sl, :] = o + lax.dot_general(qd, st_t.astype(BF16), _NT, preferred_element_type=F32)
        upd = lax.dot_general(vb, kte, _TN, preferred_element_type=F32)
        return st_t * jnp.exp(tot) + upd

    laf_scr[...] = log_gate(wgf_ref, bgf_ref)
    lab_scr[...] = log_gate(wgb_ref, bgb_ref)

    def cumulate(c, carry):
        sl = pl.ds(pl.multiple_of(c * C, C), C)
        laf_scr[sl, :] = jnp.dot((col <= row).astype(F32), laf_scr[sl, :], precision=HIGHEST,
                                 preferred_element_type=F32)
        lab_scr[sl, :] = jnp.dot((col >= row).astype(F32), lab_scr[sl, :], precision=HIGHEST,
                                 preferred_element_type=F32)
        return carry

    lax.fori_loop(0, n, cumulate, 0, unroll=4)

    def body(i, states):
        return (chunk(True, i, states[0], laf_scr, of_scr),
                chunk(False, n - 1 - i, states[1], lab_scr, o_ref))

    zero = jnp.zeros((GLA_DV, GLA_DK), F32)
    lax.fori_loop(0, n, body, (zero, zero), unroll=4)
    y = _rms(of_scr[...] + o_ref[...], g_ref[0])
    o_ref[...] = y * _silu(r_ref[...])


def _gla(proj, gz, wgf, bgf, wgb, bgb, g_out, B, S):
    T = proj.shape[0]
    H = GLA_HEADS
    return pl.pallas_call(
        _gla_kernel,
        grid=(B, H),
        in_specs=[pl.BlockSpec((S, GLA_DK), lambda b, h: (b, _COL_GQ + h)),
                  pl.BlockSpec((S, GLA_DK), lambda b, h: (b, _COL_GK + h)),
                  pl.BlockSpec((S, GLA_DV), lambda b, h: (b, _COL_GV // 2 + h)),
                  pl.BlockSpec((S, GLA_DV), lambda b, h: (b, _COL_GR // 2 + h)),
                  pl.BlockSpec((S, LANE), lambda b, h: (b, 0)),
                  pl.BlockSpec((1, LANE, GLA_DK), lambda b, h: (h, 0, 0)),
                  pl.BlockSpec((1, 1, GLA_DK), lambda b, h: (h, 0, 0)),
                  pl.BlockSpec((1, LANE, GLA_DK), lambda b, h: (h, 0, 0)),
                  pl.BlockSpec((1, 1, GLA_DK), lambda b, h: (h, 0, 0)),
                  pl.BlockSpec((1, 1, GLA_DV), lambda b, h: (h, 0, 0))],
        out_specs=pl.BlockSpec((S, GLA_DV), lambda b, h: (b, h)),
        out_shape=jax.ShapeDtypeStruct((T, GLA_HEADS * GLA_DV), F32),
        scratch_shapes=[pltpu.VMEM((S, GLA_DK), F32), pltpu.VMEM((S, GLA_DK), F32),
                        pltpu.VMEM((S, GLA_DV), F32)],
        compiler_params=_params(("parallel", "parallel"), 56),
        name="gla",
    )(proj, proj, proj, proj, gz, wgf, bgf, wgb, bgb, g_out)


_DIL_QB = 128
_DIL_HALF = 64
_DIL_KW = _DIL_QB + 2 * _DIL_HALF


def _dil_kernel(q_ref, k_ref, v_ref, cs_ref, sn_ref, o_ref, qf_scr, kf_scr, qb_scr, kb_scr, vb_scr, lse_scr):
    S = q_ref.shape[0]
    QB, KW, HALF = _DIL_QB, _DIL_KW, _DIL_HALF
    rot = DIL_HD // 2
    cs, sn = cs_ref[...], sn_ref[...]
    q = q_ref[...]
    qf_scr[...] = (q * cs + pltpu.roll(q, rot, 1) * sn) * (DIL_HD ** -0.5)
    k = k_ref[...]
    kf_scr[...] = k * cs + pltpu.roll(k, rot, 1) * sn
    rel = (lax.broadcasted_iota(jnp.int32, (QB, KW), 1) - lax.broadcasted_iota(jnp.int32, (QB, KW), 0))

    for pi, (window, d) in enumerate(DIL_PATTERNS):
        assert window // (2 * d) == HALF
        M = S // d
        per = M // QB
        for r in range(d):
            rows = pl.ds(r * M, M)
            src = pl.ds(r, M, stride=d) if d > 1 else pl.ds(0, M)
            qb_scr[rows, :] = qf_scr[src, :].astype(BF16)
            kb_scr[rows, :] = kf_scr[src, :].astype(BF16)
            vb_scr[rows, :] = v_ref[src, :].astype(BF16)

        for r in range(d):
            def block(mb, carry, r=r, first=(pi == 0)):
                m0 = mb * QB
                start = jnp.clip(m0 - HALF, 0, M - KW)
                qrow = pl.multiple_of(r * M + m0, QB)
                krow = pl.multiple_of(r * M + start, HALF)
                s = lax.dot_general(qb_scr[pl.ds(qrow, QB), :], kb_scr[pl.ds(krow, KW), :], _NT,
                                    preferred_element_type=F32)
                s = jnp.where(jnp.abs(rel + (start - m0)) <= HALF, s, NEG_INF)
                m = jnp.max(s, axis=-1, keepdims=True)
                p = jnp.exp(s - m)
                l = jnp.sum(p, axis=-1, keepdims=True)
                o = jnp.dot(p.astype(BF16), vb_scr[pl.ds(krow, KW), :], preferred_element_type=F32) / l
                ls = jnp.broadcast_to(m + jnp.log(l), (QB, DIL_HD))
                base = pl.multiple_of(m0 * d, QB)
                nat = pl.ds(base + r, QB, stride=d) if d > 1 else pl.ds(base, QB)
                if first:
                    o_ref[nat, :] = o
                    lse_scr[nat, :] = ls
                else:
                    o0, l0 = o_ref[nat, :], lse_scr[nat, :]
                    mx = jnp.maximum(l0, ls)
                    w0, w1 = jnp.exp(l0 - mx), jnp.exp(ls - mx)
                    den = w0 + w1
                    o_ref[nat, :] = (w0 * o0 + w1 * o) / den
                    lse_scr[nat, :] = mx + jnp.log(den)
                return carry

            lax.fori_loop(0, per, block, 0, unroll=min(per, 4))


def _dil(proj, cs, sn, B, S):
    T = proj.shape[0]
    blk = lambda col: pl.BlockSpec((S, DIL_HD), lambda b, h: (b, col + h))
    return pl.pallas_call(
        _dil_kernel,
        grid=(B, DIL_HEADS),
        in_specs=[blk(_COL_DQ), blk(_COL_DK), blk(_COL_DV),
                  pl.BlockSpec((S, DIL_HD), lambda b, h: (b, 0)),
                  pl.BlockSpec((S, DIL_HD), lambda b, h: (b, 0))],
        out_specs=pl.BlockSpec((S, DIL_HD), lambda b, h: (b, h)),
        out_shape=jax.ShapeDtypeStruct((T, DIL_HEADS * DIL_HD), F32),
        scratch_shapes=[pltpu.VMEM((S, DIL_HD), F32), pltpu.VMEM((S, DIL_HD), F32),
                        pltpu.VMEM((S, DIL_HD), BF16), pltpu.VMEM((S, DIL_HD), BF16),
                        pltpu.VMEM((S, DIL_HD), BF16), pltpu.VMEM((S, DIL_HD), F32)],
        compiler_params=_params(("parallel", "parallel"), 48),
        name="dil",
    )(proj, proj, proj, cs, sn)


def _outproj_kernel(og_ref, od_ref, w_ref, x_ref, ga_ref, o_ref):
    kg = og_ref.shape[1]
    mixed = jnp.dot(og_ref[...].astype(BF16), w_ref[:kg, :], preferred_element_type=F32)
    mixed = mixed + jnp.dot(od_ref[...].astype(BF16), w_ref[kg:, :], preferred_element_type=F32)
    o_ref[...] = x_ref[...] + ga_ref[0] * mixed


def _outproj(o_gla, o_dil, w_out, x2, mod3, S):
    T, D = x2.shape
    tm = 512
    per_b = S // tm
    kg, kd = o_gla.shape[1], o_dil.shape[1]
    return pl.pallas_call(
        _outproj_kernel,
        grid=(T // tm,),
        in_specs=[pl.BlockSpec((tm, kg), lambda i: (i, 0)),
                  pl.BlockSpec((tm, kd), lambda i: (i, 0)),
                  pl.BlockSpec((kg + kd, D), lambda i: (0, 0)),
                  pl.BlockSpec((tm, D), lambda i: (i, 0)),
                  pl.BlockSpec((1, 1, D), lambda i: ((i // per_b) * 6 + 2, 0, 0))],
        out_specs=pl.BlockSpec((tm, D), lambda i: (i, 0)),
        out_shape=jax.ShapeDtypeStruct((T, D), F32),
        compiler_params=_params(("parallel",), 48),
        name="outproj",
    )(o_gla, o_dil, w_out, x2, mod3)


def _pq_kernel(x_ref, g_ref, sc_ref, sh_ref, w_ref, o_ref):
    h = _rms(x_ref[...], g_ref[...]) * (1.0 + sc_ref[0]) + sh_ref[0]
    o_ref[...] = jnp.dot(h.astype(BF16), w_ref[...], preferred_element_type=F32)


def _pq(x1, g, mod3, wq, S):
    T, D = x1.shape
    N = wq.shape[1]
    tm = 512
    per_b = S // tm
    return pl.pallas_call(
        _pq_kernel,
        grid=(T // tm,),
        in_specs=[pl.BlockSpec((tm, D), lambda i: (i, 0)),
                  pl.BlockSpec((1, D), lambda i: (0, 0)),
                  pl.BlockSpec((1, 1, D), lambda i: ((i // per_b) * 6 + 4, 0, 0)),
                  pl.BlockSpec((1, 1, D), lambda i: ((i // per_b) * 6 + 3, 0, 0)),
                  pl.BlockSpec((D, N), lambda i: (0, 0))],
        out_specs=pl.BlockSpec((tm, N), lambda i: (i, 0)),
        out_shape=jax.ShapeDtypeStruct((T, N), F32),
        compiler_params=_params(("parallel",), 48),
        name="pq",
    )(x1, g, mod3, mod3, wq)


def _top_rows(s, k, payload=None):
    n_rows = s.shape[0]
    rid = lax.broadcasted_iota(jnp.int32, s.shape, 0).astype(F32)
    vals, picks = [], []
    for _ in range(k):
        m = jnp.max(s, axis=0, keepdims=True)
        pos = jnp.min(jnp.where(s == m, rid, float(n_rows)), axis=0, keepdims=True)
        hit = rid == pos
        vals.append(m)
        if payload is None:
            picks.append(pos)
        else:
            picks.append(jnp.sum(jnp.where(hit, payload, 0), axis=0, keepdims=True))
        s = jnp.where(hit, -jnp.inf, s)
    picks = jnp.concatenate(picks, axis=0)
    return jnp.concatenate(vals, axis=0), picks.astype(jnp.int32)


def _staircase(a, b, combine, fill):
    K = a.shape[0]
    half = K // 2
    jrow = lax.broadcasted_iota(jnp.int32, (half, a.shape[1]), 0)
    pieces = [combine(a[0:1], b)]
    for i in range(1, half):
        piece = combine(a[i:i + 1], b[0:half])
        width = K // (i + 1)
        pieces.append(piece if width >= half else jnp.where(jrow < width, piece, fill))
    pieces.append(combine(a[half:K], b[0:1]))
    return jnp.concatenate(pieces, axis=0)


def _topk_kernel(q_ref, keys_ref, idx_ref, gate_ref):
    K = PEER_TOPK
    for h in range(PEER_HEADS):
        tops = []
        for half in range(2):
            c0 = (h * 2 + half) * PEER_HALF
            qh = q_ref[:, c0:c0 + PEER_HALF].astype(BF16)
            sc = lax.dot_general(keys_ref[h, half], qh, _NT, preferred_element_type=F32)
            tops.append(_top_rows(sc, K))
        (s0, i0), (s1, i1) = tops
        cand_s = _staircase(s0, s1, lambda a, b: a + b, -jnp.inf)
        cand_i = _staircase(i0, i1, lambda a, b: a * PEER_NKEYS + b, 0)
        best, idx = _top_rows(cand_s, K, payload=cand_i)
        e = jnp.exp(best - best[0:1])
        gate = e / jnp.sum(e, axis=0, keepdims=True)
        idx_ref[h * K:(h + 1) * K, :] = idx
        gate_ref[h * K:(h + 1) * K, :] = gate


def _topk(qp, keys_bf):
    T, N = qp.shape
    tt = 256
    HK = PEER_HEADS * PEER_TOPK
    return pl.pallas_call(
        _topk_kernel,
        grid=(T // tt,),
        in_specs=[pl.BlockSpec((tt, N), lambda i: (i, 0)),
                  pl.BlockSpec(keys_bf.shape, lambda i: (0, 0, 0, 0))],
        out_specs=[pl.BlockSpec((HK, tt), lambda i: (0, i)),
                   pl.BlockSpec((HK, tt), lambda i: (0, i))],
        out_shape=[jax.ShapeDtypeStruct((HK, T), jnp.int32),
                   jax.ShapeDtypeStruct((HK, T), F32)],
        compiler_params=_params(("parallel",), 32),
        name="topk",
    )(qp, keys_bf)


_PEER_TB = 128
_PEER_SUB = 8
_SUBLANES = 8
_PEER_CHUNKS = 32
_SC_LANES = 16
_SC_ROWS = 16


def _peer_u_kernel(idx_hbm, gate_ref, x1_ref, gn_ref, sc_ref, sh_ref, u_hbm, w_ref,
                   idx_smem, ub0, ub1, h_scr, sem_i, sem_u, *, step0):
    HK = PEER_HEADS * PEER_TOPK
    TB, SUB = _PEER_TB, _PEER_SUB
    R = SUB * HK
    N = TB * HK
    nsub = TB // SUB
    D = x1_ref.shape[1]
    nchunk = D // LANE
    tiles = HK // _SUBLANES
    i = pl.program_id(0)
    n = pl.num_programs(0)
    cur = lax.rem(i, 2) * N
    nxt = N - cur
    more = i + 1 < n
    ubufs = (ub0, ub1)

    def idx_copy(step, base):
        return pltpu.make_async_copy(idx_hbm.at[pl.ds((step0 + step) * N, N)],
                                     idx_smem.at[pl.ds(base, N)], sem_i)

    def issue_token(base, t, slot):
        for k in range(HK):
            e = idx_smem[base + t * HK + k]
            rt, s = t * tiles + k // _SUBLANES, k % _SUBLANES
            pltpu.make_async_copy(u_hbm.at[e], ubufs[slot].at[rt, :, s, :],
                                  sem_u.at[slot]).start(priority=k % 2)

    def wait(slot):
        pltpu.make_async_copy(ubufs[slot], ubufs[slot], sem_u.at[slot]).wait()

    @pl.when(i == 0)
    def _():
        first = idx_copy(0, 0)
        first.start()
        first.wait()
        for t in range(SUB):
            issue_token(0, t, 0)

    @pl.when(more)
    def _():
        idx_copy(i + 1, nxt).start()

    h_scr[...] = _rms(x1_ref[...], gn_ref[...]) * (1.0 + sc_ref[0]) + sh_ref[0]
    lane = lax.broadcasted_iota(jnp.int32, (HK, TB), 1)

    def compute_token(j, t, slot, wacc):
        ub = ubufs[slot]
        tok = j * SUB + t
        xt = h_scr[pl.ds(tok, 1), :]
        rows = slice(t * tiles, (t + 1) * tiles)
        part = ub[rows, 0].reshape(HK, LANE) * xt[:, 0:LANE]
        for c in range(1, nchunk):
            part = part + ub[rows, c].reshape(HK, LANE) * xt[:, c * LANE:(c + 1) * LANE]
        a = jnp.sum(part, axis=1, keepdims=True)
        hit = lane == tok
        g = jnp.sum(jnp.where(hit, gate_ref[...], 0.0), axis=1, keepdims=True)
        wgt = g * (0.5 * a * (1.0 + lax.erf(a * (2.0 ** -0.5))))
        return jnp.where(hit, wgt, wacc)

    def half(j, slot, next_base, wacc):
        wait(slot)
        for t in range(SUB):
            issue_token(next_base, t, 1 - slot)
            wacc = compute_token(j, t, slot, wacc)
        return wacc

    def pair(jj, wacc):
        j0 = 2 * jj
        wacc = half(j0, 0, cur + (j0 + 1) * R, wacc)
        last = jj == nsub // 2 - 1

        @pl.when(jnp.logical_and(last, more))
        def _():
            idx_copy(i + 1, nxt).wait()

        after = jnp.where(more, nxt, cur)
        return half(j0 + 1, 1, jnp.where(last, after, cur + (j0 + 2) * R), wacc)

    wacc = lax.fori_loop(0, nsub // 2, pair, jnp.zeros((HK, TB), F32))

    @pl.when(jnp.logical_not(more))
    def _():
        wait(0)

    w_ref[:, :HK] = jnp.zeros((TB, HK), F32)
    w_ref[:, HK:] = wacc.T


def _peer_u(idx_flat, gate_t, x1, g_norm, mod3, u3, S, step0, nsteps):
    T, D = x1.shape
    HK = PEER_HEADS * PEER_TOPK
    TB, SUB = _PEER_TB, _PEER_SUB
    per_b = S // TB
    modrow = lambda k: (lambda i: (((step0 + i) // per_b) * 6 + k, 0, 0))
    gbuf = pltpu.VMEM((SUB * HK // _SUBLANES, D // LANE, _SUBLANES, LANE), F32)
    return pl.pallas_call(
        functools.partial(_peer_u_kernel, step0=step0),
        grid=(nsteps,),
        in_specs=[pl.BlockSpec(memory_space=pl.ANY),
                  pl.BlockSpec((HK, TB), lambda i: (0, step0 + i)),
                  pl.BlockSpec((TB, D), lambda i: (step0 + i, 0)),
                  pl.BlockSpec((1, D), lambda i: (0, 0)),
                  pl.BlockSpec((1, 1, D), modrow(4)),
                  pl.BlockSpec((1, 1, D), modrow(3)),
                  pl.BlockSpec(memory_space=pl.ANY)],
        out_specs=pl.BlockSpec((TB, 2 * HK), lambda i: (i, 0)),
        out_shape=jax.ShapeDtypeStruct((nsteps * TB, 2 * HK), F32),
        scratch_shapes=[pltpu.SMEM((2 * TB * HK,), jnp.int32),
                        gbuf, gbuf,
                        pltpu.VMEM((TB, D), F32),
                        pltpu.SemaphoreType.DMA,
                        pltpu.SemaphoreType.DMA((2,))],
        compiler_params=_params(("arbitrary",), 40),
        name="peer_u",
    )(idx_flat, gate_t, x1, g_norm, mod3, mod3, u3)


def _sc_peer_v(v_tab, idx_flat, wgt, tok_base):
    E, nblk, _ = v_tab.shape
    D = nblk * LANE
    Tc, HK = wgt.shape[0], wgt.shape[1] // 2
    info = plsc.get_sparse_core_info()
    nw = info.num_cores * info.num_subcores
    tpw = Tc // nw
    CH = _SC_ROWS
    nch = HK // CH
    nsl = D // _SC_LANES
    mesh = plsc.VectorSubcoreMesh(core_axis_name="c", subcore_axis_name="s")

    @functools.partial(
        pl.kernel, mesh=mesh, out_type=jax.ShapeDtypeStruct((Tc, D), F32),
        scratch_types=[pltpu.VMEM((tpw * HK,), jnp.int32), pltpu.VMEM((2 * HK,), F32),
                       pltpu.VMEM((D,), F32),
                       pltpu.VMEM((CH, nblk, LANE), F32), pltpu.VMEM((CH, nblk, LANE), F32),
                       pltpu.SemaphoreType.DMA, pltpu.SemaphoreType.DMA],
        compiler_params=pltpu.CompilerParams(needs_layout_passes=False),
        name="sc_peer_v",
    )
    def k(tab_hbm, idx_hbm, w_hbm, o_hbm, idx_v, w_v, o_v, buf0, buf1, g0, g1):
        wid = lax.axis_index("s") * info.num_cores + lax.axis_index("c")
        tok0 = wid * tpw
        pltpu.sync_copy(idx_hbm.at[pl.ds((tok_base + tok0) * HK, tpw * HK)], idx_v)
        bufs, gs = (buf0, buf1), (g0, g1)

        def gather(g, b):
            return pltpu.make_async_copy(tab_hbm.at[idx_v.at[pl.ds(g * CH, CH)]], bufs[b], gs[b])

        gather(0, 0).start()

        @pl.loop(0, tpw)
        def _(t):
            pltpu.sync_copy(w_hbm.at[tok0 + t], w_v)

            @pl.loop(0, nsl)
            def _(c):
                o_v[pl.ds(pl.multiple_of(c * _SC_LANES, _SC_LANES), _SC_LANES)] = jnp.zeros((_SC_LANES,), F32)

            for ch in range(nch):
                b = ch % 2
                g = t * nch + ch
                gather(g, b).wait()

                @pl.when(g + 1 < tpw * nch)
                def _():
                    gather(g + 1, 1 - b).start()

                ws = [plsc.load_gather(w_v, [jnp.full((_SC_LANES,), HK + ch * CH + r, jnp.int32)])
                      for r in range(CH)]

                @pl.loop(0, nsl, step=2)
                def _(c):
                    for half in range(2):
                        off = pl.multiple_of((c + half) * _SC_LANES, _SC_LANES)
                        blk = (c + half) // (LANE // _SC_LANES)
                        lo = pl.multiple_of(off - blk * LANE, _SC_LANES)
                        parts = [ws[r] * bufs[b][r, blk, pl.ds(lo, _SC_LANES)] for r in range(CH)]
                        while len(parts) > 1:
                            parts = [parts[p] + parts[p + 1] for p in range(0, len(parts), 2)]
                        o_v[pl.ds(off, _SC_LANES)] = o_v[pl.ds(off, _SC_LANES)] + parts[0]

            pltpu.sync_copy(o_v, o_hbm.at[tok0 + t])

    return k(v_tab, idx_flat, wgt)


def _peer_fin_kernel(x_ref, p_ref, ga_ref, gf_ref, o_ref):
    o_ref[...] = _rms(x_ref[...] + ga_ref[0] * p_ref[...], gf_ref[...])


def _peer_fin(x1, po, mod3, g_final, S):
    T, D = x1.shape
    tm = 512
    per_b = S // tm
    return pl.pallas_call(
        _peer_fin_kernel,
        grid=(T // tm,),
        in_specs=[pl.BlockSpec((tm, D), lambda i: (i, 0)),
                  pl.BlockSpec((tm, D), lambda i: (i, 0)),
                  pl.BlockSpec((1, 1, D), lambda i: ((i // per_b) * 6 + 5, 0, 0)),
                  pl.BlockSpec((1, D), lambda i: (0, 0))],
        out_specs=pl.BlockSpec((tm, D), lambda i: (i, 0)),
        out_shape=jax.ShapeDtypeStruct((T, D), F32),
        compiler_params=_params(("parallel",), 40),
        name="peer_fin",
    )(x1, po, mod3, g_final)


def _pad_gate(w, lo):
    rank = w.shape[0]
    wh = w.reshape(rank, GLA_HEADS, GLA_DK).transpose(1, 0, 2)
    return jnp.zeros((GLA_HEADS, LANE, GLA_DK), F32).at[:, lo:lo + rank, :].set(wh)


def kernel(x, c, positions, w_ada, b_ada, g_norm_mix, w_in, w_gate_f, b_gate_f, w_gate_b, b_gate_b,
           g_gla_out, w_out, g_norm_ffn, w_peer_q, peer_sub_keys, peer_u, peer_v, g_final):
    B, S, D = x.shape
    T = B * S
    depth = w_ada.shape[0]
    assert depth == 1, "the final norm is fused into the last PEER call; one layer only"
    xt = x.reshape(T, D)
    cs, sn = _rope_tables(positions.reshape(T, 1))
    gz0 = 2 * GLA_HEADS * GLA_DK + 2 * GLA_HEADS * GLA_DV
    gz1 = gz0 + 2 * GLA_GATE_RANK
    for l in range(depth):
        mod3 = _ada(c, w_ada[l], b_ada[l]).reshape(B * 6, 1, D)
        w_main = jnp.concatenate([w_in[l][:, :gz0], w_in[l][:, gz1:]], axis=1).astype(BF16)
        w_z = jnp.pad(w_in[l][:, gz0:gz1], ((0, 0), (0, LANE - (gz1 - gz0)))).astype(BF16)
        proj, gz = _inproj(xt, g_norm_mix[l].reshape(1, D), mod3, w_main, w_z, S)
        o_gla = _gla(proj, gz,
                     _pad_gate(w_gate_f[l], 0), b_gate_f[l].reshape(GLA_HEADS, 1, GLA_DK),
                     _pad_gate(w_gate_b[l], GLA_GATE_RANK), b_gate_b[l].reshape(GLA_HEADS, 1, GLA_DK),
                     g_gla_out[l].reshape(GLA_HEADS, 1, GLA_DV), B, S)
        o_dil = _dil(proj, cs, sn, B, S)
        x1 = _outproj(o_gla, o_dil, w_out[l].astype(BF16), xt, mod3, S)
        qp = _pq(x1, g_norm_ffn[l].reshape(1, D), mod3, w_peer_q[l].astype(BF16), S)
        idx_t, gate_t = _topk(qp, peer_sub_keys[l].astype(BF16))
        idx_flat = idx_t.T.reshape(-1)
        E = peer_u.shape[1]
        u3 = peer_u[l].reshape(E, D // LANE, LANE)
        v3 = peer_v[l].reshape(E, D // LANE, LANE)
        steps = T // _PEER_TB // _PEER_CHUNKS
        outs = []
        for ck in range(_PEER_CHUNKS):
            wgt = _peer_u(idx_flat, gate_t, x1, g_norm_ffn[l].reshape(1, D), mod3, u3, S, ck * steps, steps)
            outs.append(_sc_peer_v(v3, idx_flat, wgt, ck * steps * _PEER_TB))
        xt = _peer_fin(x1, jnp.concatenate(outs, axis=0), mod3, g_final.reshape(1, D), S)
    return xt.reshape(B, S, D)
```

```python
import functools

import jax
import jax.numpy as jnp
from jax import lax
from jax.experimental import pallas as pl
from jax.experimental.pallas import tpu as pltpu
from jax.experimental.pallas import tpu_sc as plsc

F32 = jnp.float32
BF16 = jnp.bfloat16
HIGHEST = lax.Precision.HIGHEST

NORM_EPS = 1e-6
GLA_HEADS = 4
GLA_DK = 128
GLA_DV = 256
GLA_GATE_RANK = 16
GLA_TAU = 16.0
GLA_CHUNK = 64
DIL_HD = 128
DIL_HEADS = 8
DIL_PATTERNS = ((128, 1), (512, 4), (2048, 16))
ROPE_THETA = 10000.0
NEG_INF = -1e30
PEER_HEADS = 8
PEER_NKEYS = 128
PEER_TOPK = 16
PEER_HALF = 128

LANE = 128
MIB = 1024 * 1024

_COL_GQ, _COL_GK, _COL_GV, _COL_GR, _COL_DQ, _COL_DK, _COL_DV = 0, 4, 8, 16, 24, 32, 40
_PROJ_W = 48 * LANE

_NT = (((1,), (1,)), ((), ()))
_TN = (((0,), (0,)), ((), ()))


def _params(sem, vmem_mib):
    return pltpu.CompilerParams(dimension_semantics=sem, vmem_limit_bytes=vmem_mib * MIB)


def _rms(x, g):
    return x * lax.rsqrt(jnp.mean(x * x, axis=-1, keepdims=True) + NORM_EPS) * g


def _silu(x):
    return x / (1.0 + jnp.exp(-x))


def _ada_kernel(c_ref, w_ref, b_ref, o_ref):
    s = _silu(c_ref[...]).astype(BF16)
    o_ref[...] = jnp.dot(s, w_ref[...].astype(BF16), preferred_element_type=F32) + b_ref[...]


def _ada(c, w, b):
    B, D = c.shape
    N = w.shape[1]
    tn = 1024
    cp = jnp.zeros((8, D), F32).at[:B].set(c)
    out = pl.pallas_call(
        _ada_kernel,
        grid=(N // tn,),
        in_specs=[pl.BlockSpec((8, D), lambda j: (0, 0)),
                  pl.BlockSpec((D, tn), lambda j: (0, j)),
                  pl.BlockSpec((1, tn), lambda j: (0, j))],
        out_specs=pl.BlockSpec((8, tn), lambda j: (0, j)),
        out_shape=jax.ShapeDtypeStruct((8, N), F32),
        compiler_params=_params(("parallel",), 40),
        name="ada",
    )(cp, w, b.reshape(1, N))
    return out[:B]


def _inproj_kernel(x_ref, g_ref, sc_ref, sh_ref, w_ref, wz_ref, o_ref, z_ref, h_scr):
    @pl.when(pl.program_id(1) == 0)
    def _():
        h = _rms(x_ref[...], g_ref[...]) * (1.0 + sc_ref[0]) + sh_ref[0]
        hb = h.astype(BF16)
        h_scr[...] = hb
        z_ref[...] = jnp.dot(hb, wz_ref[...], preferred_element_type=F32)

    o_ref[...] = jnp.dot(h_scr[...], w_ref[...], preferred_element_type=F32)


def _inproj(x2, g, mod3, w_main, w_z, S):
    T, D = x2.shape
    tm, tn = 1024, 768
    per_b = S // tm
    return pl.pallas_call(
        _inproj_kernel,
        grid=(T // tm, _PROJ_W // tn),
        in_specs=[pl.BlockSpec((tm, D), lambda i, j: (i, 0)),
                  pl.BlockSpec((1, D), lambda i, j: (0, 0)),
                  pl.BlockSpec((1, 1, D), lambda i, j: ((i // per_b) * 6 + 1, 0, 0)),
                  pl.BlockSpec((1, 1, D), lambda i, j: ((i // per_b) * 6 + 0, 0, 0)),
                  pl.BlockSpec((D, tn), lambda i, j: (0, j)),
                  pl.BlockSpec((D, LANE), lambda i, j: (0, 0))],
        out_specs=[pl.BlockSpec((tm, tn), lambda i, j: (i, j)),
                   pl.BlockSpec((tm, LANE), lambda i, j: (i, 0))],
        out_shape=[jax.ShapeDtypeStruct((T, _PROJ_W), F32),
                   jax.ShapeDtypeStruct((T, LANE), F32)],
        scratch_shapes=[pltpu.VMEM((tm, D), BF16)],
        compiler_params=_params(("parallel", "arbitrary"), 48),
        name="inproj",
    )(x2, g, mod3, mod3, w_main, w_z)


def _rope_kernel(pos_ref, f_ref, sg_ref, cs_ref, sn_ref):
    ang = pos_ref[...].astype(F32) * f_ref[...]
    cs_ref[...] = jnp.cos(ang)
    sn_ref[...] = jnp.sin(ang) * sg_ref[...]


def _rope_tables(pos_col):
    T = pos_col.shape[0]
    half = DIL_HD // 2
    inv = jnp.power(ROPE_THETA, -jnp.arange(half, dtype=F32) * 2.0 / DIL_HD)
    freq = jnp.concatenate([inv, inv]).reshape(1, DIL_HD)
    sign = jnp.concatenate([-jnp.ones((half,), F32), jnp.ones((half,), F32)]).reshape(1, DIL_HD)
    tm = 1024
    return pl.pallas_call(
        _rope_kernel,
        grid=(T // tm,),
        in_specs=[pl.BlockSpec((tm, 1), lambda i: (i, 0)),
                  pl.BlockSpec((1, DIL_HD), lambda i: (0, 0)),
                  pl.BlockSpec((1, DIL_HD), lambda i: (0, 0))],
        out_specs=[pl.BlockSpec((tm, DIL_HD), lambda i: (i, 0)),
                   pl.BlockSpec((tm, DIL_HD), lambda i: (i, 0))],
        out_shape=[jax.ShapeDtypeStruct((T, DIL_HD), F32)] * 2,
        compiler_params=_params(("parallel",), 32),
        name="rope",
    )(pos_col, freq, sign)


def _gla_kernel(q_ref, k_ref, v_ref, r_ref, z_ref, wgf_ref, bgf_ref, wgb_ref, bgb_ref, g_ref,
                o_ref, laf_scr, lab_scr, of_scr):
    S = q_ref.shape[0]
    C = GLA_CHUNK
    n = S // C
    scale = GLA_DK ** -0.5
    row = lax.broadcasted_iota(jnp.int32, (C, C), 0)
    col = lax.broadcasted_iota(jnp.int32, (C, C), 1)

    def log_gate(w_ref, b_ref):
        zz = jnp.dot(z_ref[...], w_ref[0], precision=HIGHEST, preferred_element_type=F32) + b_ref[0]
        return (jnp.minimum(zz, 0.0) - jnp.log(1.0 + jnp.exp(-jnp.abs(zz)))) * (1.0 / GLA_TAU)

    def chunk(fwd, c, st_t, la_scr, o_scr):
        keep = (col <= row) if fwd else (col >= row)
        sl = pl.ds(pl.multiple_of(c * C, C), C)
        cum = la_scr[sl, :]
        tot = cum[C - 1:C, :] if fwd else cum[0:1, :]
        kk = k_ref[sl, :]
        qd = (q_ref[sl, :] * scale * jnp.exp(cum)).astype(BF16)
        ki = (kk * jnp.exp(-cum)).astype(BF16)
        kte = (kk * jnp.exp(tot - cum)).astype(BF16)
        vb = v_ref[sl, :].astype(BF16)
        attn = lax.dot_general(qd, ki, _NT, preferred_element_type=F32)
        attn = jnp.where(keep, attn, 0.0).astype(BF16)
        o = jnp.dot(attn, vb, preferred_element_type=F32)
        o_scr[sl, :] = o + lax.dot_general(qd, st_t.astype(BF16), _NT, preferred_element_type=F32)
        upd = lax.dot_general(vb, kte, _TN, preferred_element_type=F32)
        return st_t * jnp.exp(tot) + upd

    laf_scr[...] = log_gate(wgf_ref, bgf_ref)
    lab_scr[...] = log_gate(wgb_ref, bgb_ref)

    def cumulate(c, carry):
        sl = pl.ds(pl.multiple_of(c * C, C), C)
        laf_scr[sl, :] = jnp.dot((col <= row).astype(F32), laf_scr[sl, :], precision=HIGHEST,
                                 preferred_element_type=F32)
        lab_scr[sl, :] = jnp.dot((col >= row).astype(F32), lab_scr[sl, :], precision=HIGHEST,
                                 preferred_element_type=F32)
        return carry

    lax.fori_loop(0, n, cumulate, 0, unroll=4)

    def body(i, states):
        return (chunk(True, i, states[0], laf_scr, of_scr),
                chunk(False, n - 1 - i, states[1], lab_scr, o_ref))

    zero = jnp.zeros((GLA_DV, GLA_DK), F32)
    lax.fori_loop(0, n, body, (zero, zero), unroll=4)
    y = _rms(of_scr[...] + o_ref[...], g_ref[0])
    o_ref[...] = y * _silu(r_ref[...])


def _gla(proj, gz, wgf, bgf, wgb, bgb, g_out, B, S):
    T = proj.shape[0]
    H = GLA_HEADS
    return pl.pallas_call(
        _gla_kernel,
        grid=(B, H),
        in_specs=[pl.BlockSpec((S, GLA_DK), lambda b, h: (b, _COL_GQ + h)),
                  pl.BlockSpec((S, GLA_DK), lambda b, h: (b, _COL_GK + h)),
                  pl.BlockSpec((S, GLA_DV), lambda b, h: (b, _COL_GV // 2 + h)),
                  pl.BlockSpec((S, GLA_DV), lambda b, h: (b, _COL_GR // 2 + h)),
                  pl.BlockSpec((S, LANE), lambda b, h: (b, 0)),
                  pl.BlockSpec((1, LANE, GLA_DK), lambda b, h: (h, 0, 0)),
                  pl.BlockSpec((1, 1, GLA_DK), lambda b, h: (h, 0, 0)),
                  pl.BlockSpec((1, LANE, GLA_DK), lambda b, h: (h, 0, 0)),
                  pl.BlockSpec((1, 1, GLA_DK), lambda b, h: (h, 0, 0)),
                  pl.BlockSpec((1, 1, GLA_DV), lambda b, h: (h, 0, 0))],
        out_specs=pl.BlockSpec((S, GLA_DV), lambda b, h: (b, h)),
        out_shape=jax.ShapeDtypeStruct((T, GLA_HEADS * GLA_DV), F32),
        scratch_shapes=[pltpu.VMEM((S, GLA_DK), F32), pltpu.VMEM((S, GLA_DK), F32),
                        pltpu.VMEM((S, GLA_DV), F32)],
        compiler_params=_params(("parallel", "parallel"), 56),
        name="gla",
    )(proj, proj, proj, proj, gz, wgf, bgf, wgb, bgb, g_out)


_DIL_QB = 128
_DIL_HALF = 64
_DIL_KW = _DIL_QB + 2 * _DIL_HALF


def _dil_kernel(q_ref, k_ref, v_ref, cs_ref, sn_ref, o_ref, qf_scr, kf_scr, qb_scr, kb_scr, vb_scr, lse_scr):
    S = q_ref.shape[0]
    QB, KW, HALF = _DIL_QB, _DIL_KW, _DIL_HALF
    rot = DIL_HD // 2
    cs, sn = cs_ref[...], sn_ref[...]
    q = q_ref[...]
    qf_scr[...] = (q * cs + pltpu.roll(q, rot, 1) * sn) * (DIL_HD ** -0.5)
    k = k_ref[...]
    kf_scr[...] = k * cs + pltpu.roll(k, rot, 1) * sn
    rel = (lax.broadcasted_iota(jnp.int32, (QB, KW), 1) - lax.broadcasted_iota(jnp.int32, (QB, KW), 0))

    for pi, (window, d) in enumerate(DIL_PATTERNS):
        assert window // (2 * d) == HALF
        M = S // d
        per = M // QB
        for r in range(d):
            rows = pl.ds(r * M, M)
            src = pl.ds(r, M, stride=d) if d > 1 else pl.ds(0, M)
            qb_scr[rows, :] = qf_scr[src, :].astype(BF16)
            kb_scr[rows, :] = kf_scr[src, :].astype(BF16)
            vb_scr[rows, :] = v_ref[src, :].astype(BF16)

        for r in range(d):
            def block(mb, carry, r=r, first=(pi == 0)):
                m0 = mb * QB
                start = jnp.clip(m0 - HALF, 0, M - KW)
                qrow = pl.multiple_of(r * M + m0, QB)
                krow = pl.multiple_of(r * M + start, HALF)
                s = lax.dot_general(qb_scr[pl.ds(qrow, QB), :], kb_scr[pl.ds(krow, KW), :], _NT,
                                    preferred_element_type=F32)
                s = jnp.where(jnp.abs(rel + (start - m0)) <= HALF, s, NEG_INF)
                m = jnp.max(s, axis=-1, keepdims=True)
                p = jnp.exp(s - m)
                l = jnp.sum(p, axis=-1, keepdims=True)
                o = jnp.dot(p.astype(BF16), vb_scr[pl.ds(krow, KW), :], preferred_element_type=F32) / l
                ls = jnp.broadcast_to(m + jnp.log(l), (QB, DIL_HD))
                base = pl.multiple_of(m0 * d, QB)
                nat = pl.ds(base + r, QB, stride=d) if d > 1 else pl.ds(base, QB)
                if first:
                    o_ref[nat, :] = o
                    lse_scr[nat, :] = ls
                else:
                    o0, l0 = o_ref[nat, :], lse_scr[nat, :]
                    mx = jnp.maximum(l0, ls)
                    w0, w1 = jnp.exp(l0 - mx), jnp.exp(ls - mx)
                    den = w0 + w1
                    o_ref[nat, :] = (w0 * o0 + w1 * o) / den
                    lse_scr[nat, :] = mx + jnp.log(den)
                return carry

            lax.fori_loop(0, per, block, 0, unroll=min(per, 8))


def _dil(proj, cs, sn, B, S):
    T = proj.shape[0]
    blk = lambda col: pl.BlockSpec((S, DIL_HD), lambda b, h: (b, col + h))
    return pl.pallas_call(
        _dil_kernel,
        grid=(B, DIL_HEADS),
        in_specs=[blk(_COL_DQ), blk(_COL_DK), blk(_COL_DV),
                  pl.BlockSpec((S, DIL_HD), lambda b, h: (b, 0)),
                  pl.BlockSpec((S, DIL_HD), lambda b, h: (b, 0))],
        out_specs=pl.BlockSpec((S, DIL_HD), lambda b, h: (b, h)),
        out_shape=jax.ShapeDtypeStruct((T, DIL_HEADS * DIL_HD), F32),
        scratch_shapes=[pltpu.VMEM((S, DIL_HD), F32), pltpu.VMEM((S, DIL_HD), F32),
                        pltpu.VMEM((S, DIL_HD), BF16), pltpu.VMEM((S, DIL_HD), BF16),
                        pltpu.VMEM((S, DIL_HD), BF16), pltpu.VMEM((S, DIL_HD), F32)],
        compiler_params=_params(("parallel", "parallel"), 48),
        name="dil",
    )(proj, proj, proj, cs, sn)


def _outproj_kernel(og_ref, od_ref, w_ref, x_ref, ga_ref, o_ref):
    kg = og_ref.shape[1]
    mixed = jnp.dot(og_ref[...].astype(BF16), w_ref[:kg, :], preferred_element_type=F32)
    mixed = mixed + jnp.dot(od_ref[...].astype(BF16), w_ref[kg:, :], preferred_element_type=F32)
    o_ref[...] = x_ref[...] + ga_ref[0] * mixed


def _outproj(o_gla, o_dil, w_out, x2, mod3, S):
    T, D = x2.shape
    tm = 512
    per_b = S // tm
    kg, kd = o_gla.shape[1], o_dil.shape[1]
    return pl.pallas_call(
        _outproj_kernel,
        grid=(T // tm,),
        in_specs=[pl.BlockSpec((tm, kg), lambda i: (i, 0)),
                  pl.BlockSpec((tm, kd), lambda i: (i, 0)),
                  pl.BlockSpec((kg + kd, D), lambda i: (0, 0)),
                  pl.BlockSpec((tm, D), lambda i: (i, 0)),
                  pl.BlockSpec((1, 1, D), lambda i: ((i // per_b) * 6 + 2, 0, 0))],
        out_specs=pl.BlockSpec((tm, D), lambda i: (i, 0)),
        out_shape=jax.ShapeDtypeStruct((T, D), F32),
        compiler_params=_params(("parallel",), 48),
        name="outproj",
    )(o_gla, o_dil, w_out, x2, mod3)


def _pq_kernel(x_ref, g_ref, sc_ref, sh_ref, w_ref, o_ref):
    h = _rms(x_ref[...], g_ref[...]) * (1.0 + sc_ref[0]) + sh_ref[0]
    o_ref[...] = jnp.dot(h.astype(BF16), w_ref[...], preferred_element_type=F32)


def _pq(x1, g, mod3, wq, S):
    T, D = x1.shape
    N = wq.shape[1]
    tm = 512
    per_b = S // tm
    return pl.pallas_call(
        _pq_kernel,
        grid=(T // tm,),
        in_specs=[pl.BlockSpec((tm, D), lambda i: (i, 0)),
                  pl.BlockSpec((1, D), lambda i: (0, 0)),
                  pl.BlockSpec((1, 1, D), lambda i: ((i // per_b) * 6 + 4, 0, 0)),
                  pl.BlockSpec((1, 1, D), lambda i: ((i // per_b) * 6 + 3, 0, 0)),
                  pl.BlockSpec((D, N), lambda i: (0, 0))],
        out_specs=pl.BlockSpec((tm, N), lambda i: (i, 0)),
        out_shape=jax.ShapeDtypeStruct((T, N), F32),
        compiler_params=_params(("parallel",), 48),
        name="pq",
    )(x1, g, mod3, mod3, wq)


def _top_rows(s, k, payload=None):
    n_rows = s.shape[0]
    rid = lax.broadcasted_iota(jnp.int32, s.shape, 0).astype(F32)
    vals, picks = [], []
    for _ in range(k):
        m = jnp.max(s, axis=0, keepdims=True)
        pos = jnp.min(jnp.where(s == m, rid, float(n_rows)), axis=0, keepdims=True)
        hit = rid == pos
        vals.append(m)
        if payload is None:
            picks.append(pos)
        else:
            picks.append(jnp.sum(jnp.where(hit, payload, 0), axis=0, keepdims=True))
        s = jnp.where(hit, -jnp.inf, s)
    picks = jnp.concatenate(picks, axis=0)
    return jnp.concatenate(vals, axis=0), picks.astype(jnp.int32)


def _staircase(a, b, combine, fill):
    K = a.shape[0]
    half = K // 2
    jrow = lax.broadcasted_iota(jnp.int32, (half, a.shape[1]), 0)
    pieces = [combine(a[0:1], b)]
    for i in range(1, half):
        piece = combine(a[i:i + 1], b[0:half])
        width = K // (i + 1)
        pieces.append(piece if width >= half else jnp.where(jrow < width, piece, fill))
    pieces.append(combine(a[half:K], b[0:1]))
    return jnp.concatenate(pieces, axis=0)


def _topk_kernel(q_ref, keys_ref, idx_ref, gate_ref):
    K = PEER_TOPK
    for h in range(PEER_HEADS):
        tops = []
        for half in range(2):
            c0 = (h * 2 + half) * PEER_HALF
            qh = q_ref[:, c0:c0 + PEER_HALF].astype(BF16)
            sc = lax.dot_general(keys_ref[h, half], qh, _NT, preferred_element_type=F32)
            tops.append(_top_rows(sc, K))
        (s0, i0), (s1, i1) = tops
        cand_s = _staircase(s0, s1, lambda a, b: a + b, -jnp.inf)
        cand_i = _staircase(i0, i1, lambda a, b: a * PEER_NKEYS + b, 0)
        best, idx = _top_rows(cand_s, K, payload=cand_i)
        e = jnp.exp(best - best[0:1])
        gate = e / jnp.sum(e, axis=0, keepdims=True)
        idx_ref[h * K:(h + 1) * K, :] = idx
        gate_ref[h * K:(h + 1) * K, :] = gate


def _topk(qp, keys_bf):
    T, N = qp.shape
    tt = 256
    HK = PEER_HEADS * PEER_TOPK
    return pl.pallas_call(
        _topk_kernel,
        grid=(T // tt,),
        in_specs=[pl.BlockSpec((tt, N), lambda i: (i, 0)),
                  pl.BlockSpec(keys_bf.shape, lambda i: (0, 0, 0, 0))],
        out_specs=[pl.BlockSpec((HK, tt), lambda i: (0, i)),
                   pl.BlockSpec((HK, tt), lambda i: (0, i))],
        out_shape=[jax.ShapeDtypeStruct((HK, T), jnp.int32),
                   jax.ShapeDtypeStruct((HK, T), F32)],
        compiler_params=_params(("parallel",), 32),
        name="topk",
    )(qp, keys_bf)


_PEER_TB = 128
_PEER_SUB = 8
_SUBLANES = 8
_PEER_CHUNKS = 32
_SC_LANES = 16
_SC_ROWS = 16


def _peer_u_kernel(idx_hbm, gate_ref, x1_ref, gn_ref, sc_ref, sh_ref, u_hbm, w_ref,
                   idx_smem, ub0, ub1, h_scr, sem_i, sem_u, *, step0):
    HK = PEER_HEADS * PEER_TOPK
    TB, SUB = _PEER_TB, _PEER_SUB
    R = SUB * HK
    N = TB * HK
    nsub = TB // SUB
    D = x1_ref.shape[1]
    nchunk = D // LANE
    tiles = HK // _SUBLANES
    i = pl.program_id(0)
    n = pl.num_programs(0)
    cur = lax.rem(i, 2) * N
    nxt = N - cur
    more = i + 1 < n
    ubufs = (ub0, ub1)

    def idx_copy(step, base):
        return pltpu.make_async_copy(idx_hbm.at[pl.ds((step0 + step) * N, N)],
                                     idx_smem.at[pl.ds(base, N)], sem_i)

    def issue_token(base, t, slot):
        for k in range(HK):
            e = idx_smem[base + t * HK + k]
            rt, s = t * tiles + k // _SUBLANES, k % _SUBLANES
            pltpu.make_async_copy(u_hbm.at[e], ubufs[slot].at[rt, :, s, :],
                                  sem_u.at[slot]).start(priority=k % 2)

    def wait(slot):
        pltpu.make_async_copy(ubufs[slot], ubufs[slot], sem_u.at[slot]).wait()

    @pl.when(i == 0)
    def _():
        first = idx_copy(0, 0)
        first.start()
        first.wait()
        for t in range(SUB):
            issue_token(0, t, 0)

    @pl.when(more)
    def _():
        idx_copy(i + 1, nxt).start()

    h_scr[...] = _rms(x1_ref[...], gn_ref[...]) * (1.0 + sc_ref[0]) + sh_ref[0]
    lane = lax.broadcasted_iota(jnp.int32, (HK, TB), 1)

    def compute_token(j, t, slot, wacc):
        ub = ubufs[slot]
        tok = j * SUB + t
        xt = h_scr[pl.ds(tok, 1), :]
        rows = slice(t * tiles, (t + 1) * tiles)
        part = ub[rows, 0].reshape(HK, LANE) * xt[:, 0:LANE]
        for c in range(1, nchunk):
            part = part + ub[rows, c].reshape(HK, LANE) * xt[:, c * LANE:(c + 1) * LANE]
        a = jnp.sum(part, axis=1, keepdims=True)
        hit = lane == tok
        g = jnp.sum(jnp.where(hit, gate_ref[...], 0.0), axis=1, keepdims=True)
        wgt = g * (0.5 * a * (1.0 + lax.erf(a * (2.0 ** -0.5))))
        return jnp.where(hit, wgt, wacc)

    def half(j, slot, next_base, wacc):
        wait(slot)
        for t in range(SUB):
            issue_token(next_base, t, 1 - slot)
            wacc = compute_token(j, t, slot, wacc)
        return wacc

    def pair(jj, wacc):
        j0 = 2 * jj
        wacc = half(j0, 0, cur + (j0 + 1) * R, wacc)
        last = jj == nsub // 2 - 1

        @pl.when(jnp.logical_and(last, more))
        def _():
            idx_copy(i + 1, nxt).wait()

        after = jnp.where(more, nxt, cur)
        return half(j0 + 1, 1, jnp.where(last, after, cur + (j0 + 2) * R), wacc)

    wacc = lax.fori_loop(0, nsub // 2, pair, jnp.zeros((HK, TB), F32))

    @pl.when(jnp.logical_not(more))
    def _():
        wait(0)

    w_ref[:, :HK] = jnp.zeros((TB, HK), F32)
    w_ref[:, HK:] = wacc.T


def _peer_u(idx_flat, gate_t, x1, g_norm, mod3, u3, S, step0, nsteps):
    T, D = x1.shape
    HK = PEER_HEADS * PEER_TOPK
    TB, SUB = _PEER_TB, _PEER_SUB
    per_b = S // TB
    modrow = lambda k: (lambda i: (((step0 + i) // per_b) * 6 + k, 0, 0))
    gbuf = pltpu.VMEM((SUB * HK // _SUBLANES, D // LANE, _SUBLANES, LANE), F32)
    return pl.pallas_call(
        functools.partial(_peer_u_kernel, step0=step0),
        grid=(nsteps,),
        in_specs=[pl.BlockSpec(memory_space=pl.ANY),
                  pl.BlockSpec((HK, TB), lambda i: (0, step0 + i)),
                  pl.BlockSpec((TB, D), lambda i: (step0 + i, 0)),
                  pl.BlockSpec((1, D), lambda i: (0, 0)),
                  pl.BlockSpec((1, 1, D), modrow(4)),
                  pl.BlockSpec((1, 1, D), modrow(3)),
                  pl.BlockSpec(memory_space=pl.ANY)],
        out_specs=pl.BlockSpec((TB, 2 * HK), lambda i: (i, 0)),
        out_shape=jax.ShapeDtypeStruct((nsteps * TB, 2 * HK), F32),
        scratch_shapes=[pltpu.SMEM((2 * TB * HK,), jnp.int32),
                        gbuf, gbuf,
                        pltpu.VMEM((TB, D), F32),
                        pltpu.SemaphoreType.DMA,
                        pltpu.SemaphoreType.DMA((2,))],
        compiler_params=_params(("arbitrary",), 40),
        name="peer_u",
    )(idx_flat, gate_t, x1, g_norm, mod3, mod3, u3)


def _sc_peer_v(v_tab, idx_flat, wgt, tok_base):
    E, nblk, _ = v_tab.shape
    D = nblk * LANE
    Tc, HK = wgt.shape[0], wgt.shape[1] // 2
    info = plsc.get_sparse_core_info()
    nw = info.num_cores * info.num_subcores
    tpw = Tc // nw
    CH = _SC_ROWS
    nch = HK // CH
    nsl = D // _SC_LANES
    mesh = plsc.VectorSubcoreMesh(core_axis_name="c", subcore_axis_name="s")

    @functools.partial(
        pl.kernel, mesh=mesh, out_type=jax.ShapeDtypeStruct((Tc, D), F32),
        scratch_types=[pltpu.VMEM((tpw * HK,), jnp.int32), pltpu.VMEM((2 * HK,), F32),
                       pltpu.VMEM((D,), F32),
                       pltpu.VMEM((CH, nblk, LANE), F32), pltpu.VMEM((CH, nblk, LANE), F32),
                       pltpu.SemaphoreType.DMA, pltpu.SemaphoreType.DMA],
        compiler_params=pltpu.CompilerParams(needs_layout_passes=False),
        name="sc_peer_v",
    )
    def k(tab_hbm, idx_hbm, w_hbm, o_hbm, idx_v, w_v, o_v, buf0, buf1, g0, g1):
        wid = lax.axis_index("s") * info.num_cores + lax.axis_index("c")
        tok0 = wid * tpw
        pltpu.sync_copy(idx_hbm.at[pl.ds((tok_base + tok0) * HK, tpw * HK)], idx_v)
        bufs, gs = (buf0, buf1), (g0, g1)

        def gather(g, b):
            return pltpu.make_async_copy(tab_hbm.at[idx_v.at[pl.ds(g * CH, CH)]], bufs[b], gs[b])

        gather(0, 0).start()

        @pl.loop(0, tpw)
        def _(t):
            pltpu.sync_copy(w_hbm.at[tok0 + t], w_v)

            @pl.loop(0, nsl)
            def _(c):
                o_v[pl.ds(pl.multiple_of(c * _SC_LANES, _SC_LANES), _SC_LANES)] = jnp.zeros((_SC_LANES,), F32)

            for ch in range(nch):
                b = ch % 2
                g = t * nch + ch
                gather(g, b).wait()

                @pl.when(g + 1 < tpw * nch)
                def _():
                    gather(g + 1, 1 - b).start()

                ws = [plsc.load_gather(w_v, [jnp.full((_SC_LANES,), HK + ch * CH + r, jnp.int32)])
                      for r in range(CH)]

                @pl.loop(0, nsl, step=2)
                def _(c):
                    for half in range(2):
                        off = pl.multiple_of((c + half) * _SC_LANES, _SC_LANES)
                        blk = (c + half) // (LANE // _SC_LANES)
                        lo = pl.multiple_of(off - blk * LANE, _SC_LANES)
                        parts = [ws[r] * bufs[b][r, blk, pl.ds(lo, _SC_LANES)] for r in range(CH)]
                        while len(parts) > 1:
                            parts = [parts[p] + parts[p + 1] for p in range(0, len(parts), 2)]
                        o_v[pl.ds(off, _SC_LANES)] = o_v[pl.ds(off, _SC_LANES)] + parts[0]

            pltpu.sync_copy(o_v, o_hbm.at[tok0 + t])

    return k(v_tab, idx_flat, wgt)


def _peer_fin_kernel(x_ref, p_ref, ga_ref, gf_ref, *rest):
    o_ref = rest[-1]
    o_ref[...] = _rms(x_ref[...] + ga_ref[0] * p_ref[...], gf_ref[...])


def _peer_fin(x1, po, mod3, g_final, S, tok_base, out_prev):
    T, D = x1.shape
    Tc = po.shape[0]
    tm = min(Tc, 512)
    per_b = S // tm
    blk0 = tok_base // tm
    in_specs = [pl.BlockSpec((tm, D), lambda i: (blk0 + i, 0)),
                pl.BlockSpec((tm, D), lambda i: (i, 0)),
                pl.BlockSpec((1, 1, D), lambda i: (((blk0 + i) // per_b) * 6 + 5, 0, 0)),
                pl.BlockSpec((1, D), lambda i: (0, 0))]
    args = [x1, po, mod3, g_final]
    aliases = {}
    if out_prev is not None:
        in_specs.append(pl.BlockSpec(memory_space=pl.ANY))
        args.append(out_prev)
        aliases = {4: 0}
    return pl.pallas_call(
        _peer_fin_kernel,
        grid=(Tc // tm,),
        in_specs=in_specs,
        out_specs=pl.BlockSpec((tm, D), lambda i: (blk0 + i, 0)),
        out_shape=jax.ShapeDtypeStruct((T, D), F32),
        input_output_aliases=aliases,
        compiler_params=_params(("arbitrary",), 40),
        name="peer_fin",
    )(*args)


def _pad_gate(w, lo):
    rank = w.shape[0]
    wh = w.reshape(rank, GLA_HEADS, GLA_DK).transpose(1, 0, 2)
    return jnp.zeros((GLA_HEADS, LANE, GLA_DK), F32).at[:, lo:lo + rank, :].set(wh)


def kernel(x, c, positions, w_ada, b_ada, g_norm_mix, w_in, w_gate_f, b_gate_f, w_gate_b, b_gate_b,
           g_gla_out, w_out, g_norm_ffn, w_peer_q, peer_sub_keys, peer_u, peer_v, g_final):
    B, S, D = x.shape
    T = B * S
    depth = w_ada.shape[0]
    assert depth == 1, "the final norm is fused into the last PEER call; one layer only"
    xt = x.reshape(T, D)
    cs, sn = _rope_tables(positions.reshape(T, 1))
    gz0 = 2 * GLA_HEADS * GLA_DK + 2 * GLA_HEADS * GLA_DV
    gz1 = gz0 + 2 * GLA_GATE_RANK
    for l in range(depth):
        mod3 = _ada(c, w_ada[l], b_ada[l]).reshape(B * 6, 1, D)
        w_main = jnp.concatenate([w_in[l][:, :gz0], w_in[l][:, gz1:]], axis=1).astype(BF16)
        w_z = jnp.pad(w_in[l][:, gz0:gz1], ((0, 0), (0, LANE - (gz1 - gz0)))).astype(BF16)
        proj, gz = _inproj(xt, g_norm_mix[l].reshape(1, D), mod3, w_main, w_z, S)
        o_gla = _gla(proj, gz,
                     _pad_gate(w_gate_f[l], 0), b_gate_f[l].reshape(GLA_HEADS, 1, GLA_DK),
                     _pad_gate(w_gate_b[l], GLA_GATE_RANK), b_gate_b[l].reshape(GLA_HEADS, 1, GLA_DK),
                     g_gla_out[l].reshape(GLA_HEADS, 1, GLA_DV), B, S)
        o_dil = _dil(proj, cs, sn, B, S)
        x1 = _outproj(o_gla, o_dil, w_out[l].astype(BF16), xt, mod3, S)
        qp = _pq(x1, g_norm_ffn[l].reshape(1, D), mod3, w_peer_q[l].astype(BF16), S)
        idx_t, gate_t = _topk(qp, peer_sub_keys[l].astype(BF16))
        idx_flat = idx_t.T.reshape(-1)
        E = peer_u.shape[1]
        u3 = peer_u[l].reshape(E, D // LANE, LANE)
        v3 = peer_v[l].reshape(E, D // LANE, LANE)
        steps = T // _PEER_TB // _PEER_CHUNKS
        out, pending = None, []
        for ck in range(_PEER_CHUNKS + 2):
            if ck < _PEER_CHUNKS:
                wgt = _peer_u(idx_flat, gate_t, x1, g_norm_ffn[l].reshape(1, D), mod3, u3, S,
                              ck * steps, steps)
                pending.append(_sc_peer_v(v3, idx_flat, wgt, ck * steps * _PEER_TB))
            if ck >= 2:
                out = _peer_fin(x1, pending[ck - 2], mod3, g_final.reshape(1, D), S,
                                (ck - 2) * steps * _PEER_TB, out)
        xt = out
    return xt.reshape(B, S, D)
```

```python
import functools

import jax
import jax.numpy as jnp
from jax import lax
from jax.experimental import pallas as pl
from jax.experimental.pallas import tpu as pltpu
from jax.experimental.pallas import tpu_sc as plsc

F32 = jnp.float32
BF16 = jnp.bfloat16
HIGHEST = lax.Precision.HIGHEST

NORM_EPS = 1e-6
GLA_HEADS = 4
GLA_DK = 128
GLA_DV = 256
GLA_GATE_RANK = 16
GLA_TAU = 16.0
GLA_CHUNK = 64
DIL_HD = 128
DIL_HEADS = 8
DIL_PATTERNS = ((128, 1), (512, 4), (2048, 16))
ROPE_THETA = 10000.0
NEG_INF = -1e30
PEER_HEADS = 8
PEER_NKEYS = 128
PEER_TOPK = 16
PEER_HALF = 128

LANE = 128
MIB = 1024 * 1024

_COL_GQ, _COL_GK, _COL_GV, _COL_GR, _COL_DQ, _COL_DK, _COL_DV = 0, 4, 8, 16, 24, 32, 40
_PROJ_W = 48 * LANE

_NT = (((1,), (1,)), ((), ()))
_TN = (((0,), (0,)), ((), ()))


def _params(sem, vmem_mib):
    return pltpu.CompilerParams(dimension_semantics=sem, vmem_limit_bytes=vmem_mib * MIB)


def _rms(x, g):
    return x * lax.rsqrt(jnp.mean(x * x, axis=-1, keepdims=True) + NORM_EPS) * g


def _silu(x):
    return x / (1.0 + jnp.exp(-x))


def _ada_kernel(c_ref, w_ref, b_ref, o_ref):
    s = _silu(c_ref[...]).astype(BF16)
    o_ref[...] = jnp.dot(s, w_ref[...].astype(BF16), preferred_element_type=F32) + b_ref[...]


def _ada(c, w, b):
    B, D = c.shape
    N = w.shape[1]
    tn = 1024
    cp = jnp.zeros((8, D), F32).at[:B].set(c)
    out = pl.pallas_call(
        _ada_kernel,
        grid=(N // tn,),
        in_specs=[pl.BlockSpec((8, D), lambda j: (0, 0)),
                  pl.BlockSpec((D, tn), lambda j: (0, j)),
                  pl.BlockSpec((1, tn), lambda j: (0, j))],
        out_specs=pl.BlockSpec((8, tn), lambda j: (0, j)),
        out_shape=jax.ShapeDtypeStruct((8, N), F32),
        compiler_params=_params(("parallel",), 40),
        name="ada",
    )(cp, w, b.reshape(1, N))
    return out[:B]


def _inproj_kernel(x_ref, g_ref, sc_ref, sh_ref, w_ref, wz_ref, o_ref, z_ref, h_scr):
    @pl.when(pl.program_id(1) == 0)
    def _():
        h = _rms(x_ref[...], g_ref[...]) * (1.0 + sc_ref[0]) + sh_ref[0]
        hb = h.astype(BF16)
        h_scr[...] = hb
        z_ref[...] = jnp.dot(hb, wz_ref[...], preferred_element_type=F32)

    o_ref[...] = jnp.dot(h_scr[...], w_ref[...], preferred_element_type=F32)


def _inproj(x2, g, mod3, w_main, w_z, S):
    T, D = x2.shape
    tm, tn = 1024, 768
    per_b = S // tm
    return pl.pallas_call(
        _inproj_kernel,
        grid=(T // tm, _PROJ_W // tn),
        in_specs=[pl.BlockSpec((tm, D), lambda i, j: (i, 0)),
                  pl.BlockSpec((1, D), lambda i, j: (0, 0)),
                  pl.BlockSpec((1, 1, D), lambda i, j: ((i // per_b) * 6 + 1, 0, 0)),
                  pl.BlockSpec((1, 1, D), lambda i, j: ((i // per_b) * 6 + 0, 0, 0)),
                  pl.BlockSpec((D, tn), lambda i, j: (0, j)),
                  pl.BlockSpec((D, LANE), lambda i, j: (0, 0))],
        out_specs=[pl.BlockSpec((tm, tn), lambda i, j: (i, j)),
                   pl.BlockSpec((tm, LANE), lambda i, j: (i, 0))],
        out_shape=[jax.ShapeDtypeStruct((T, _PROJ_W), F32),
                   jax.ShapeDtypeStruct((T, LANE), F32)],
        scratch_shapes=[pltpu.VMEM((tm, D), BF16)],
        compiler_params=_params(("parallel", "arbitrary"), 48),
        name="inproj",
    )(x2, g, mod3, mod3, w_main, w_z)


def _rope_kernel(pos_ref, f_ref, sg_ref, cs_ref, sn_ref):
    ang = pos_ref[...].astype(F32) * f_ref[...]
    cs_ref[...] = jnp.cos(ang)
    sn_ref[...] = jnp.sin(ang) * sg_ref[...]


def _rope_tables(pos_col):
    T = pos_col.shape[0]
    half = DIL_HD // 2
    inv = jnp.power(ROPE_THETA, -jnp.arange(half, dtype=F32) * 2.0 / DIL_HD)
    freq = jnp.concatenate([inv, inv]).reshape(1, DIL_HD)
    sign = jnp.concatenate([-jnp.ones((half,), F32), jnp.ones((half,), F32)]).reshape(1, DIL_HD)
    tm = 1024
    return pl.pallas_call(
        _rope_kernel,
        grid=(T // tm,),
        in_specs=[pl.BlockSpec((tm, 1), lambda i: (i, 0)),
                  pl.BlockSpec((1, DIL_HD), lambda i: (0, 0)),
                  pl.BlockSpec((1, DIL_HD), lambda i: (0, 0))],
        out_specs=[pl.BlockSpec((tm, DIL_HD), lambda i: (i, 0)),
                   pl.BlockSpec((tm, DIL_HD), lambda i: (i, 0))],
        out_shape=[jax.ShapeDtypeStruct((T, DIL_HD), F32)] * 2,
        compiler_params=_params(("parallel",), 32),
        name="rope",
    )(pos_col, freq, sign)


def _gla_kernel(q_ref, k_ref, v_ref, r_ref, z_ref, wgf_ref, bgf_ref, wgb_ref, bgb_ref, g_ref,
                o_ref, laf_scr, lab_scr, of_scr):
    S = q_ref.shape[0]
    C = GLA_CHUNK
    n = S // C
    scale = GLA_DK ** -0.5
    row = lax.broadcasted_iota(jnp.int32, (C, C), 0)
    col = lax.broadcasted_iota(jnp.int32, (C, C), 1)

    def log_gate(w_ref, b_ref):
        zz = jnp.dot(z_ref[...], w_ref[0], precision=HIGHEST, preferred_element_type=F32) + b_ref[0]
        return (jnp.minimum(zz, 0.0) - jnp.log(1.0 + jnp.exp(-jnp.abs(zz)))) * (1.0 / GLA_TAU)

    def chunk(fwd, c, st_t, la_scr, o_scr):
        keep = (col <= row) if fwd else (col >= row)
        sl = pl.ds(pl.multiple_of(c * C, C), C)
        cum = la_scr[sl, :]
        tot = cum[C - 1:C, :] if fwd else cum[0:1, :]
        kk = k_ref[sl, :]
        qd = (q_ref[sl, :] * scale * jnp.exp(cum)).astype(BF16)
        ki = (kk * jnp.exp(-cum)).astype(BF16)
        kte = (kk * jnp.exp(tot - cum)).astype(BF16)
        vb = v_ref[sl, :].astype(BF16)
        attn = lax.dot_general(qd, ki, _NT, preferred_element_type=F32)
        attn = jnp.where(keep, attn, 0.0).astype(BF16)
        o = jnp.dot(attn, vb, preferred_element_type=F32)
        o_scr[sl, :] = o + lax.dot_general(qd, st_t.astype(BF16), _NT, preferred_element_type=F32)
        upd = lax.dot_general(vb, kte, _TN, preferred_element_type=F32)
        return st_t * jnp.exp(tot) + upd

    laf_scr[...] = log_gate(wgf_ref, bgf_ref)
    lab_scr[...] = log_gate(wgb_ref, bgb_ref)

    def cumulate(c, carry):
        sl = pl.ds(pl.multiple_of(c * C, C), C)
        laf_scr[sl, :] = jnp.dot((col <= row).astype(F32), laf_scr[sl, :], precision=HIGHEST,
                                 preferred_element_type=F32)
        lab_scr[sl, :] = jnp.dot((col >= row).astype(F32), lab_scr[sl, :], precision=HIGHEST,
                                 preferred_element_type=F32)
        return carry

    lax.fori_loop(0, n, cumulate, 0, unroll=4)

    def body(i, states):
        return (chunk(True, i, states[0], laf_scr, of_scr),
                chunk(False, n - 1 - i, states[1], lab_scr, o_ref))

    zero = jnp.zeros((GLA_DV, GLA_DK), F32)
    lax.fori_loop(0, n, body, (zero, zero), unroll=8)
    y = _rms(of_scr[...] + o_ref[...], g_ref[0])
    o_ref[...] = y * _silu(r_ref[...])


def _gla(proj, gz, wgf, bgf, wgb, bgb, g_out, B, S):
    T = proj.shape[0]
    H = GLA_HEADS
    return pl.pallas_call(
        _gla_kernel,
        grid=(B, H),
        in_specs=[pl.BlockSpec((S, GLA_DK), lambda b, h: (b, _COL_GQ + h)),
                  pl.BlockSpec((S, GLA_DK), lambda b, h: (b, _COL_GK + h)),
                  pl.BlockSpec((S, GLA_DV), lambda b, h: (b, _COL_GV // 2 + h)),
                  pl.BlockSpec((S, GLA_DV), lambda b, h: (b, _COL_GR // 2 + h)),
                  pl.BlockSpec((S, LANE), lambda b, h: (b, 0)),
                  pl.BlockSpec((1, LANE, GLA_DK), lambda b, h: (h, 0, 0)),
                  pl.BlockSpec((1, 1, GLA_DK), lambda b, h: (h, 0, 0)),
                  pl.BlockSpec((1, LANE, GLA_DK), lambda b, h: (h, 0, 0)),
                  pl.BlockSpec((1, 1, GLA_DK), lambda b, h: (h, 0, 0)),
                  pl.BlockSpec((1, 1, GLA_DV), lambda b, h: (h, 0, 0))],
        out_specs=pl.BlockSpec((S, GLA_DV), lambda b, h: (b, h)),
        out_shape=jax.ShapeDtypeStruct((T, GLA_HEADS * GLA_DV), F32),
        scratch_shapes=[pltpu.VMEM((S, GLA_DK), F32), pltpu.VMEM((S, GLA_DK), F32),
                        pltpu.VMEM((S, GLA_DV), F32)],
        compiler_params=_params(("parallel", "parallel"), 56),
        name="gla",
    )(proj, proj, proj, proj, gz, wgf, bgf, wgb, bgb, g_out)


_DIL_QB = 128
_DIL_HALF = 64
_DIL_KW = _DIL_QB + 2 * _DIL_HALF


def _dil_kernel(q_ref, k_ref, v_ref, cs_ref, sn_ref, o_ref, qf_scr, kf_scr, qb_scr, kb_scr, vb_scr, lse_scr):
    S = q_ref.shape[0]
    QB, KW, HALF = _DIL_QB, _DIL_KW, _DIL_HALF
    rot = DIL_HD // 2
    cs, sn = cs_ref[...], sn_ref[...]
    q = q_ref[...]
    qf_scr[...] = (q * cs + pltpu.roll(q, rot, 1) * sn) * (DIL_HD ** -0.5)
    k = k_ref[...]
    kf_scr[...] = k * cs + pltpu.roll(k, rot, 1) * sn
    rel = (lax.broadcasted_iota(jnp.int32, (QB, KW), 1) - lax.broadcasted_iota(jnp.int32, (QB, KW), 0))

    for pi, (window, d) in enumerate(DIL_PATTERNS):
        assert window // (2 * d) == HALF
        M = S // d
        per = M // QB
        for r in range(d):
            rows = pl.ds(r * M, M)
            src = pl.ds(r, M, stride=d) if d > 1 else pl.ds(0, M)
            qb_scr[rows, :] = qf_scr[src, :].astype(BF16)
            kb_scr[rows, :] = kf_scr[src, :].astype(BF16)
            vb_scr[rows, :] = v_ref[src, :].astype(BF16)

        for r in range(d):
            def block(mb, carry, r=r, first=(pi == 0)):
                m0 = mb * QB
                start = jnp.clip(m0 - HALF, 0, M - KW)
                qrow = pl.multiple_of(r * M + m0, QB)
                krow = pl.multiple_of(r * M + start, HALF)
                s = lax.dot_general(qb_scr[pl.ds(qrow, QB), :], kb_scr[pl.ds(krow, KW), :], _NT,
                                    preferred_element_type=F32)
                s = jnp.where(jnp.abs(rel + (start - m0)) <= HALF, s, NEG_INF)
                m = jnp.max(s, axis=-1, keepdims=True)
                p = jnp.exp(s - m)
                l = jnp.sum(p, axis=-1, keepdims=True)
                o = jnp.dot(p.astype(BF16), vb_scr[pl.ds(krow, KW), :], preferred_element_type=F32) / l
                ls = jnp.broadcast_to(m + jnp.log(l), (QB, DIL_HD))
                base = pl.multiple_of(m0 * d, QB)
                nat = pl.ds(base + r, QB, stride=d) if d > 1 else pl.ds(base, QB)
                if first:
                    o_ref[nat, :] = o
                    lse_scr[nat, :] = ls
                else:
                    o0, l0 = o_ref[nat, :], lse_scr[nat, :]
                    mx = jnp.maximum(l0, ls)
                    w0, w1 = jnp.exp(l0 - mx), jnp.exp(ls - mx)
                    den = w0 + w1
                    o_ref[nat, :] = (w0 * o0 + w1 * o) / den
                    lse_scr[nat, :] = mx + jnp.log(den)
                return carry

            lax.fori_loop(0, per, block, 0, unroll=min(per, 16))


def _dil(proj, cs, sn, B, S):
    T = proj.shape[0]
    blk = lambda col: pl.BlockSpec((S, DIL_HD), lambda b, h: (b, col + h))
    return pl.pallas_call(
        _dil_kernel,
        grid=(B, DIL_HEADS),
        in_specs=[blk(_COL_DQ), blk(_COL_DK), blk(_COL_DV),
                  pl.BlockSpec((S, DIL_HD), lambda b, h: (b, 0)),
                  pl.BlockSpec((S, DIL_HD), lambda b, h: (b, 0))],
        out_specs=pl.BlockSpec((S, DIL_HD), lambda b, h: (b, h)),
        out_shape=jax.ShapeDtypeStruct((T, DIL_HEADS * DIL_HD), F32),
        scratch_shapes=[pltpu.VMEM((S, DIL_HD), F32), pltpu.VMEM((S, DIL_HD), F32),
                        pltpu.VMEM((S, DIL_HD), BF16), pltpu.VMEM((S, DIL_HD), BF16),
                        pltpu.VMEM((S, DIL_HD), BF16), pltpu.VMEM((S, DIL_HD), F32)],
        compiler_params=_params(("parallel", "parallel"), 48),
        name="dil",
    )(proj, proj, proj, cs, sn)


def _outproj_kernel(og_ref, od_ref, w_ref, x_ref, ga_ref, o_ref):
    kg = og_ref.shape[1]
    mixed = jnp.dot(og_ref[...].astype(BF16), w_ref[:kg, :], preferred_element_type=F32)
    mixed = mixed + jnp.dot(od_ref[...].astype(BF16), w_ref[kg:, :], preferred_element_type=F32)
    o_ref[...] = x_ref[...] + ga_ref[0] * mixed


def _outproj(o_gla, o_dil, w_out, x2, mod3, S):
    T, D = x2.shape
    tm = 512
    per_b = S // tm
    kg, kd = o_gla.shape[1], o_dil.shape[1]
    return pl.pallas_call(
        _outproj_kernel,
        grid=(T // tm,),
        in_specs=[pl.BlockSpec((tm, kg), lambda i: (i, 0)),
                  pl.BlockSpec((tm, kd), lambda i: (i, 0)),
                  pl.BlockSpec((kg + kd, D), lambda i: (0, 0)),
                  pl.BlockSpec((tm, D), lambda i: (i, 0)),
                  pl.BlockSpec((1, 1, D), lambda i: ((i // per_b) * 6 + 2, 0, 0))],
        out_specs=pl.BlockSpec((tm, D), lambda i: (i, 0)),
        out_shape=jax.ShapeDtypeStruct((T, D), F32),
        compiler_params=_params(("parallel",), 48),
        name="outproj",
    )(o_gla, o_dil, w_out, x2, mod3)


def _pq_kernel(x_ref, g_ref, sc_ref, sh_ref, w_ref, o_ref):
    h = _rms(x_ref[...], g_ref[...]) * (1.0 + sc_ref[0]) + sh_ref[0]
    o_ref[...] = jnp.dot(h.astype(BF16), w_ref[...], preferred_element_type=F32)


def _pq(x1, g, mod3, wq, S):
    T, D = x1.shape
    N = wq.shape[1]
    tm = 512
    per_b = S // tm
    return pl.pallas_call(
        _pq_kernel,
        grid=(T // tm,),
        in_specs=[pl.BlockSpec((tm, D), lambda i: (i, 0)),
                  pl.BlockSpec((1, D), lambda i: (0, 0)),
                  pl.BlockSpec((1, 1, D), lambda i: ((i // per_b) * 6 + 4, 0, 0)),
                  pl.BlockSpec((1, 1, D), lambda i: ((i // per_b) * 6 + 3, 0, 0)),
                  pl.BlockSpec((D, N), lambda i: (0, 0))],
        out_specs=pl.BlockSpec((tm, N), lambda i: (i, 0)),
        out_shape=jax.ShapeDtypeStruct((T, N), F32),
        compiler_params=_params(("parallel",), 48),
        name="pq",
    )(x1, g, mod3, mod3, wq)


def _top_rows(s, k, payload=None):
    n_rows = s.shape[0]
    rid = lax.broadcasted_iota(jnp.int32, s.shape, 0).astype(F32)
    vals, picks = [], []
    for _ in range(k):
        m = jnp.max(s, axis=0, keepdims=True)
        pos = jnp.min(jnp.where(s == m, rid, float(n_rows)), axis=0, keepdims=True)
        hit = rid == pos
        vals.append(m)
        if payload is None:
            picks.append(pos)
        else:
            picks.append(jnp.sum(jnp.where(hit, payload, 0), axis=0, keepdims=True))
        s = jnp.where(hit, -jnp.inf, s)
    picks = jnp.concatenate(picks, axis=0)
    return jnp.concatenate(vals, axis=0), picks.astype(jnp.int32)


def _staircase(a, b, combine, fill):
    K = a.shape[0]
    half = K // 2
    jrow = lax.broadcasted_iota(jnp.int32, (half, a.shape[1]), 0)
    pieces = [combine(a[0:1], b)]
    for i in range(1, half):
        piece = combine(a[i:i + 1], b[0:half])
        width = K // (i + 1)
        pieces.append(piece if width >= half else jnp.where(jrow < width, piece, fill))
    pieces.append(combine(a[half:K], b[0:1]))
    return jnp.concatenate(pieces, axis=0)


def _topk_kernel(q_ref, keys_ref, idx_ref, gate_ref):
    K = PEER_TOPK
    for h in range(PEER_HEADS):
        tops = []
        for half in range(2):
            c0 = (h * 2 + half) * PEER_HALF
            qh = q_ref[:, c0:c0 + PEER_HALF].astype(BF16)
            sc = lax.dot_general(keys_ref[h, half], qh, _NT, preferred_element_type=F32)
            tops.append(_top_rows(sc, K))
        (s0, i0), (s1, i1) = tops
        cand_s = _staircase(s0, s1, lambda a, b: a + b, -jnp.inf)
        cand_i = _staircase(i0, i1, lambda a, b: a * PEER_NKEYS + b, 0)
        best, idx = _top_rows(cand_s, K, payload=cand_i)
        e = jnp.exp(best - best[0:1])
        gate = e / jnp.sum(e, axis=0, keepdims=True)
        idx_ref[h * K:(h + 1) * K, :] = idx
        gate_ref[h * K:(h + 1) * K, :] = gate


def _topk(qp, keys_bf):
    T, N = qp.shape
    tt = 256
    HK = PEER_HEADS * PEER_TOPK
    return pl.pallas_call(
        _topk_kernel,
        grid=(T // tt,),
        in_specs=[pl.BlockSpec((tt, N), lambda i: (i, 0)),
                  pl.BlockSpec(keys_bf.shape, lambda i: (0, 0, 0, 0))],
        out_specs=[pl.BlockSpec((HK, tt), lambda i: (0, i)),
                   pl.BlockSpec((HK, tt), lambda i: (0, i))],
        out_shape=[jax.ShapeDtypeStruct((HK, T), jnp.int32),
                   jax.ShapeDtypeStruct((HK, T), F32)],
        compiler_params=_params(("parallel",), 32),
        name="topk",
    )(qp, keys_bf)


_PEER_TB = 128
_PEER_SUB = 8
_SUBLANES = 8
_PEER_CHUNKS = 32
_SC_LANES = 16
_SC_ROWS = 16


def _peer_u_kernel(idx_hbm, gate_ref, x1_ref, gn_ref, sc_ref, sh_ref, u_hbm, w_ref,
                   idx_smem, ub0, ub1, h_scr, sem_i, sem_u, *, step0):
    HK = PEER_HEADS * PEER_TOPK
    TB, SUB = _PEER_TB, _PEER_SUB
    R = SUB * HK
    N = TB * HK
    nsub = TB // SUB
    D = x1_ref.shape[1]
    nchunk = D // LANE
    tiles = HK // _SUBLANES
    i = pl.program_id(0)
    n = pl.num_programs(0)
    cur = lax.rem(i, 2) * N
    nxt = N - cur
    more = i + 1 < n
    ubufs = (ub0, ub1)

    def idx_copy(step, base):
        return pltpu.make_async_copy(idx_hbm.at[pl.ds((step0 + step) * N, N)],
                                     idx_smem.at[pl.ds(base, N)], sem_i)

    def issue_token(base, t, slot):
        for k in range(HK):
            e = idx_smem[base + t * HK + k]
            rt, s = t * tiles + k // _SUBLANES, k % _SUBLANES
            pltpu.make_async_copy(u_hbm.at[e], ubufs[slot].at[rt, :, s, :],
                                  sem_u.at[slot]).start(priority=k % 2)

    def wait(slot):
        pltpu.make_async_copy(ubufs[slot], ubufs[slot], sem_u.at[slot]).wait()

    @pl.when(i == 0)
    def _():
        first = idx_copy(0, 0)
        first.start()
        first.wait()
        for t in range(SUB):
            issue_token(0, t, 0)

    @pl.when(more)
    def _():
        idx_copy(i + 1, nxt).start()

    h_scr[...] = _rms(x1_ref[...], gn_ref[...]) * (1.0 + sc_ref[0]) + sh_ref[0]
    lane = lax.broadcasted_iota(jnp.int32, (HK, TB), 1)

    def compute_token(j, t, slot, wacc):
        ub = ubufs[slot]
        tok = j * SUB + t
        xt = h_scr[pl.ds(tok, 1), :]
        rows = slice(t * tiles, (t + 1) * tiles)
        part = ub[rows, 0].reshape(HK, LANE) * xt[:, 0:LANE]
        for c in range(1, nchunk):
            part = part + ub[rows, c].reshape(HK, LANE) * xt[:, c * LANE:(c + 1) * LANE]
        a = jnp.sum(part, axis=1, keepdims=True)
        hit = lane == tok
        g = jnp.sum(jnp.where(hit, gate_ref[...], 0.0), axis=1, keepdims=True)
        wgt = g * (0.5 * a * (1.0 + lax.erf(a * (2.0 ** -0.5))))
        return jnp.where(hit, wgt, wacc)

    def half(j, slot, next_base, wacc):
        wait(slot)
        for t in range(SUB):
            issue_token(next_base, t, 1 - slot)
            wacc = compute_token(j, t, slot, wacc)
        return wacc

    def pair(jj, wacc):
        j0 = 2 * jj
        wacc = half(j0, 0, cur + (j0 + 1) * R, wacc)
        last = jj == nsub // 2 - 1

        @pl.when(jnp.logical_and(last, more))
        def _():
            idx_copy(i + 1, nxt).wait()

        after = jnp.where(more, nxt, cur)
        return half(j0 + 1, 1, jnp.where(last, after, cur + (j0 + 2) * R), wacc)

    wacc = lax.fori_loop(0, nsub // 2, pair, jnp.zeros((HK, TB), F32))

    @pl.when(jnp.logical_not(more))
    def _():
        wait(0)

    w_ref[:, :HK] = jnp.zeros((TB, HK), F32)
    w_ref[:, HK:] = wacc.T


def _peer_u(idx_flat, gate_t, x1, g_norm, mod3, u3, S, step0, nsteps):
    T, D = x1.shape
    HK = PEER_HEADS * PEER_TOPK
    TB, SUB = _PEER_TB, _PEER_SUB
    per_b = S // TB
    modrow = lambda k: (lambda i: (((step0 + i) // per_b) * 6 + k, 0, 0))
    gbuf = pltpu.VMEM((SUB * HK // _SUBLANES, D // LANE, _SUBLANES, LANE), F32)
    return pl.pallas_call(
        functools.partial(_peer_u_kernel, step0=step0),
        grid=(nsteps,),
        in_specs=[pl.BlockSpec(memory_space=pl.ANY),
                  pl.BlockSpec((HK, TB), lambda i: (0, step0 + i)),
                  pl.BlockSpec((TB, D), lambda i: (step0 + i, 0)),
                  pl.BlockSpec((1, D), lambda i: (0, 0)),
                  pl.BlockSpec((1, 1, D), modrow(4)),
                  pl.BlockSpec((1, 1, D), modrow(3)),
                  pl.BlockSpec(memory_space=pl.ANY)],
        out_specs=pl.BlockSpec((TB, 2 * HK), lambda i: (i, 0)),
        out_shape=jax.ShapeDtypeStruct((nsteps * TB, 2 * HK), F32),
        scratch_shapes=[pltpu.SMEM((2 * TB * HK,), jnp.int32),
                        gbuf, gbuf,
                        pltpu.VMEM((TB, D), F32),
                        pltpu.SemaphoreType.DMA,
                        pltpu.SemaphoreType.DMA((2,))],
        compiler_params=_params(("arbitrary",), 40),
        name="peer_u",
    )(idx_flat, gate_t, x1, g_norm, mod3, mod3, u3)


def _sc_peer_v(v_tab, idx_flat, wgt, tok_base):
    E, nblk, _ = v_tab.shape
    D = nblk * LANE
    Tc, HK = wgt.shape[0], wgt.shape[1] // 2
    info = plsc.get_sparse_core_info()
    nw = info.num_cores * info.num_subcores
    tpw = Tc // nw
    CH = _SC_ROWS
    nch = HK // CH
    nsl = D // _SC_LANES
    mesh = plsc.VectorSubcoreMesh(core_axis_name="c", subcore_axis_name="s")

    @functools.partial(
        pl.kernel, mesh=mesh, out_type=jax.ShapeDtypeStruct((Tc, D), F32),
        scratch_types=[pltpu.VMEM((tpw * HK,), jnp.int32), pltpu.VMEM((2 * HK,), F32),
                       pltpu.VMEM((D,), F32),
                       pltpu.VMEM((CH, nblk, LANE), F32), pltpu.VMEM((CH, nblk, LANE), F32),
                       pltpu.SemaphoreType.DMA, pltpu.SemaphoreType.DMA],
        compiler_params=pltpu.CompilerParams(needs_layout_passes=False),
        name="sc_peer_v",
    )
    def k(tab_hbm, idx_hbm, w_hbm, o_hbm, idx_v, w_v, o_v, buf0, buf1, g0, g1):
        wid = lax.axis_index("s") * info.num_cores + lax.axis_index("c")
        tok0 = wid * tpw
        pltpu.sync_copy(idx_hbm.at[pl.ds((tok_base + tok0) * HK, tpw * HK)], idx_v)
        bufs, gs = (buf0, buf1), (g0, g1)

        def gather(g, b):
            return pltpu.make_async_copy(tab_hbm.at[idx_v.at[pl.ds(g * CH, CH)]], bufs[b], gs[b])

        gather(0, 0).start()

        @pl.loop(0, tpw)
        def _(t):
            pltpu.sync_copy(w_hbm.at[tok0 + t], w_v)

            @pl.loop(0, nsl)
            def _(c):
                o_v[pl.ds(pl.multiple_of(c * _SC_LANES, _SC_LANES), _SC_LANES)] = jnp.zeros((_SC_LANES,), F32)

            for ch in range(nch):
                b = ch % 2
                g = t * nch + ch
                gather(g, b).wait()

                @pl.when(g + 1 < tpw * nch)
                def _():
                    gather(g + 1, 1 - b).start()

                ws = [plsc.load_gather(w_v, [jnp.full((_SC_LANES,), HK + ch * CH + r, jnp.int32)])
                      for r in range(CH)]

                @pl.loop(0, nsl, step=2)
                def _(c):
                    for half in range(2):
                        off = pl.multiple_of((c + half) * _SC_LANES, _SC_LANES)
                        blk = (c + half) // (LANE // _SC_LANES)
                        lo = pl.multiple_of(off - blk * LANE, _SC_LANES)
                        parts = [ws[r] * bufs[b][r, blk, pl.ds(lo, _SC_LANES)] for r in range(CH)]
                        while len(parts) > 1:
                            parts = [parts[p] + parts[p + 1] for p in range(0, len(parts), 2)]
                        o_v[pl.ds(off, _SC_LANES)] = o_v[pl.ds(off, _SC_LANES)] + parts[0]

            pltpu.sync_copy(o_v, o_hbm.at[tok0 + t])

    return k(v_tab, idx_flat, wgt)


def _peer_fin_kernel(x_ref, p_ref, ga_ref, gf_ref, *rest):
    o_ref = rest[-1]
    o_ref[...] = _rms(x_ref[...] + ga_ref[0] * p_ref[...], gf_ref[...])


def _peer_fin(x1, po, mod3, g_final, S, tok_base, out_prev):
    T, D = x1.shape
    Tc = po.shape[0]
    tm = min(Tc, 512)
    per_b = S // tm
    blk0 = tok_base // tm
    in_specs = [pl.BlockSpec((tm, D), lambda i: (blk0 + i, 0)),
                pl.BlockSpec((tm, D), lambda i: (i, 0)),
                pl.BlockSpec((1, 1, D), lambda i: (((blk0 + i) // per_b) * 6 + 5, 0, 0)),
                pl.BlockSpec((1, D), lambda i: (0, 0))]
    args = [x1, po, mod3, g_final]
    aliases = {}
    if out_prev is not None:
        in_specs.append(pl.BlockSpec(memory_space=pl.ANY))
        args.append(out_prev)
        aliases = {4: 0}
    return pl.pallas_call(
        _peer_fin_kernel,
        grid=(Tc // tm,),
        in_specs=in_specs,
        out_specs=pl.BlockSpec((tm, D), lambda i: (blk0 + i, 0)),
        out_shape=jax.ShapeDtypeStruct((T, D), F32),
        input_output_aliases=aliases,
        compiler_params=_params(("arbitrary",), 40),
        name="peer_fin",
    )(*args)


def _pad_gate(w, lo):
    rank = w.shape[0]
    wh = w.reshape(rank, GLA_HEADS, GLA_DK).transpose(1, 0, 2)
    return jnp.zeros((GLA_HEADS, LANE, GLA_DK), F32).at[:, lo:lo + rank, :].set(wh)


def kernel(x, c, positions, w_ada, b_ada, g_norm_mix, w_in, w_gate_f, b_gate_f, w_gate_b, b_gate_b,
           g_gla_out, w_out, g_norm_ffn, w_peer_q, peer_sub_keys, peer_u, peer_v, g_final):
    B, S, D = x.shape
    T = B * S
    depth = w_ada.shape[0]
    assert depth == 1, "the final norm is fused into the last PEER call; one layer only"
    xt = x.reshape(T, D)
    cs, sn = _rope_tables(positions.reshape(T, 1))
    gz0 = 2 * GLA_HEADS * GLA_DK + 2 * GLA_HEADS * GLA_DV
    gz1 = gz0 + 2 * GLA_GATE_RANK
    for l in range(depth):
        mod3 = _ada(c, w_ada[l], b_ada[l]).reshape(B * 6, 1, D)
        w_main = jnp.concatenate([w_in[l][:, :gz0], w_in[l][:, gz1:]], axis=1).astype(BF16)
        w_z = jnp.pad(w_in[l][:, gz0:gz1], ((0, 0), (0, LANE - (gz1 - gz0)))).astype(BF16)
        proj, gz = _inproj(xt, g_norm_mix[l].reshape(1, D), mod3, w_main, w_z, S)
        o_gla = _gla(proj, gz,
                     _pad_gate(w_gate_f[l], 0), b_gate_f[l].reshape(GLA_HEADS, 1, GLA_DK),
                     _pad_gate(w_gate_b[l], GLA_GATE_RANK), b_gate_b[l].reshape(GLA_HEADS, 1, GLA_DK),
                     g_gla_out[l].reshape(GLA_HEADS, 1, GLA_DV), B, S)
        o_dil = _dil(proj, cs, sn, B, S)
        x1 = _outproj(o_gla, o_dil, w_out[l].astype(BF16), xt, mod3, S)
        qp = _pq(x1, g_norm_ffn[l].reshape(1, D), mod3, w_peer_q[l].astype(BF16), S)
        idx_t, gate_t = _topk(qp, peer_sub_keys[l].astype(BF16))
        idx_flat = idx_t.T.reshape(-1)
        E = peer_u.shape[1]
        u3 = peer_u[l].reshape(E, D // LANE, LANE)
        v3 = peer_v[l].reshape(E, D // LANE, LANE)
        steps = T // _PEER_TB // _PEER_CHUNKS
        out, pending = None, []
        for ck in range(_PEER_CHUNKS + 2):
            if ck < _PEER_CHUNKS:
                wgt = _peer_u(idx_flat, gate_t, x1, g_norm_ffn[l].reshape(1, D), mod3, u3, S,
                              ck * steps, steps)
                pending.append(_sc_peer_v(v3, idx_flat, wgt, ck * steps * _PEER_TB))
            if ck >= 2:
                out = _peer_fin(x1, pending[ck - 2], mod3, g_final.reshape(1, D), S,
                                (ck - 2) * steps * _PEER_TB, out)
        xt = out
    return xt.reshape(B, S, D)
```

```python
import functools

import jax
import jax.numpy as jnp
from jax import lax
from jax.experimental import pallas as pl
from jax.experimental.pallas import tpu as pltpu
from jax.experimental.pallas import tpu_sc as plsc

F32 = jnp.float32
BF16 = jnp.bfloat16
HIGHEST = lax.Precision.HIGHEST

NORM_EPS = 1e-6
GLA_HEADS = 4
GLA_DK = 128
GLA_DV = 256
GLA_GATE_RANK = 16
GLA_TAU = 16.0
GLA_CHUNK = 64
DIL_HD = 128
DIL_HEADS = 8
DIL_PATTERNS = ((128, 1), (512, 4), (2048, 16))
ROPE_THETA = 10000.0
NEG_INF = -1e30
PEER_HEADS = 8
PEER_NKEYS = 128
PEER_TOPK = 16
PEER_HALF = 128

LANE = 128
MIB = 1024 * 1024

_COL_GQ, _COL_GK, _COL_GV, _COL_GR, _COL_DQ, _COL_DK, _COL_DV = 0, 4, 8, 16, 24, 32, 40
_PROJ_W = 48 * LANE

_NT = (((1,), (1,)), ((), ()))
_TN = (((0,), (0,)), ((), ()))


def _params(sem, vmem_mib):
    return pltpu.CompilerParams(dimension_semantics=sem, vmem_limit_bytes=vmem_mib * MIB)


def _rms(x, g):
    return x * lax.rsqrt(jnp.mean(x * x, axis=-1, keepdims=True) + NORM_EPS) * g


def _silu(x):
    return x / (1.0 + jnp.exp(-x))


def _ada_kernel(c_ref, w_ref, b_ref, o_ref):
    s = _silu(c_ref[...]).astype(BF16)
    o_ref[...] = jnp.dot(s, w_ref[...].astype(BF16), preferred_element_type=F32) + b_ref[...]


def _ada(c, w, b):
    B, D = c.shape
    N = w.shape[1]
    tn = 1024
    cp = jnp.zeros((8, D), F32).at[:B].set(c)
    out = pl.pallas_call(
        _ada_kernel,
        grid=(N // tn,),
        in_specs=[pl.BlockSpec((8, D), lambda j: (0, 0)),
                  pl.BlockSpec((D, tn), lambda j: (0, j)),
                  pl.BlockSpec((1, tn), lambda j: (0, j))],
        out_specs=pl.BlockSpec((8, tn), lambda j: (0, j)),
        out_shape=jax.ShapeDtypeStruct((8, N), F32),
        compiler_params=_params(("parallel",), 40),
        name="ada",
    )(cp, w, b.reshape(1, N))
    return out[:B]


def _inproj_kernel(x_ref, g_ref, sc_ref, sh_ref, w_ref, wz_ref, o_ref, z_ref, h_scr):
    @pl.when(pl.program_id(1) == 0)
    def _():
        h = _rms(x_ref[...], g_ref[...]) * (1.0 + sc_ref[0]) + sh_ref[0]
        hb = h.astype(BF16)
        h_scr[...] = hb
        z_ref[...] = jnp.dot(hb, wz_ref[...], preferred_element_type=F32)

    o_ref[...] = jnp.dot(h_scr[...], w_ref[...], preferred_element_type=F32)


def _inproj(x2, g, mod3, w_main, w_z, S):
    T, D = x2.shape
    tm, tn = 1024, 768
    per_b = S // tm
    return pl.pallas_call(
        _inproj_kernel,
        grid=(T // tm, _PROJ_W // tn),
        in_specs=[pl.BlockSpec((tm, D), lambda i, j: (i, 0)),
                  pl.BlockSpec((1, D), lambda i, j: (0, 0)),
                  pl.BlockSpec((1, 1, D), lambda i, j: ((i // per_b) * 6 + 1, 0, 0)),
                  pl.BlockSpec((1, 1, D), lambda i, j: ((i // per_b) * 6 + 0, 0, 0)),
                  pl.BlockSpec((D, tn), lambda i, j: (0, j)),
                  pl.BlockSpec((D, LANE), lambda i, j: (0, 0))],
        out_specs=[pl.BlockSpec((tm, tn), lambda i, j: (i, j)),
                   pl.BlockSpec((tm, LANE), lambda i, j: (i, 0))],
        out_shape=[jax.ShapeDtypeStruct((T, _PROJ_W), F32),
                   jax.ShapeDtypeStruct((T, LANE), F32)],
        scratch_shapes=[pltpu.VMEM((tm, D), BF16)],
        compiler_params=_params(("parallel", "arbitrary"), 48),
        name="inproj",
    )(x2, g, mod3, mod3, w_main, w_z)


def _rope_kernel(pos_ref, f_ref, sg_ref, cs_ref, sn_ref):
    ang = pos_ref[...].astype(F32) * f_ref[...]
    cs_ref[...] = jnp.cos(ang)
    sn_ref[...] = jnp.sin(ang) * sg_ref[...]


def _rope_tables(pos_col):
    T = pos_col.shape[0]
    half = DIL_HD // 2
    inv = jnp.power(ROPE_THETA, -jnp.arange(half, dtype=F32) * 2.0 / DIL_HD)
    freq = jnp.concatenate([inv, inv]).reshape(1, DIL_HD)
    sign = jnp.concatenate([-jnp.ones((half,), F32), jnp.ones((half,), F32)]).reshape(1, DIL_HD)
    tm = 1024
    return pl.pallas_call(
        _rope_kernel,
        grid=(T // tm,),
        in_specs=[pl.BlockSpec((tm, 1), lambda i: (i, 0)),
                  pl.BlockSpec((1, DIL_HD), lambda i: (0, 0)),
                  pl.BlockSpec((1, DIL_HD), lambda i: (0, 0))],
        out_specs=[pl.BlockSpec((tm, DIL_HD), lambda i: (i, 0)),
                   pl.BlockSpec((tm, DIL_HD), lambda i: (i, 0))],
        out_shape=[jax.ShapeDtypeStruct((T, DIL_HD), F32)] * 2,
        compiler_params=_params(("parallel",), 32),
        name="rope",
    )(pos_col, freq, sign)


def _gla_kernel(q_ref, k_ref, v_ref, r_ref, z_ref, wgf_ref, bgf_ref, wgb_ref, bgb_ref, g_ref,
                o_ref, laf_scr, lab_scr, of_scr):
    S = q_ref.shape[0]
    C = GLA_CHUNK
    n = S // C
    scale = GLA_DK ** -0.5
    row = lax.broadcasted_iota(jnp.int32, (C, C), 0)
    col = lax.broadcasted_iota(jnp.int32, (C, C), 1)

    def log_gate(w_ref, b_ref):
        zz = jnp.dot(z_ref[...], w_ref[0], precision=HIGHEST, preferred_element_type=F32) + b_ref[0]
        return (jnp.minimum(zz, 0.0) - jnp.log(1.0 + jnp.exp(-jnp.abs(zz)))) * (1.0 / GLA_TAU)

    def chunk(fwd, c, st_t, la_scr, o_scr):
        keep = (col <= row) if fwd else (col >= row)
        sl = pl.ds(pl.multiple_of(c * C, C), C)
        cum = la_scr[sl, :]
        tot = cum[C - 1:C, :] if fwd else cum[0:1, :]
        kk = k_ref[sl, :]
        qd = (q_ref[sl, :] * scale * jnp.exp(cum)).astype(BF16)
        ki = (kk * jnp.exp(-cum)).astype(BF16)
        kte = (kk * jnp.exp(tot - cum)).astype(BF16)
        vb = v_ref[sl, :].astype(BF16)
        attn = lax.dot_general(qd, ki, _NT, preferred_element_type=F32)
        attn = jnp.where(keep, attn, 0.0).astype(BF16)
        o = jnp.dot(attn, vb, preferred_element_type=F32)
        o_scr[sl, :] = o + lax.dot_general(qd, st_t.astype(BF16), _NT, preferred_element_type=F32)
        upd = lax.dot_general(vb, kte, _TN, preferred_element_type=F32)
        return st_t * jnp.exp(tot) + upd

    laf_scr[...] = log_gate(wgf_ref, bgf_ref)
    lab_scr[...] = log_gate(wgb_ref, bgb_ref)

    def cumulate(c, carry):
        sl = pl.ds(pl.multiple_of(c * C, C), C)
        laf_scr[sl, :] = jnp.dot((col <= row).astype(F32), laf_scr[sl, :], precision=HIGHEST,
                                 preferred_element_type=F32)
        lab_scr[sl, :] = jnp.dot((col >= row).astype(F32), lab_scr[sl, :], precision=HIGHEST,
                                 preferred_element_type=F32)
        return carry

    lax.fori_loop(0, n, cumulate, 0, unroll=4)

    def body(i, states):
        return (chunk(True, i, states[0], laf_scr, of_scr),
                chunk(False, n - 1 - i, states[1], lab_scr, o_ref))

    zero = jnp.zeros((GLA_DV, GLA_DK), F32)
    lax.fori_loop(0, n, body, (zero, zero), unroll=8)
    y = _rms(of_scr[...] + o_ref[...], g_ref[0])
    o_ref[...] = y * _silu(r_ref[...])


def _gla(proj, gz, wgf, bgf, wgb, bgb, g_out, B, S):
    T = proj.shape[0]
    H = GLA_HEADS
    return pl.pallas_call(
        _gla_kernel,
        grid=(B, H),
        in_specs=[pl.BlockSpec((S, GLA_DK), lambda b, h: (b, _COL_GQ + h)),
                  pl.BlockSpec((S, GLA_DK), lambda b, h: (b, _COL_GK + h)),
                  pl.BlockSpec((S, GLA_DV), lambda b, h: (b, _COL_GV // 2 + h)),
                  pl.BlockSpec((S, GLA_DV), lambda b, h: (b, _COL_GR // 2 + h)),
                  pl.BlockSpec((S, LANE), lambda b, h: (b, 0)),
                  pl.BlockSpec((1, LANE, GLA_DK), lambda b, h: (h, 0, 0)),
                  pl.BlockSpec((1, 1, GLA_DK), lambda b, h: (h, 0, 0)),
                  pl.BlockSpec((1, LANE, GLA_DK), lambda b, h: (h, 0, 0)),
                  pl.BlockSpec((1, 1, GLA_DK), lambda b, h: (h, 0, 0)),
                  pl.BlockSpec((1, 1, GLA_DV), lambda b, h: (h, 0, 0))],
        out_specs=pl.BlockSpec((S, GLA_DV), lambda b, h: (b, h)),
        out_shape=jax.ShapeDtypeStruct((T, GLA_HEADS * GLA_DV), F32),
        scratch_shapes=[pltpu.VMEM((S, GLA_DK), F32), pltpu.VMEM((S, GLA_DK), F32),
                        pltpu.VMEM((S, GLA_DV), F32)],
        compiler_params=_params(("parallel", "parallel"), 56),
        name="gla",
    )(proj, proj, proj, proj, gz, wgf, bgf, wgb, bgb, g_out)


_DIL_QB = 128
_DIL_HALF = 64
_DIL_KW = _DIL_QB + 2 * _DIL_HALF


def _dil_kernel(q_ref, k_ref, v_ref, cs_ref, sn_ref, o_ref, qf_scr, kf_scr, qb_scr, kb_scr, vb_scr, lse_scr):
    S = q_ref.shape[0]
    QB, KW, HALF = _DIL_QB, _DIL_KW, _DIL_HALF
    rot = DIL_HD // 2
    cs, sn = cs_ref[...], sn_ref[...]
    q = q_ref[...]
    qf_scr[...] = (q * cs + pltpu.roll(q, rot, 1) * sn) * (DIL_HD ** -0.5)
    k = k_ref[...]
    kf_scr[...] = k * cs + pltpu.roll(k, rot, 1) * sn
    rel = (lax.broadcasted_iota(jnp.int32, (QB, KW), 1) - lax.broadcasted_iota(jnp.int32, (QB, KW), 0))

    for pi, (window, d) in enumerate(DIL_PATTERNS):
        assert window // (2 * d) == HALF
        M = S // d
        per = M // QB
        for r in range(d):
            rows = pl.ds(r * M, M)
            src = pl.ds(r, M, stride=d) if d > 1 else pl.ds(0, M)
            qb_scr[rows, :] = qf_scr[src, :].astype(BF16)
            kb_scr[rows, :] = kf_scr[src, :].astype(BF16)
            vb_scr[rows, :] = v_ref[src, :].astype(BF16)

        for r in range(d):
            def block(mb, carry, r=r, first=(pi == 0)):
                m0 = mb * QB
                start = jnp.clip(m0 - HALF, 0, M - KW)
                qrow = pl.multiple_of(r * M + m0, QB)
                krow = pl.multiple_of(r * M + start, HALF)
                s = lax.dot_general(qb_scr[pl.ds(qrow, QB), :], kb_scr[pl.ds(krow, KW), :], _NT,
                                    preferred_element_type=F32)
                s = jnp.where(jnp.abs(rel + (start - m0)) <= HALF, s, NEG_INF)
                m = jnp.max(s, axis=-1, keepdims=True)
                p = jnp.exp(s - m)
                l = jnp.sum(p, axis=-1, keepdims=True)
                o = jnp.dot(p.astype(BF16), vb_scr[pl.ds(krow, KW), :], preferred_element_type=F32) / l
                ls = jnp.broadcast_to(m + jnp.log(l), (QB, DIL_HD))
                base = pl.multiple_of(m0 * d, QB)
                nat = pl.ds(base + r, QB, stride=d) if d > 1 else pl.ds(base, QB)
                if first:
                    o_ref[nat, :] = o
                    lse_scr[nat, :] = ls
                else:
                    o0, l0 = o_ref[nat, :], lse_scr[nat, :]
                    mx = jnp.maximum(l0, ls)
                    w0, w1 = jnp.exp(l0 - mx), jnp.exp(ls - mx)
                    den = w0 + w1
                    o_ref[nat, :] = (w0 * o0 + w1 * o) / den
                    lse_scr[nat, :] = mx + jnp.log(den)
                return carry

            lax.fori_loop(0, per, block, 0, unroll=min(per, 16))


def _dil(proj, cs, sn, B, S):
    T = proj.shape[0]
    blk = lambda col: pl.BlockSpec((S, DIL_HD), lambda b, h: (b, col + h))
    return pl.pallas_call(
        _dil_kernel,
        grid=(B, DIL_HEADS),
        in_specs=[blk(_COL_DQ), blk(_COL_DK), blk(_COL_DV),
                  pl.BlockSpec((S, DIL_HD), lambda b, h: (b, 0)),
                  pl.BlockSpec((S, DIL_HD), lambda b, h: (b, 0))],
        out_specs=pl.BlockSpec((S, DIL_HD), lambda b, h: (b, h)),
        out_shape=jax.ShapeDtypeStruct((T, DIL_HEADS * DIL_HD), F32),
        scratch_shapes=[pltpu.VMEM((S, DIL_HD), F32), pltpu.VMEM((S, DIL_HD), F32),
                        pltpu.VMEM((S, DIL_HD), BF16), pltpu.VMEM((S, DIL_HD), BF16),
                        pltpu.VMEM((S, DIL_HD), BF16), pltpu.VMEM((S, DIL_HD), F32)],
        compiler_params=_params(("parallel", "parallel"), 48),
        name="dil",
    )(proj, proj, proj, cs, sn)


def _outproj_kernel(og_ref, od_ref, w_ref, x_ref, ga_ref, o_ref):
    kg = og_ref.shape[1]
    mixed = jnp.dot(og_ref[...].astype(BF16), w_ref[:kg, :], preferred_element_type=F32)
    mixed = mixed + jnp.dot(od_ref[...].astype(BF16), w_ref[kg:, :], preferred_element_type=F32)
    o_ref[...] = x_ref[...] + ga_ref[0] * mixed


def _outproj(o_gla, o_dil, w_out, x2, mod3, S):
    T, D = x2.shape
    tm = 512
    per_b = S // tm
    kg, kd = o_gla.shape[1], o_dil.shape[1]
    return pl.pallas_call(
        _outproj_kernel,
        grid=(T // tm,),
        in_specs=[pl.BlockSpec((tm, kg), lambda i: (i, 0)),
                  pl.BlockSpec((tm, kd), lambda i: (i, 0)),
                  pl.BlockSpec((kg + kd, D), lambda i: (0, 0)),
                  pl.BlockSpec((tm, D), lambda i: (i, 0)),
                  pl.BlockSpec((1, 1, D), lambda i: ((i // per_b) * 6 + 2, 0, 0))],
        out_specs=pl.BlockSpec((tm, D), lambda i: (i, 0)),
        out_shape=jax.ShapeDtypeStruct((T, D), F32),
        compiler_params=_params(("parallel",), 48),
        name="outproj",
    )(o_gla, o_dil, w_out, x2, mod3)


def _top_rows(s, k, payload=None):
    n_rows = s.shape[0]
    rid = lax.broadcasted_iota(jnp.int32, s.shape, 0).astype(F32)
    vals, picks = [], []
    for _ in range(k):
        m = jnp.max(s, axis=0, keepdims=True)
        pos = jnp.min(jnp.where(s == m, rid, float(n_rows)), axis=0, keepdims=True)
        hit = rid == pos
        vals.append(m)
        if payload is None:
            picks.append(pos)
        else:
            picks.append(jnp.sum(jnp.where(hit, payload, 0), axis=0, keepdims=True))
        s = jnp.where(hit, -jnp.inf, s)
    picks = jnp.concatenate(picks, axis=0)
    return jnp.concatenate(vals, axis=0), picks.astype(jnp.int32)


def _staircase(a, b, combine, fill):
    K = a.shape[0]
    half = K // 2
    jrow = lax.broadcasted_iota(jnp.int32, (half, a.shape[1]), 0)
    pieces = [combine(a[0:1], b)]
    for i in range(1, half):
        piece = combine(a[i:i + 1], b[0:half])
        width = K // (i + 1)
        pieces.append(piece if width >= half else jnp.where(jrow < width, piece, fill))
    pieces.append(combine(a[half:K], b[0:1]))
    return jnp.concatenate(pieces, axis=0)


def _topk_kernel(x_ref, g_ref, sc_ref, sh_ref, w_ref, keys_ref, idx_ref, gate_ref):
    K = PEER_TOPK
    hb = (_rms(x_ref[...], g_ref[...]) * (1.0 + sc_ref[0]) + sh_ref[0]).astype(BF16)
    for h in range(PEER_HEADS):
        qhead = jnp.dot(hb, w_ref[:, h * 2 * PEER_HALF:(h + 1) * 2 * PEER_HALF],
                        preferred_element_type=F32)
        tops = []
        for half in range(2):
            qh = qhead[:, half * PEER_HALF:(half + 1) * PEER_HALF].astype(BF16)
            sc = lax.dot_general(keys_ref[h, half], qh, _NT, preferred_element_type=F32)
            tops.append(_top_rows(sc, K))
        (s0, i0), (s1, i1) = tops
        cand_s = _staircase(s0, s1, lambda a, b: a + b, -jnp.inf)
        cand_i = _staircase(i0, i1, lambda a, b: a * PEER_NKEYS + b, 0)
        best, idx = _top_rows(cand_s, K, payload=cand_i)
        e = jnp.exp(best - best[0:1])
        gate = e / jnp.sum(e, axis=0, keepdims=True)
        idx_ref[h * K:(h + 1) * K, :] = idx
        gate_ref[h * K:(h + 1) * K, :] = gate


def _topk(x1, g, mod3, wq, keys_bf, S):
    T, D = x1.shape
    N = wq.shape[1]
    tt = 256
    per_b = S // tt
    HK = PEER_HEADS * PEER_TOPK
    return pl.pallas_call(
        _topk_kernel,
        grid=(T // tt,),
        in_specs=[pl.BlockSpec((tt, D), lambda i: (i, 0)),
                  pl.BlockSpec((1, D), lambda i: (0, 0)),
                  pl.BlockSpec((1, 1, D), lambda i: ((i // per_b) * 6 + 4, 0, 0)),
                  pl.BlockSpec((1, 1, D), lambda i: ((i // per_b) * 6 + 3, 0, 0)),
                  pl.BlockSpec((D, N), lambda i: (0, 0)),
                  pl.BlockSpec(keys_bf.shape, lambda i: (0, 0, 0, 0))],
        out_specs=[pl.BlockSpec((HK, tt), lambda i: (0, i)),
                   pl.BlockSpec((HK, tt), lambda i: (0, i))],
        out_shape=[jax.ShapeDtypeStruct((HK, T), jnp.int32),
                   jax.ShapeDtypeStruct((HK, T), F32)],
        compiler_params=_params(("parallel",), 40),
        name="pq_topk",
    )(x1, g, mod3, mod3, wq, keys_bf)


_PEER_TB = 128
_PEER_SUB = 8
_SUBLANES = 8
_PEER_CHUNKS = 32
_SC_LANES = 16
_SC_ROWS = 16


def _peer_u_kernel(idx_hbm, gate_ref, x1_ref, gn_ref, sc_ref, sh_ref, u_hbm, w_ref,
                   idx_smem, ub0, ub1, h_scr, sem_i, sem_u, *, step0):
    HK = PEER_HEADS * PEER_TOPK
    TB, SUB = _PEER_TB, _PEER_SUB
    R = SUB * HK
    N = TB * HK
    nsub = TB // SUB
    D = x1_ref.shape[1]
    nchunk = D // LANE
    tiles = HK // _SUBLANES
    i = pl.program_id(0)
    n = pl.num_programs(0)
    cur = lax.rem(i, 2) * N
    nxt = N - cur
    more = i + 1 < n
    ubufs = (ub0, ub1)

    def idx_copy(step, base):
        return pltpu.make_async_copy(idx_hbm.at[pl.ds((step0 + step) * N, N)],
                                     idx_smem.at[pl.ds(base, N)], sem_i)

    def issue_token(base, t, slot):
        for k in range(HK):
            e = idx_smem[base + t * HK + k]
            rt, s = t * tiles + k // _SUBLANES, k % _SUBLANES
            pltpu.make_async_copy(u_hbm.at[e], ubufs[slot].at[rt, :, s, :],
                                  sem_u.at[slot]).start(priority=k % 2)

    def wait(slot):
        pltpu.make_async_copy(ubufs[slot], ubufs[slot], sem_u.at[slot]).wait()

    @pl.when(i == 0)
    def _():
        first = idx_copy(0, 0)
        first.start()
        first.wait()
        for t in range(SUB):
            issue_token(0, t, 0)

    @pl.when(more)
    def _():
        idx_copy(i + 1, nxt).start()

    h_scr[...] = _rms(x1_ref[...], gn_ref[...]) * (1.0 + sc_ref[0]) + sh_ref[0]
    lane = lax.broadcasted_iota(jnp.int32, (HK, TB), 1)

    def compute_token(j, t, slot, wacc):
        ub = ubufs[slot]
        tok = j * SUB + t
        xt = h_scr[pl.ds(tok, 1), :]
        rows = slice(t * tiles, (t + 1) * tiles)
        part = ub[rows, 0].reshape(HK, LANE) * xt[:, 0:LANE]
        for c in range(1, nchunk):
            part = part + ub[rows, c].reshape(HK, LANE) * xt[:, c * LANE:(c + 1) * LANE]
        a = jnp.sum(part, axis=1, keepdims=True)
        hit = lane == tok
        g = jnp.sum(jnp.where(hit, gate_ref[...], 0.0), axis=1, keepdims=True)
        wgt = g * (0.5 * a * (1.0 + lax.erf(a * (2.0 ** -0.5))))
        return jnp.where(hit, wgt, wacc)

    def half(j, slot, next_base, wacc):
        wait(slot)
        for t in range(SUB):
            issue_token(next_base, t, 1 - slot)
            wacc = compute_token(j, t, slot, wacc)
        return wacc

    def pair(jj, wacc):
        j0 = 2 * jj
        wacc = half(j0, 0, cur + (j0 + 1) * R, wacc)
        last = jj == nsub // 2 - 1

        @pl.when(jnp.logical_and(last, more))
        def _():
            idx_copy(i + 1, nxt).wait()

        after = jnp.where(more, nxt, cur)
        return half(j0 + 1, 1, jnp.where(last, after, cur + (j0 + 2) * R), wacc)

    wacc = lax.fori_loop(0, nsub // 2, pair, jnp.zeros((HK, TB), F32))

    @pl.when(jnp.logical_not(more))
    def _():
        wait(0)

    w_ref[:, :HK] = jnp.zeros((TB, HK), F32)
    w_ref[:, HK:] = wacc.T


def _peer_u(idx_flat, gate_t, x1, g_norm, mod3, u3, S, step0, nsteps):
    T, D = x1.shape
    HK = PEER_HEADS * PEER_TOPK
    TB, SUB = _PEER_TB, _PEER_SUB
    per_b = S // TB
    modrow = lambda k: (lambda i: (((step0 + i) // per_b) * 6 + k, 0, 0))
    gbuf = pltpu.VMEM((SUB * HK // _SUBLANES, D // LANE, _SUBLANES, LANE), F32)
    return pl.pallas_call(
        functools.partial(_peer_u_kernel, step0=step0),
        grid=(nsteps,),
        in_specs=[pl.BlockSpec(memory_space=pl.ANY),
                  pl.BlockSpec((HK, TB), lambda i: (0, step0 + i)),
                  pl.BlockSpec((TB, D), lambda i: (step0 + i, 0)),
                  pl.BlockSpec((1, D), lambda i: (0, 0)),
                  pl.BlockSpec((1, 1, D), modrow(4)),
                  pl.BlockSpec((1, 1, D), modrow(3)),
                  pl.BlockSpec(memory_space=pl.ANY)],
        out_specs=pl.BlockSpec((TB, 2 * HK), lambda i: (i, 0)),
        out_shape=jax.ShapeDtypeStruct((nsteps * TB, 2 * HK), F32),
        scratch_shapes=[pltpu.SMEM((2 * TB * HK,), jnp.int32),
                        gbuf, gbuf,
                        pltpu.VMEM((TB, D), F32),
                        pltpu.SemaphoreType.DMA,
                        pltpu.SemaphoreType.DMA((2,))],
        compiler_params=_params(("arbitrary",), 40),
        name="peer_u",
    )(idx_flat, gate_t, x1, g_norm, mod3, mod3, u3)


def _sc_peer_v(v_tab, idx_flat, wgt, tok_base):
    E, nblk, _ = v_tab.shape
    D = nblk * LANE
    Tc, HK = wgt.shape[0], wgt.shape[1] // 2
    info = plsc.get_sparse_core_info()
    nw = info.num_cores * info.num_subcores
    tpw = Tc // nw
    CH = _SC_ROWS
    nch = HK // CH
    nsl = D // _SC_LANES
    mesh = plsc.VectorSubcoreMesh(core_axis_name="c", subcore_axis_name="s")

    @functools.partial(
        pl.kernel, mesh=mesh, out_type=jax.ShapeDtypeStruct((Tc, D), F32),
        scratch_types=[pltpu.VMEM((tpw * HK,), jnp.int32), pltpu.VMEM((2 * HK,), F32),
                       pltpu.VMEM((D,), F32),
                       pltpu.VMEM((CH, nblk, LANE), F32), pltpu.VMEM((CH, nblk, LANE), F32),
                       pltpu.SemaphoreType.DMA, pltpu.SemaphoreType.DMA],
        compiler_params=pltpu.CompilerParams(needs_layout_passes=False),
        name="sc_peer_v",
    )
    def k(tab_hbm, idx_hbm, w_hbm, o_hbm, idx_v, w_v, o_v, buf0, buf1, g0, g1):
        wid = lax.axis_index("s") * info.num_cores + lax.axis_index("c")
        tok0 = wid * tpw
        pltpu.sync_copy(idx_hbm.at[pl.ds((tok_base + tok0) * HK, tpw * HK)], idx_v)
        bufs, gs = (buf0, buf1), (g0, g1)

        def gather(g, b):
            return pltpu.make_async_copy(tab_hbm.at[idx_v.at[pl.ds(g * CH, CH)]], bufs[b], gs[b])

        gather(0, 0).start()

        @pl.loop(0, tpw)
        def _(t):
            pltpu.sync_copy(w_hbm.at[tok0 + t], w_v)

            @pl.loop(0, nsl)
            def _(c):
                o_v[pl.ds(pl.multiple_of(c * _SC_LANES, _SC_LANES), _SC_LANES)] = jnp.zeros((_SC_LANES,), F32)

            for ch in range(nch):
                b = ch % 2
                g = t * nch + ch
                gather(g, b).wait()

                @pl.when(g + 1 < tpw * nch)
                def _():
                    gather(g + 1, 1 - b).start()

                ws = [plsc.load_gather(w_v, [jnp.full((_SC_LANES,), HK + ch * CH + r, jnp.int32)])
                      for r in range(CH)]

                @pl.loop(0, nsl, step=2)
                def _(c):
                    for half in range(2):
                        off = pl.multiple_of((c + half) * _SC_LANES, _SC_LANES)
                        blk = (c + half) // (LANE // _SC_LANES)
                        lo = pl.multiple_of(off - blk * LANE, _SC_LANES)
                        parts = [ws[r] * bufs[b][r, blk, pl.ds(lo, _SC_LANES)] for r in range(CH)]
                        while len(parts) > 1:
                            parts = [parts[p] + parts[p + 1] for p in range(0, len(parts), 2)]
                        o_v[pl.ds(off, _SC_LANES)] = o_v[pl.ds(off, _SC_LANES)] + parts[0]

            pltpu.sync_copy(o_v, o_hbm.at[tok0 + t])

    return k(v_tab, idx_flat, wgt)


def _peer_fin_kernel(x_ref, p_ref, ga_ref, gf_ref, *rest):
    o_ref = rest[-1]
    o_ref[...] = _rms(x_ref[...] + ga_ref[0] * p_ref[...], gf_ref[...])


def _peer_fin(x1, po, mod3, g_final, S, tok_base, out_prev):
    T, D = x1.shape
    Tc = po.shape[0]
    tm = min(Tc, 512)
    per_b = S // tm
    blk0 = tok_base // tm
    in_specs = [pl.BlockSpec((tm, D), lambda i: (blk0 + i, 0)),
                pl.BlockSpec((tm, D), lambda i: (i, 0)),
                pl.BlockSpec((1, 1, D), lambda i: (((blk0 + i) // per_b) * 6 + 5, 0, 0)),
                pl.BlockSpec((1, D), lambda i: (0, 0))]
    args = [x1, po, mod3, g_final]
    aliases = {}
    if out_prev is not None:
        in_specs.append(pl.BlockSpec(memory_space=pl.ANY))
        args.append(out_prev)
        aliases = {4: 0}
    return pl.pallas_call(
        _peer_fin_kernel,
        grid=(Tc // tm,),
        in_specs=in_specs,
        out_specs=pl.BlockSpec((tm, D), lambda i: (blk0 + i, 0)),
        out_shape=jax.ShapeDtypeStruct((T, D), F32),
        input_output_aliases=aliases,
        compiler_params=_params(("arbitrary",), 40),
        name="peer_fin",
    )(*args)


def _pad_gate(w, lo):
    rank = w.shape[0]
    wh = w.reshape(rank, GLA_HEADS, GLA_DK).transpose(1, 0, 2)
    return jnp.zeros((GLA_HEADS, LANE, GLA_DK), F32).at[:, lo:lo + rank, :].set(wh)


def kernel(x, c, positions, w_ada, b_ada, g_norm_mix, w_in, w_gate_f, b_gate_f, w_gate_b, b_gate_b,
           g_gla_out, w_out, g_norm_ffn, w_peer_q, peer_sub_keys, peer_u, peer_v, g_final):
    B, S, D = x.shape
    T = B * S
    depth = w_ada.shape[0]
    assert depth == 1, "the final norm is fused into the last PEER call; one layer only"
    xt = x.reshape(T, D)
    cs, sn = _rope_tables(positions.reshape(T, 1))
    gz0 = 2 * GLA_HEADS * GLA_DK + 2 * GLA_HEADS * GLA_DV
    gz1 = gz0 + 2 * GLA_GATE_RANK
    for l in range(depth):
        mod3 = _ada(c, w_ada[l], b_ada[l]).reshape(B * 6, 1, D)
        w_main = jnp.concatenate([w_in[l][:, :gz0], w_in[l][:, gz1:]], axis=1).astype(BF16)
        w_z = jnp.pad(w_in[l][:, gz0:gz1], ((0, 0), (0, LANE - (gz1 - gz0)))).astype(BF16)
        proj, gz = _inproj(xt, g_norm_mix[l].reshape(1, D), mod3, w_main, w_z, S)
        o_gla = _gla(proj, gz,
                     _pad_gate(w_gate_f[l], 0), b_gate_f[l].reshape(GLA_HEADS, 1, GLA_DK),
                     _pad_gate(w_gate_b[l], GLA_GATE_RANK), b_gate_b[l].reshape(GLA_HEADS, 1, GLA_DK),
                     g_gla_out[l].reshape(GLA_HEADS, 1, GLA_DV), B, S)
        o_dil = _dil(proj, cs, sn, B, S)
        x1 = _outproj(o_gla, o_dil, w_out[l].astype(BF16), xt, mod3, S)
        idx_t, gate_t = _topk(x1, g_norm_ffn[l].reshape(1, D), mod3, w_peer_q[l].astype(BF16),
                              peer_sub_keys[l].astype(BF16), S)
        idx_flat = idx_t.T.reshape(-1)
        E = peer_u.shape[1]
        u3 = peer_u[l].reshape(E, D // LANE, LANE)
        v3 = peer_v[l].reshape(E, D // LANE, LANE)
        steps = T // _PEER_TB // _PEER_CHUNKS
        out, pending = None, []
        for ck in range(_PEER_CHUNKS + 2):
            if ck < _PEER_CHUNKS:
                wgt = _peer_u(idx_flat, gate_t, x1, g_norm_ffn[l].reshape(1, D), mod3, u3, S,
                              ck * steps, steps)
                pending.append(_sc_peer_v(v3, idx_flat, wgt, ck * steps * _PEER_TB))
            if ck >= 2:
                out = _peer_fin(x1, pending[ck - 2], mod3, g_final.reshape(1, D), S,
                                (ck - 2) * steps * _PEER_TB, out)
        xt = out
    return xt.reshape(B, S, D)
```
